```python
import jax, jax.numpy as jnp
from jax import lax
import numpy as np

D_MODEL = 1024
BATCH = 16
SEQ = 2048
DEPTH = 2

SB_HEAD_DIM = 64
SB_HEADS = D_MODEL // (2 * SB_HEAD_DIM)
SB_WIDTH = SB_HEADS * SB_HEAD_DIM
MLA_NOPE_DIM = 64
MLA_ROPE_DIM = 32
MLA_V_DIM = 64
MLA_HEADS = D_MODEL // (2 * MLA_V_DIM)
MLA_Q_RANK = 256
MLA_KV_RANK = 128
MLA_WIDTH = MLA_HEADS * MLA_V_DIM
MIX_WIDTH = SB_WIDTH + MLA_WIDTH
IN_WIDTH = 3 * SB_WIDTH + MLA_Q_RANK + MLA_KV_RANK + MLA_ROPE_DIM
IN_SPLITS = (SB_WIDTH, 2 * SB_WIDTH, 3 * SB_WIDTH,
             3 * SB_WIDTH + MLA_Q_RANK, 3 * SB_WIDTH + MLA_Q_RANK + MLA_KV_RANK)
ROPE_BASE = 10000.0
Q_BLOCK = 128
N_EXPERTS = 32
N_GROUPS = 8
EXPERTS_PER_GROUP = N_EXPERTS // N_GROUPS
TOP_K = 2
D_EXPERT = 256
DISPATCH_BLOCK = 256
DEEPNORM_ALPHA = (2 * DEPTH) ** 0.25
DEEPNORM_BETA = (8 * DEPTH) ** -0.25
LN_EPS = 1e-5
RMS_EPS = 1e-6

kernel_name = "hybrid_stickbreak_mla_groupmoe_deepnorm_adaln"


def standardize(x):
    xf = x.astype(jnp.float32)
    mu = jnp.mean(xf, axis=-1, keepdims=True)
    var = jnp.mean(jnp.square(xf - mu), axis=-1, keepdims=True)
    return (xf - mu) * lax.rsqrt(var + LN_EPS)


def layer_norm(x, gain, bias):
    return (standardize(x) * gain.astype(jnp.float32) + bias.astype(jnp.float32)).astype(x.dtype)


def modulate(x, shift, scale):
    return (standardize(x) * (1.0 + scale.astype(jnp.float32)) + shift.astype(jnp.float32)).astype(x.dtype)


def rms_norm(x, gain):
    xf = x.astype(jnp.float32)
    y = xf * lax.rsqrt(jnp.mean(jnp.square(xf), axis=-1, keepdims=True) + RMS_EPS)
    return (y * gain.astype(jnp.float32)).astype(x.dtype)


def rope_tables(positions):
    inv_freq = ROPE_BASE ** (-jnp.arange(0, MLA_ROPE_DIM, 2, dtype=jnp.float32) / MLA_ROPE_DIM)
    ang = positions.astype(jnp.float32)[..., None] * inv_freq
    return jnp.cos(ang), jnp.sin(ang)


def apply_rope(t, cos, sin):
    t1, t2 = jnp.split(t.astype(jnp.float32), 2, axis=-1)
    return jnp.concatenate([t1 * cos - t2 * sin, t1 * sin + t2 * cos], axis=-1).astype(t.dtype)


def split_heads(t, n_heads):
    b, s, _ = t.shape
    return t.reshape(b, s, n_heads, -1).transpose(0, 2, 1, 3)


def merge_heads(t):
    b, h, s, d = t.shape
    return t.transpose(0, 2, 1, 3).reshape(b, s, h * d)


def stick_breaking_attention(q, k, v):
    scale = SB_HEAD_DIM ** -0.5
    outs = []
    for blk in range(q.shape[2] // Q_BLOCK):
        q0 = blk * Q_BLOCK
        end = q0 + Q_BLOCK
        z = jnp.einsum('bhqd,bhkd->bhqk', q[:, :, q0:end].astype(jnp.float32),
                       k[:, :, :end].astype(jnp.float32)) * scale
        strict = jnp.arange(end)[None, :] < (q0 + jnp.arange(Q_BLOCK))[:, None]
        log_keep = jnp.where(strict, jax.nn.log_sigmoid(-z), 0.0)
        later = lax.cumsum(log_keep, axis=3, reverse=True) - log_keep
        w = jnp.where(strict, jnp.exp(jax.nn.log_sigmoid(z) + later), 0.0)
        outs.append(jnp.einsum('bhqk,bhkd->bhqd', w.astype(v.dtype), v[:, :, :end]))
    return jnp.concatenate(outs, axis=2)


def mla_attention(q_nope, q_rope, k_nope, k_rope, v):
    scale = (MLA_NOPE_DIM + MLA_ROPE_DIM) ** -0.5
    outs = []
    for blk in range(q_nope.shape[2] // Q_BLOCK):
        q0 = blk * Q_BLOCK
        end = q0 + Q_BLOCK
        s = (jnp.einsum('bhqd,bhkd->bhqk', q_nope[:, :, q0:end].astype(jnp.float32),
                        k_nope[:, :, :end].astype(jnp.float32))
             + jnp.einsum('bhqr,bkr->bhqk', q_rope[:, :, q0:end].astype(jnp.float32),
                          k_rope[:, :end].astype(jnp.float32))) * scale
        causal = jnp.arange(end)[None, :] <= (q0 + jnp.arange(Q_BLOCK))[:, None]
        p = jax.nn.softmax(jnp.where(causal, s, -jnp.inf), axis=-1)
        outs.append(jnp.einsum('bhqk,bhkd->bhqd', p.astype(v.dtype), v[:, :, :end]))
    return jnp.concatenate(outs, axis=2)


def token_mixer(h, cos, sin, w_in, q_norm, kv_norm, w_uq, w_ukv, w_o):
    proj = h @ w_in
    sb_q, sb_k, sb_v, q_lat, kv_lat, k_rope = jnp.split(proj, IN_SPLITS, axis=-1)
    sb_out = stick_breaking_attention(split_heads(sb_q, SB_HEADS), split_heads(sb_k, SB_HEADS),
                                      split_heads(sb_v, SB_HEADS))
    q = split_heads(rms_norm(q_lat, q_norm) @ w_uq, MLA_HEADS)
    kv = split_heads(rms_norm(kv_lat, kv_norm) @ w_ukv, MLA_HEADS)
    q_rope = apply_rope(q[..., MLA_NOPE_DIM:], cos[:, None], sin[:, None])
    k_rope = apply_rope(k_rope, cos, sin)
    mla_out = mla_attention(q[..., :MLA_NOPE_DIM], q_rope, kv[..., :MLA_NOPE_DIM], k_rope,
                            kv[..., MLA_NOPE_DIM:])
    merged = jnp.concatenate([merge_heads(sb_out), merge_heads(mla_out)], axis=-1)
    return merged @ w_o


def moe_ffn(h, router_w, router_bias, w_gate, w_up, w_down):
    b, s, d = h.shape
    x = h.reshape(-1, d)
    n_tok = x.shape[0]
    scores = jax.nn.sigmoid(x.astype(jnp.float32) @ router_w.astype(jnp.float32))
    biased = (scores + router_bias.astype(jnp.float32)).reshape(n_tok, N_GROUPS, EXPERTS_PER_GROUP)
    group_score = lax.top_k(biased, TOP_K)[0].sum(-1)
    g_sel = jnp.argmax(group_score, axis=-1)
    in_group = jnp.take_along_axis(biased, g_sel[:, None, None], axis=1)[:, 0]
    _, local = lax.top_k(in_group, TOP_K)
    expert_idx = g_sel[:, None] * EXPERTS_PER_GROUP + local
    gate = jnp.take_along_axis(scores, expert_idx, axis=-1)
    gate = gate / jnp.sum(gate, axis=-1, keepdims=True)
    m = n_tok * TOP_K
    flat_e = expert_idx.reshape(m)
    flat_tok = jnp.repeat(jnp.arange(n_tok, dtype=jnp.int32), TOP_K)
    flat_gate = gate.reshape(m)
    order = jnp.argsort(flat_e)
    se, stok, sgate = flat_e[order], flat_tok[order], flat_gate[order]
    counts = jnp.bincount(flat_e, length=N_EXPERTS)
    padded = (counts + DISPATCH_BLOCK - 1) // DISPATCH_BLOCK * DISPATCH_BLOCK
    start = jnp.cumsum(counts) - counts
    pend = jnp.cumsum(padded)
    pstart = pend - padded
    dest = pstart[se] + jnp.arange(m, dtype=jnp.int32) - start[se]
    n_blocks = -(-m // DISPATCH_BLOCK) + N_EXPERTS
    buf = jnp.zeros((n_blocks * DISPATCH_BLOCK, d), x.dtype).at[dest].set(x[stok])
    block_e = jnp.minimum(jnp.searchsorted(pend, jnp.arange(n_blocks) * DISPATCH_BLOCK, side='right'),
                          N_EXPERTS - 1)

    def expert_block(args):
        xb, e = args
        return (jax.nn.silu(xb @ w_gate[e]) * (xb @ w_up[e])) @ w_down[e]

    y_buf = lax.map(expert_block, (buf.reshape(n_blocks, DISPATCH_BLOCK, d), block_e))
    y_buf = y_buf.reshape(n_blocks * DISPATCH_BLOCK, d)
    y = jax.ops.segment_sum(y_buf[dest] * sgate[:, None].astype(x.dtype), stok, num_segments=n_tok)
    return y.reshape(b, s, d)


def setup_inputs(seed: int = 0) -> dict:
    key = jax.random.key(seed)
    ks = jax.random.split(key, 20)
    f32 = jnp.float32

    def nrm(k, shape, scale):
        return jax.random.normal(k, shape, f32) * scale

    d_in = D_MODEL ** -0.5
    x = nrm(ks[0], (BATCH, SEQ, D_MODEL), 1.0)
    c = nrm(ks[1], (BATCH, D_MODEL), 1.0)
    offset = jax.random.randint(ks[2], (BATCH, 1), 0, 1024, dtype=jnp.int32)
    positions = offset + jnp.arange(SEQ, dtype=jnp.int32)[None, :]
    ada_w = nrm(ks[3], (DEPTH, D_MODEL, 6 * D_MODEL), 0.1 * d_in)
    ada_b = nrm(ks[4], (DEPTH, 6 * D_MODEL), 0.01)
    in_scale = jnp.concatenate([
        jnp.full((2 * SB_WIDTH,), d_in, f32),
        jnp.full((SB_WIDTH,), d_in * DEEPNORM_BETA, f32),
        jnp.full((MLA_Q_RANK + MLA_KV_RANK + MLA_ROPE_DIM,), d_in, f32)])
    w_in = nrm(ks[5], (DEPTH, D_MODEL, IN_WIDTH), 1.0) * in_scale
    q_norm = 1.0 + nrm(ks[6], (DEPTH, MLA_Q_RANK), 0.01)
    kv_norm = 1.0 + nrm(ks[7], (DEPTH, MLA_KV_RANK), 0.01)
    w_uq = nrm(ks[8], (DEPTH, MLA_Q_RANK, MLA_HEADS * (MLA_NOPE_DIM + MLA_ROPE_DIM)), MLA_Q_RANK ** -0.5)
    ukv_scale = jnp.tile(jnp.concatenate([jnp.ones((MLA_NOPE_DIM,), f32),
                                          jnp.full((MLA_V_DIM,), DEEPNORM_BETA, f32)]),
                         MLA_HEADS) * (MLA_KV_RANK ** -0.5)
    w_ukv = nrm(ks[9], (DEPTH, MLA_KV_RANK, MLA_HEADS * (MLA_NOPE_DIM + MLA_V_DIM)), 1.0) * ukv_scale
    w_o = nrm(ks[10], (DEPTH, MIX_WIDTH, D_MODEL), MIX_WIDTH ** -0.5 * DEEPNORM_BETA)
    ln1_g = 1.0 + nrm(ks[11], (DEPTH, D_MODEL), 0.01)
    ln1_b = nrm(ks[12], (DEPTH, D_MODEL), 0.01)
    router_w = nrm(ks[13], (D_MODEL, N_EXPERTS), d_in)
    router_bias = nrm(ks[14], (N_EXPERTS,), 0.01)
    w_gate = nrm(ks[15], (DEPTH, N_EXPERTS, D_MODEL, D_EXPERT), d_in * DEEPNORM_BETA)
    w_up = nrm(ks[16], (DEPTH, N_EXPERTS, D_MODEL, D_EXPERT), d_in * DEEPNORM_BETA)
    w_down = nrm(ks[17], (DEPTH, N_EXPERTS, D_EXPERT, D_MODEL), D_EXPERT ** -0.5 * DEEPNORM_BETA)
    ln2_g = 1.0 + nrm(ks[18], (DEPTH, D_MODEL), 0.01)
    ln2_b = nrm(ks[19], (DEPTH, D_MODEL), 0.01)
    return {"x": x, "c": c, "positions": positions, "ada_w": ada_w, "ada_b": ada_b,
            "w_in": w_in, "q_norm": q_norm, "kv_norm": kv_norm, "w_uq": w_uq, "w_ukv": w_ukv,
            "w_o": w_o, "ln1_g": ln1_g, "ln1_b": ln1_b, "router_w": router_w,
            "router_bias": router_bias, "w_gate": w_gate, "w_up": w_up, "w_down": w_down,
            "ln2_g": ln2_g, "ln2_b": ln2_b}


def reference(x, c, positions, ada_w, ada_b, w_in, q_norm, kv_norm, w_uq, w_ukv, w_o,
              ln1_g, ln1_b, router_w, router_bias, w_gate, w_up, w_down, ln2_g, ln2_b):
    cos, sin = rope_tables(positions)
    c_act = jax.nn.silu(c)
    for l in range(DEPTH):
        mod = (c_act @ ada_w[l] + ada_b[l])[:, None, :]
        shift1, scale1, gate1, shift2, scale2, gate2 = jnp.split(mod, 6, axis=-1)
        h = modulate(x, shift1, scale1)
        mix = token_mixer(h, cos, sin, w_in[l], q_norm[l], kv_norm[l], w_uq[l], w_ukv[l], w_o[l])
        x = layer_norm(DEEPNORM_ALPHA * x + (1.0 + gate1) * mix, ln1_g[l], ln1_b[l])
        h = modulate(x, shift2, scale2)
        ffn = moe_ffn(h, router_w, router_bias, w_gate[l], w_up[l], w_down[l])
        x = layer_norm(DEEPNORM_ALPHA * x + (1.0 + gate2) * ffn, ln2_g[l], ln2_b[l])
    return x
```

```python
import functools

import jax
import jax.numpy as jnp
from jax import lax
from jax.experimental import pallas as pl
from jax.experimental.pallas import tpu as pltpu

F32 = jnp.float32
BF16 = jnp.bfloat16

D_MODEL = 1024
DEPTH = 2
SB_HEADS = 8
SB_HEAD_DIM = 64
SB_WIDTH = SB_HEADS * SB_HEAD_DIM
MLA_HEADS = 8
MLA_NOPE_DIM = 64
MLA_ROPE_DIM = 32
MLA_V_DIM = 64
MLA_Q_RANK = 256
MLA_KV_RANK = 128
MLA_WIDTH = MLA_HEADS * MLA_V_DIM
ROPE_BASE = 10000.0
N_EXPERTS = 32
N_GROUPS = 8
EXPERTS_PER_GROUP = 4
TOP_K = 2
D_EXPERT = 256
DISPATCH_BLOCK = 256
DEEPNORM_ALPHA = (2 * DEPTH) ** 0.25
LN_EPS = 1e-5
RMS_EPS = 1e-6

LANES = 128
HALF_ROPE = MLA_ROPE_DIM // 2
ROPE_LANE0 = MLA_NOPE_DIM
VMEM_LIMIT = 56 * 1024 * 1024

TOKEN_TILE = 512
ATT_TILE = 256
DMA_TILE = 256


def _dot(a, b):
    return jnp.dot(a, b, preferred_element_type=F32)


def _dot_nt(a, b):
    return lax.dot_general(a, b, (((1,), (1,)), ((), ())), preferred_element_type=F32)


def _split_bf16(v):
    hi = v.astype(BF16)
    lo = (v - hi.astype(F32)).astype(BF16)
    return hi, lo


def _standardize(x):
    mu = jnp.mean(x, axis=-1, keepdims=True)
    xc = x - mu
    var = jnp.mean(xc * xc, axis=-1, keepdims=True)
    return xc * lax.rsqrt(var + LN_EPS)


def _ada_kernel(c_ref, w_ref, b_ref, o_ref):
    c = c_ref[...]
    ca = c * jax.nn.sigmoid(c)
    ca_hi, ca_lo = _split_bf16(ca)
    w_hi, w_lo = _split_bf16(w_ref[...])
    o_ref[...] = _dot(ca_hi, w_hi) + _dot(ca_lo, w_hi) + _dot(ca_hi, w_lo) + b_ref[...]


def _ada_call(c, ada_w, ada_b):
    depth, d, n6 = ada_w.shape
    b = c.shape[0]
    tn = 1536
    return pl.pallas_call(
        _ada_kernel,
        grid=(depth, n6 // tn),
        in_specs=[pl.BlockSpec((b, d), lambda l, j: (0, 0)),
                  pl.BlockSpec((None, d, tn), lambda l, j: (l, 0, j)),
                  pl.BlockSpec((None, 1, tn), lambda l, j: (l, 0, j))],
        out_specs=pl.BlockSpec((None, b, tn), lambda l, j: (l, 0, j)),
        out_shape=jax.ShapeDtypeStruct((depth, b, n6), F32),
        compiler_params=pltpu.CompilerParams(vmem_limit_bytes=VMEM_LIMIT),
        name="ada",
    )(c, ada_w, ada_b.reshape(depth, 1, n6))


def _rope_kernel(pos_ref, invf_ref, cos_ref, sin_ref):
    ang = pos_ref[...].astype(F32) * invf_ref[...]
    lane = lax.broadcasted_iota(jnp.int32, ang.shape, 1)
    rot = (lane >= ROPE_LANE0) & (lane < ROPE_LANE0 + MLA_ROPE_DIM)
    cos_ref[...] = jnp.where(rot, jnp.cos(ang), 1.0)
    sin_ref[...] = jnp.where(rot, jnp.sin(ang), 0.0)


def _rope_call(positions):
    n = positions.size
    tm = TOKEN_TILE
    inv_freq = ROPE_BASE ** (-jnp.arange(0, MLA_ROPE_DIM, 2, dtype=F32) / MLA_ROPE_DIM)
    invf = jnp.zeros((1, LANES), F32)
    invf = invf.at[0, ROPE_LANE0:ROPE_LANE0 + HALF_ROPE].set(inv_freq)
    invf = invf.at[0, ROPE_LANE0 + HALF_ROPE:ROPE_LANE0 + MLA_ROPE_DIM].set(inv_freq)
    return pl.pallas_call(
        _rope_kernel,
        grid=(n // tm,),
        in_specs=[pl.BlockSpec((tm, 1), lambda i: (i, 0)),
                  pl.BlockSpec((1, LANES), lambda i: (0, 0))],
        out_specs=[pl.BlockSpec((tm, LANES), lambda i: (i, 0)),
                   pl.BlockSpec((tm, LANES), lambda i: (i, 0))],
        out_shape=[jax.ShapeDtypeStruct((n, LANES), F32)] * 2,
        name="rope_tables",
    )(positions.reshape(n, 1), invf)


def _proj_kernel(x_ref, mod_ref, cos_ref, sin_ref, w1_ref, w2_ref, qn_ref, kvn_ref, wuq_ref,
                 wukv_ref, sbq_ref, sbk_ref, sbv_ref, mq_ref, mk_ref, mv_ref):
    mod = mod_ref[...]
    h = _standardize(x_ref[...]) * (1.0 + mod[1:2]) + mod[0:1]
    hb = h.astype(BF16)

    p1 = _dot(hb, w1_ref[...])
    sbq_ref[...] = (p1[:, :SB_WIDTH] * (SB_HEAD_DIM ** -0.5)).astype(BF16)
    sbk_ref[...] = p1[:, SB_WIDTH:2 * SB_WIDTH].astype(BF16)
    sbv_ref[...] = p1[:, 2 * SB_WIDTH:].astype(BF16)

    p2 = _dot(hb, w2_ref[...])
    q_lat = p2[:, :MLA_Q_RANK]
    kv_lat = p2[:, MLA_Q_RANK:MLA_Q_RANK + MLA_KV_RANK]
    k_rope = p2[:, MLA_Q_RANK + MLA_KV_RANK:]

    qn = q_lat * lax.rsqrt(jnp.mean(q_lat * q_lat, axis=-1, keepdims=True) + RMS_EPS) * qn_ref[...]
    kvn = kv_lat * lax.rsqrt(jnp.mean(kv_lat * kv_lat, axis=-1, keepdims=True) + RMS_EPS) * kvn_ref[...]
    q = _dot(qn.astype(BF16), wuq_ref[...])
    kv = _dot(kvn.astype(BF16), wukv_ref[...])

    cos = cos_ref[...]
    sin = sin_ref[...]
    lane = lax.broadcasted_iota(jnp.int32, cos.shape, 1)
    second = lane >= ROPE_LANE0 + HALF_ROPE
    sin_up = jnp.where(second, sin, 0.0)
    sin_dn = jnp.where(second, 0.0, -sin)

    def rope(t):
        return (t * cos + pltpu.roll(t, HALF_ROPE, 1) * sin_up
                + pltpu.roll(t, LANES - HALF_ROPE, 1) * sin_dn)

    kr = rope(k_rope)
    mla_scale = (MLA_NOPE_DIM + MLA_ROPE_DIM) ** -0.5
    for hd in range(MLA_HEADS):
        sl = slice(hd * LANES, (hd + 1) * LANES)
        mq_ref[:, sl] = (rope(q[:, sl]) * mla_scale).astype(BF16)
        mk_ref[:, sl] = (kv[:, sl] + kr).astype(BF16)
    mv_ref[...] = kv[:, MLA_HEADS * LANES:].astype(BF16)


def _proj_call(x2d, mod, cos_t, sin_t, w1, w2, qn, kvn, wuq, wukv, seq):
    n, d = x2d.shape
    tm = TOKEN_TILE
    per_b = seq // tm
    tok = lambda i: (i, 0)
    full = lambda i: (0, 0)
    widths = (SB_WIDTH, SB_WIDTH, SB_WIDTH, MLA_HEADS * LANES, MLA_HEADS * LANES, MLA_WIDTH)
    return pl.pallas_call(
        _proj_kernel,
        grid=(n // tm,),
        in_specs=[pl.BlockSpec((tm, d), tok),
                  pl.BlockSpec((None, 6, d), lambda i: (i // per_b, 0, 0)),
                  pl.BlockSpec((tm, LANES), tok),
                  pl.BlockSpec((tm, LANES), tok),
                  pl.BlockSpec(w1.shape, full),
                  pl.BlockSpec(w2.shape, full),
                  pl.BlockSpec(qn.shape, full),
                  pl.BlockSpec(kvn.shape, full),
                  pl.BlockSpec(wuq.shape, full),
                  pl.BlockSpec(wukv.shape, full)],
        out_specs=[pl.BlockSpec((tm, w), tok) for w in widths],
        out_shape=[jax.ShapeDtypeStruct((n, w), BF16) for w in widths],
        compiler_params=pltpu.CompilerParams(vmem_limit_bytes=VMEM_LIMIT),
        name="proj",
    )(x2d, mod, cos_t, sin_t, w1, w2, qn, kvn, wuq, wukv)


def _sb_kernel(q_ref, k_ref, v_ref, tri_ref, o_ref):
    i = pl.program_id(2)
    t = ATT_TILE
    q = q_ref[...]
    tri = tri_ref[...]
    lane = lax.broadcasted_iota(jnp.int32, (1, LANES), 1)
    row = lax.broadcasted_iota(jnp.int32, (t, t), 0)
    col = lax.broadcasted_iota(jnp.int32, (t, t), 1)
    strict = col < row

    def tile(qh, j, carry, acc, diagonal):
        k = k_ref[pl.ds(j * t, t), :]
        v = v_ref[pl.ds(j * t, t), :]
        z = _dot_nt(qh, k)
        log_beta = jnp.minimum(z, 0.0) - jnp.log1p(jnp.exp(-jnp.abs(z)))
        log_keep = log_beta - z
        if diagonal:
            log_keep = jnp.where(strict, log_keep, 0.0)
        hi, lo = _split_bf16(log_keep)
        later = _dot(hi, tri) + _dot(lo, tri) + carry
        w = jnp.exp(log_beta + later)
        if diagonal:
            w = jnp.where(strict, w, 0.0)
        acc = acc + _dot(w.astype(BF16), v)
        carry = carry + jnp.sum(log_keep, axis=1, keepdims=True)
        return carry, acc

    out = jnp.zeros((t, LANES), F32)
    for hd in range(2):
        mine = (lane >= hd * SB_HEAD_DIM) & (lane < (hd + 1) * SB_HEAD_DIM)
        qh = jnp.where(mine, q, jnp.zeros_like(q))
        carry, acc = tile(qh, i, jnp.zeros((t, 1), F32), jnp.zeros((t, LANES), F32), True)

        def body(s, state, qh=qh):
            return tile(qh, i - 1 - s, state[0], state[1], False)

        carry, acc = lax.fori_loop(0, i, body, (carry, acc))
        out = jnp.where(mine, acc, out)
    o_ref[...] = out.astype(o_ref.dtype)


def _sb_call(q, k, v, tri, batch, seq):
    t = ATT_TILE
    nq = seq // t
    pairs = SB_WIDTH // LANES
    kv_spec = pl.BlockSpec((seq, LANES), lambda b, p, i: (b, p))
    return pl.pallas_call(
        _sb_kernel,
        grid=(batch, pairs, nq),
        in_specs=[pl.BlockSpec((t, LANES), lambda b, p, i: (b * nq + i, p)),
                  kv_spec, kv_spec,
                  pl.BlockSpec((t, t), lambda b, p, i: (0, 0))],
        out_specs=pl.BlockSpec((t, LANES), lambda b, p, i: (b * nq + i, p)),
        out_shape=jax.ShapeDtypeStruct(q.shape, BF16),
        compiler_params=pltpu.CompilerParams(vmem_limit_bytes=VMEM_LIMIT),
        name="sb_attention",
    )(q, k, v, tri)


def _mla_kernel(q_ref, k_ref, v_ref, o_ref):
    i = pl.program_id(2)
    t = ATT_TILE
    lane = lax.broadcasted_iota(jnp.int32, (1, LANES), 1)
    row = lax.broadcasted_iota(jnp.int32, (t, t), 0)
    col = lax.broadcasted_iota(jnp.int32, (t, t), 1)
    causal = col <= row

    def tile(qh, hd, j, m, l, acc, diagonal):
        k = k_ref[pl.ds(j * t, t), hd * LANES:(hd + 1) * LANES]
        v = v_ref[pl.ds(j * t, t), :]
        s = _dot_nt(qh, k)
        if diagonal:
            s = jnp.where(causal, s, -jnp.inf)
        m_new = jnp.maximum(m, jnp.max(s, axis=1, keepdims=True))
        a = jnp.exp(m - m_new)
        p = jnp.exp(s - m_new)
        l = a * l + jnp.sum(p, axis=1, keepdims=True)
        acc = a * acc + _dot(p.astype(BF16), v)
        return m_new, l, acc

    out = jnp.zeros((t, LANES), F32)
    for hd in range(2):
        qh = q_ref[:, hd * LANES:(hd + 1) * LANES]
        state = tile(qh, hd, i, jnp.full((t, 1), -jnp.inf, F32), jnp.zeros((t, 1), F32),
                     jnp.zeros((t, LANES), F32), True)

        def body(j, st, qh=qh, hd=hd):
            return tile(qh, hd, j, st[0], st[1], st[2], False)

        m, l, acc = lax.fori_loop(0, i, body, state)
        mine = (lane >= hd * MLA_V_DIM) & (lane < (hd + 1) * MLA_V_DIM)
        out = jnp.where(mine, acc / l, out)
    o_ref[...] = out.astype(o_ref.dtype)


def _mla_call(q, k, v, batch, seq):
    t = ATT_TILE
    nq = seq // t
    pairs = MLA_WIDTH // LANES
    return pl.pallas_call(
        _mla_kernel,
        grid=(batch, pairs, nq),
        in_specs=[pl.BlockSpec((t, 2 * LANES), lambda b, p, i: (b * nq + i, p)),
                  pl.BlockSpec((seq, 2 * LANES), lambda b, p, i: (b, p)),
                  pl.BlockSpec((seq, LANES), lambda b, p, i: (b, p))],
        out_specs=pl.BlockSpec((t, LANES), lambda b, p, i: (b * nq + i, p)),
        out_shape=jax.ShapeDtypeStruct(v.shape, BF16),
        compiler_params=pltpu.CompilerParams(vmem_limit_bytes=VMEM_LIMIT),
        name="mla_attention",
    )(q, k, v)


def _top2_sum(a, b, c, d):
    hi1, lo1 = jnp.maximum(a, b), jnp.minimum(a, b)
    hi2, lo2 = jnp.maximum(c, d), jnp.minimum(c, d)
    return jnp.maximum(hi1, hi2) + jnp.maximum(jnp.minimum(hi1, hi2), jnp.maximum(lo1, lo2))


def _first_argmax4(v):
    m = jnp.maximum(jnp.maximum(v[0], v[1]), jnp.maximum(v[2], v[3]))
    return jnp.where(v[0] == m, 0, jnp.where(v[1] == m, 1, jnp.where(v[2] == m, 2, 3)))


def _pick4(idx, v):
    return jnp.where(idx == 0, v[0], jnp.where(idx == 1, v[1], jnp.where(idx == 2, v[2], v[3])))


def _post_kernel(sb_ref, mla_ref, x_ref, mod_ref, wo_sb_ref, wo_mla_ref, g_ref, b_ref, rw_ref,
                 rb_ref, ut_ref, x1_ref, h2_ref, route_ref, gate_ref, cnt_ref, run_ref):
    i = pl.program_id(0)
    tm = x_ref.shape[0]

    @pl.when(i == 0)
    def _():
        run_ref[...] = jnp.zeros_like(run_ref)

    mod = mod_ref[...]
    mix = _dot(sb_ref[...], wo_sb_ref[...]) + _dot(mla_ref[...], wo_mla_ref[...])
    x1 = _standardize(DEEPNORM_ALPHA * x_ref[...] + (1.0 + mod[2:3]) * mix) * g_ref[...] + b_ref[...]
    x1_ref[...] = x1
    h2 = _standardize(x1) * (1.0 + mod[4:5]) + mod[3:4]
    h2_ref[...] = h2

    h_hi, h_lo = _split_bf16(h2)
    rw = rw_ref[...]
    big = _dot(h_hi, rw)
    logits = big[:, :LANES] + big[:, LANES:] + _dot(h_lo, rw[:, :LANES])
    lt = logits.T[:N_EXPERTS]
    scores = jax.nn.sigmoid(lt)
    biased = scores + rb_ref[...]
    sc = [scores[N_GROUPS * p:N_GROUPS * (p + 1)] for p in range(EXPERTS_PER_GROUP)]
    bi = [biased[N_GROUPS * p:N_GROUPS * (p + 1)] for p in range(EXPERTS_PER_GROUP)]

    group_score = _top2_sum(*bi)
    gidx = lax.broadcasted_iota(jnp.int32, group_score.shape, 0)
    best = jnp.max(group_score, axis=0, keepdims=True)
    g_sel = jnp.min(jnp.where(group_score == best, gidx, N_GROUPS), axis=0, keepdims=True)
    in_sel = gidx == g_sel
    vb = [jnp.sum(jnp.where(in_sel, b, 0.0), axis=0, keepdims=True) for b in bi]
    vs = [jnp.sum(jnp.where(in_sel, s, 0.0), axis=0, keepdims=True) for s in sc]
    l1 = _first_argmax4(vb)
    vb2 = [jnp.where(l1 == p, -jnp.inf, vb[p]) for p in range(EXPERTS_PER_GROUP)]
    l2 = _first_argmax4(vb2)
    s1 = _pick4(l1, vs)
    s2 = _pick4(l2, vs)
    tot = s1 + s2
    e1 = g_sel * EXPERTS_PER_GROUP + l1
    e2 = g_sel * EXPERTS_PER_GROUP + l2

    eidx = lax.broadcasted_iota(jnp.int32, (N_EXPERTS, tm), 0)
    hit1 = eidx == e1
    hit2 = eidx == e2
    onehot = jnp.where(hit1 | hit2, 1.0, 0.0)
    before = _dot(onehot.astype(BF16), ut_ref[...]) + run_ref[:, 0:1]
    r1 = jnp.sum(jnp.where(hit1, before, 0.0), axis=0, keepdims=True).astype(jnp.int32)
    r2 = jnp.sum(jnp.where(hit2, before, 0.0), axis=0, keepdims=True).astype(jnp.int32)
    run_ref[...] = run_ref[...] + jnp.sum(onehot, axis=1, keepdims=True)
    cnt_ref[...] = run_ref[...].astype(jnp.int32)

    r8 = lax.broadcasted_iota(jnp.int32, (8, tm), 0)
    route_ref[...] = jnp.where(r8 == 0, e1, jnp.where(r8 == 1, e2, jnp.where(r8 == 2, r1, jnp.where(r8 == 3, r2, 0))))
    r128 = lax.broadcasted_iota(jnp.int32, (LANES, tm), 0)
    gate_rows = jnp.where(r128 == 0, s1 / tot, jnp.where(r128 == 1, s2 / tot, 0.0))
    gate_ref[...] = gate_rows.T


def _post_call(sb_out, mla_out, x2d, mod, wo_sb, wo_mla, g, b, rw, rb, ut, seq):
    n, d = x2d.shape
    tm = TOKEN_TILE
    per_b = seq // tm
    tok = lambda i: (i, 0)
    full = lambda i: (0, 0)
    return pl.pallas_call(
        _post_kernel,
        grid=(n // tm,),
        in_specs=[pl.BlockSpec((tm, SB_WIDTH), tok),
                  pl.BlockSpec((tm, MLA_WIDTH), tok),
                  pl.BlockSpec((tm, d), tok),
                  pl.BlockSpec((None, 6, d), lambda i: (i // per_b, 0, 0)),
                  pl.BlockSpec(wo_sb.shape, full),
                  pl.BlockSpec(wo_mla.shape, full),
                  pl.BlockSpec((1, d), full),
                  pl.BlockSpec((1, d), full),
                  pl.BlockSpec(rw.shape, full),
                  pl.BlockSpec(rb.shape, full),
                  pl.BlockSpec(ut.shape, full)],
        out_specs=[pl.BlockSpec((tm, d), tok),
                   pl.BlockSpec((tm, d), tok),
                   pl.BlockSpec((8, tm), lambda i: (0, i)),
                   pl.BlockSpec((tm, LANES), tok),
                   pl.BlockSpec((N_EXPERTS, LANES), full)],
        out_shape=[jax.ShapeDtypeStruct((n, d), F32),
                   jax.ShapeDtypeStruct((n, d), F32),
                   jax.ShapeDtypeStruct((8, n), jnp.int32),
                   jax.ShapeDtypeStruct((n, LANES), F32),
                   jax.ShapeDtypeStruct((N_EXPERTS, LANES), jnp.int32)],
        scratch_shapes=[pltpu.VMEM((N_EXPERTS, LANES), F32)],
        compiler_params=pltpu.CompilerParams(dimension_semantics=("arbitrary",),
                                             vmem_limit_bytes=VMEM_LIMIT),
        name="post_attention",
    )(sb_out, mla_out, x2d, mod, wo_sb, wo_mla, g, b, rw, rb, ut)


def _row_copy(src_ref, src_row, dst_ref, dst_row, sem):
    return pltpu.make_async_copy(src_ref.at[pl.ds(src_row, 1), :], dst_ref.at[pl.ds(dst_row, 1), :], sem)


def _dispatch_kernel(dest_ref, h_ref, buf_in_ref, buf_ref, sem):
    del buf_in_ref
    tg = h_ref.shape[0]

    def start(t, c):
        for k in range(TOP_K):
            _row_copy(h_ref, t, buf_ref, dest_ref[k, t], sem).start()
        return c

    lax.fori_loop(0, tg, start, 0)

    def wait(t, c):
        for k in range(TOP_K):
            _row_copy(h_ref, 0, buf_ref, 0, sem).wait()
        return c

    lax.fori_loop(0, tg, wait, 0)


def _dispatch_call(dest3, h2, buf_zero):
    n, d = h2.shape
    tg = DMA_TILE
    return pl.pallas_call(
        _dispatch_kernel,
        grid=(n // tg,),
        in_specs=[pl.BlockSpec((None, TOP_K, tg), lambda i: (i, 0, 0), memory_space=pltpu.SMEM),
                  pl.BlockSpec((tg, d), lambda i: (i, 0)),
                  pl.BlockSpec(memory_space=pl.ANY)],
        out_specs=pl.BlockSpec(memory_space=pl.ANY),
        out_shape=jax.ShapeDtypeStruct(buf_zero.shape, buf_zero.dtype),
        scratch_shapes=[pltpu.SemaphoreType.DMA(())],
        input_output_aliases={2: 0},
        compiler_params=pltpu.CompilerParams(dimension_semantics=("arbitrary",)),
        name="dispatch",
    )(dest3, h2, buf_zero)


def _ffn_kernel(be_ref, nb_ref, x_ref, wg_ref, wu_ref, wd_ref, y_ref, wg_s, wu_s, wd_s):
    i = pl.program_id(0)
    prev = be_ref[jnp.maximum(i - 1, 0)]

    @pl.when((i == 0) | (be_ref[i] != prev))
    def _():
        wg_s[...] = wg_ref[...].astype(BF16)
        wu_s[...] = wu_ref[...].astype(BF16)
        wd_s[...] = wd_ref[...].astype(BF16)

    @pl.when(i < nb_ref[0])
    def _():
        xb = x_ref[...].astype(BF16)
        g = _dot(xb, wg_s[...])
        u = _dot(xb, wu_s[...])
        a = g * jax.nn.sigmoid(g) * u
        y_ref[...] = _dot(a.astype(BF16), wd_s[...])

    @pl.when(i >= nb_ref[0])
    def _():
        y_ref[...] = jnp.zeros_like(y_ref)


def _ffn_call(block_e, n_used, buf, w_gate, w_up, w_down):
    rows, d = buf.shape
    bm = DISPATCH_BLOCK
    de = w_gate.shape[-1]
    grid_spec = pltpu.PrefetchScalarGridSpec(
        num_scalar_prefetch=2,
        grid=(rows // bm,),
        in_specs=[pl.BlockSpec((bm, d), lambda i, be, nb: (jnp.minimum(i, nb[0] - 1), 0)),
                  pl.BlockSpec((None, d, de), lambda i, be, nb: (be[i], 0, 0)),
                  pl.BlockSpec((None, d, de), lambda i, be, nb: (be[i], 0, 0)),
                  pl.BlockSpec((None, de, d), lambda i, be, nb: (be[i], 0, 0))],
        out_specs=pl.BlockSpec((bm, d), lambda i, be, nb: (i, 0)),
        scratch_shapes=[pltpu.VMEM((d, de), BF16), pltpu.VMEM((d, de), BF16), pltpu.VMEM((de, d), BF16)],
    )
    return pl.pallas_call(
        _ffn_kernel,
        grid_spec=grid_spec,
        out_shape=jax.ShapeDtypeStruct((rows, d), F32),
        compiler_params=pltpu.CompilerParams(dimension_semantics=("arbitrary",),
                                             vmem_limit_bytes=VMEM_LIMIT),
        name="expert_ffn",
    )(block_e, n_used, buf, w_gate, w_up, w_down)


def _combine_kernel(dest_ref, x1_ref, gate_ref, mod_ref, g_ref, b_ref, y_hbm_ref, o_ref, rows_ref, sem):
    tc = x1_ref.shape[0]

    def start(t, c):
        for k in range(TOP_K):
            _row_copy(y_hbm_ref, dest_ref[k, t], rows_ref.at[k], t, sem).start()
        return c

    lax.fori_loop(0, tc, start, 0)

    def wait(t, c):
        for k in range(TOP_K):
            _row_copy(y_hbm_ref, 0, rows_ref.at[k], 0, sem).wait()
        return c

    lax.fori_loop(0, tc, wait, 0)

    gate = gate_ref[...]
    ffn = gate[:, 0:1] * rows_ref[0] + gate[:, 1:2] * rows_ref[1]
    mod = mod_ref[...]
    o_ref[...] = _standardize(DEEPNORM_ALPHA * x1_ref[...] + (1.0 + mod[5:6]) * ffn) * g_ref[...] + b_ref[...]


def _combine_call(dest3, x1, gates, mod, g, b, y_buf, seq):
    n, d = x1.shape
    tc = DMA_TILE
    per_b = seq // tc
    tok = lambda i: (i, 0)
    full = lambda i: (0, 0)
    return pl.pallas_call(
        _combine_kernel,
        grid=(n // tc,),
        in_specs=[pl.BlockSpec((None, TOP_K, tc), lambda i: (i, 0, 0), memory_space=pltpu.SMEM),
                  pl.BlockSpec((tc, d), tok),
                  pl.BlockSpec((tc, LANES), tok),
                  pl.BlockSpec((None, 6, d), lambda i: (i // per_b, 0, 0)),
                  pl.BlockSpec((1, d), full),
                  pl.BlockSpec((1, d), full),
                  pl.BlockSpec(memory_space=pl.ANY)],
        out_specs=pl.BlockSpec((tc, d), tok),
        out_shape=jax.ShapeDtypeStruct((n, d), F32),
        scratch_shapes=[pltpu.VMEM((TOP_K, tc, d), F32), pltpu.SemaphoreType.DMA(())],
        compiler_params=pltpu.CompilerParams(dimension_semantics=("arbitrary",)),
        name="combine",
    )(dest3, x1, gates, mod, g, b, y_buf)


def _layer_weights(w_in, w_uq, w_ukv, w_o):
    d = w_in.shape[0]
    w1 = w_in[:, :3 * SB_WIDTH].astype(BF16)
    lat = 3 * SB_WIDTH + MLA_Q_RANK + MLA_KV_RANK
    w2 = jnp.concatenate([w_in[:, 3 * SB_WIDTH:lat], jnp.zeros((d, ROPE_LANE0), F32), w_in[:, lat:],
                          jnp.zeros((d, LANES - ROPE_LANE0 - MLA_ROPE_DIM), F32)], axis=1).astype(BF16)
    uq = w_uq.reshape(MLA_Q_RANK, MLA_HEADS, MLA_NOPE_DIM + MLA_ROPE_DIM)
    uq = jnp.pad(uq, ((0, 0), (0, 0), (0, LANES - MLA_NOPE_DIM - MLA_ROPE_DIM)))
    wuq = uq.reshape(MLA_Q_RANK, MLA_HEADS * LANES).astype(BF16)
    ukv = w_ukv.reshape(MLA_KV_RANK, MLA_HEADS, MLA_NOPE_DIM + MLA_V_DIM)
    uk = jnp.pad(ukv[:, :, :MLA_NOPE_DIM], ((0, 0), (0, 0), (0, LANES - MLA_NOPE_DIM)))
    wukv = jnp.concatenate([uk.reshape(MLA_KV_RANK, MLA_HEADS * LANES),
                            ukv[:, :, MLA_NOPE_DIM:].reshape(MLA_KV_RANK, MLA_WIDTH)], axis=1).astype(BF16)
    return w1, w2, wuq, wukv, w_o[:SB_WIDTH].astype(BF16), w_o[SB_WIDTH:].astype(BF16)


def _router_weights(router_w, router_bias):
    d = router_w.shape[0]
    rw = router_w.reshape(d, N_GROUPS, EXPERTS_PER_GROUP).transpose(0, 2, 1).reshape(d, N_EXPERTS)
    hi = rw.astype(BF16)
    lo = (rw - hi.astype(F32)).astype(BF16)
    pad = jnp.zeros((d, LANES - N_EXPERTS), BF16)
    rwcat = jnp.concatenate([hi, pad, lo, pad], axis=1)
    rb = router_bias.reshape(N_GROUPS, EXPERTS_PER_GROUP).T.reshape(N_EXPERTS, 1)
    return rwcat, rb


def _dispatch_plan(route, counts, n_blocks):
    bm = DISPATCH_BLOCK
    cnt = counts[:, 0]
    padded = (cnt + bm - 1) // bm * bm
    pend = jnp.cumsum(padded)
    pstart = pend - padded
    eids = jnp.arange(N_EXPERTS, dtype=jnp.int32)
    start_of = jnp.sum(jnp.where(route[:TOP_K, :, None] == eids, pstart, 0), axis=-1)
    dest = start_of + route[TOP_K:2 * TOP_K]
    block_row = jnp.arange(n_blocks, dtype=jnp.int32) * bm
    block_e = jnp.minimum(jnp.sum(block_row[:, None] >= pend[None, :], axis=1), N_EXPERTS - 1)
    n_used = (pend[-1] // bm).reshape(1)
    return dest.astype(jnp.int32), block_e.astype(jnp.int32), n_used.astype(jnp.int32)


def kernel(x, c, positions, ada_w, ada_b, w_in, q_norm, kv_norm, w_uq, w_ukv, w_o, ln1_g, ln1_b,
           router_w, router_bias, w_gate, w_up, w_down, ln2_g, ln2_b):
    batch, seq, d = x.shape
    n = batch * seq
    depth = ada_w.shape[0]
    assert seq % TOKEN_TILE == 0 and seq % ATT_TILE == 0 and seq % DMA_TILE == 0

    mod_all = _ada_call(c, ada_w, ada_b).reshape(depth, batch, 6, d)
    cos_t, sin_t = _rope_call(positions)
    rwcat, rb = _router_weights(router_w, router_bias)
    t = ATT_TILE
    tri = (jnp.arange(t)[:, None] > jnp.arange(t)[None, :]).astype(BF16)
    tm = TOKEN_TILE
    ut = (jnp.arange(tm)[:, None] < jnp.arange(tm)[None, :]).astype(BF16)
    n_blocks = -(-(n * TOP_K) // DISPATCH_BLOCK) + N_EXPERTS
    buf_zero = jnp.zeros((n_blocks * DISPATCH_BLOCK, d), F32)

    x2d = x.reshape(n, d)
    for l in range(depth):
        mod = mod_all[l]
        w1, w2, wuq, wukv, wo_sb, wo_mla = _layer_weights(w_in[l], w_uq[l], w_ukv[l], w_o[l])
        sbq, sbk, sbv, mq, mk, mv = _proj_call(x2d, mod, cos_t, sin_t, w1, w2, q_norm[l].reshape(1, -1),
                                               kv_norm[l].reshape(1, -1), wuq, wukv, seq)
        sb_out = _sb_call(sbq, sbk, sbv, tri, batch, seq)
        mla_out = _mla_call(mq, mk, mv, batch, seq)
        x1, h2, route, gates, counts = _post_call(sb_out, mla_out, x2d, mod, wo_sb, wo_mla,
                                                  ln1_g[l].reshape(1, d), ln1_b[l].reshape(1, d),
                                                  rwcat, rb, ut, seq)
        dest, block_e, n_used = _dispatch_plan(route, counts, n_blocks)
        dest3 = dest.reshape(TOP_K, n // DMA_TILE, DMA_TILE).transpose(1, 0, 2)
        buf = _dispatch_call(dest3, h2, buf_zero)
        y_buf = _ffn_call(block_e, n_used, buf, w_gate[l], w_up[l], w_down[l])
        x2d = _combine_call(dest3, x1, gates, mod, ln2_g[l].reshape(1, d), ln2_b[l].reshape(1, d), y_buf, seq)
    return x2d.reshape(batch, seq, d)
```

```python
import functools

import jax
import jax.numpy as jnp
from jax import lax
from jax.experimental import pallas as pl
from jax.experimental.pallas import tpu as pltpu

F32 = jnp.float32
BF16 = jnp.bfloat16

D_MODEL = 1024
DEPTH = 2
SB_HEADS = 8
SB_HEAD_DIM = 64
SB_WIDTH = SB_HEADS * SB_HEAD_DIM
MLA_HEADS = 8
MLA_NOPE_DIM = 64
MLA_ROPE_DIM = 32
MLA_V_DIM = 64
MLA_Q_RANK = 256
MLA_KV_RANK = 128
MLA_WIDTH = MLA_HEADS * MLA_V_DIM
ROPE_BASE = 10000.0
N_EXPERTS = 32
N_GROUPS = 8
EXPERTS_PER_GROUP = 4
TOP_K = 2
D_EXPERT = 256
DISPATCH_BLOCK = 256
DEEPNORM_ALPHA = (2 * DEPTH) ** 0.25
LN_EPS = 1e-5
RMS_EPS = 1e-6
LOG2_E = 1.4426950408889634

LANES = 128
HALF_ROPE = MLA_ROPE_DIM // 2
ROPE_LANE0 = MLA_NOPE_DIM
VMEM_LIMIT = 56 * 1024 * 1024

TOKEN_TILE = 512
ATT_TILE = 256
DMA_TILE = 256


def _dot(a, b):
    return jnp.dot(a, b, preferred_element_type=F32)


def _dot_nt(a, b):
    return lax.dot_general(a, b, (((1,), (1,)), ((), ())), preferred_element_type=F32)


def _split_bf16(v):
    hi = v.astype(BF16)
    lo = (v - hi.astype(F32)).astype(BF16)
    return hi, lo


def _standardize(x):
    mu = jnp.mean(x, axis=-1, keepdims=True)
    xc = x - mu
    var = jnp.mean(xc * xc, axis=-1, keepdims=True)
    return xc * lax.rsqrt(var + LN_EPS)


def _ada_kernel(c_ref, w_ref, b_ref, o_ref):
    c = c_ref[...]
    ca = c * jax.nn.sigmoid(c)
    ca_hi, ca_lo = _split_bf16(ca)
    w_hi, w_lo = _split_bf16(w_ref[...])
    o_ref[...] = _dot(ca_hi, w_hi) + _dot(ca_lo, w_hi) + _dot(ca_hi, w_lo) + b_ref[...]


def _ada_call(c, ada_w, ada_b):
    depth, d, n6 = ada_w.shape
    b = c.shape[0]
    tn = 1536
    return pl.pallas_call(
        _ada_kernel,
        grid=(depth, n6 // tn),
        in_specs=[pl.BlockSpec((b, d), lambda l, j: (0, 0)),
                  pl.BlockSpec((None, d, tn), lambda l, j: (l, 0, j)),
                  pl.BlockSpec((None, 1, tn), lambda l, j: (l, 0, j))],
        out_specs=pl.BlockSpec((None, b, tn), lambda l, j: (l, 0, j)),
        out_shape=jax.ShapeDtypeStruct((depth, b, n6), F32),
        compiler_params=pltpu.CompilerParams(vmem_limit_bytes=VMEM_LIMIT),
        name="ada",
    )(c, ada_w, ada_b.reshape(depth, 1, n6))


def _rope_kernel(pos_ref, invf_ref, cos_ref, sin_ref):
    ang = pos_ref[...].astype(F32) * invf_ref[...]
    lane = lax.broadcasted_iota(jnp.int32, ang.shape, 1)
    rot = (lane >= ROPE_LANE0) & (lane < ROPE_LANE0 + MLA_ROPE_DIM)
    cos_ref[...] = jnp.where(rot, jnp.cos(ang), 1.0)
    sin_ref[...] = jnp.where(rot, jnp.sin(ang), 0.0)


def _rope_call(positions):
    n = positions.size
    tm = TOKEN_TILE
    inv_freq = ROPE_BASE ** (-jnp.arange(0, MLA_ROPE_DIM, 2, dtype=F32) / MLA_ROPE_DIM)
    invf = jnp.zeros((1, LANES), F32)
    invf = invf.at[0, ROPE_LANE0:ROPE_LANE0 + HALF_ROPE].set(inv_freq)
    invf = invf.at[0, ROPE_LANE0 + HALF_ROPE:ROPE_LANE0 + MLA_ROPE_DIM].set(inv_freq)
    return pl.pallas_call(
        _rope_kernel,
        grid=(n // tm,),
        in_specs=[pl.BlockSpec((tm, 1), lambda i: (i, 0)),
                  pl.BlockSpec((1, LANES), lambda i: (0, 0))],
        out_specs=[pl.BlockSpec((tm, LANES), lambda i: (i, 0)),
                   pl.BlockSpec((tm, LANES), lambda i: (i, 0))],
        out_shape=[jax.ShapeDtypeStruct((n, LANES), F32)] * 2,
        name="rope_tables",
    )(positions.reshape(n, 1), invf)


def _proj_kernel(x_ref, mod_ref, cos_ref, sin_ref, w1_ref, w2_ref, qn_ref, kvn_ref, wuq_ref,
                 wukv_ref, sbq_ref, sbk_ref, sbv_ref, mq_ref, mk_ref, mv_ref):
    mod = mod_ref[...]
    h = _standardize(x_ref[...]) * (1.0 + mod[1:2]) + mod[0:1]
    hb = h.astype(BF16)

    p1 = _dot(hb, w1_ref[...])
    sbq_ref[...] = (p1[:, :SB_WIDTH] * (SB_HEAD_DIM ** -0.5 * LOG2_E)).astype(BF16)
    sbk_ref[...] = p1[:, SB_WIDTH:2 * SB_WIDTH].astype(BF16)
    sbv_ref[...] = p1[:, 2 * SB_WIDTH:].astype(BF16)

    p2 = _dot(hb, w2_ref[...])
    q_lat = p2[:, :MLA_Q_RANK]
    kv_lat = p2[:, MLA_Q_RANK:MLA_Q_RANK + MLA_KV_RANK]
    k_rope = p2[:, MLA_Q_RANK + MLA_KV_RANK:]

    qn = q_lat * lax.rsqrt(jnp.mean(q_lat * q_lat, axis=-1, keepdims=True) + RMS_EPS) * qn_ref[...]
    kvn = kv_lat * lax.rsqrt(jnp.mean(kv_lat * kv_lat, axis=-1, keepdims=True) + RMS_EPS) * kvn_ref[...]
    q = _dot(qn.astype(BF16), wuq_ref[...])
    kv = _dot(kvn.astype(BF16), wukv_ref[...])

    cos = cos_ref[...]
    sin = sin_ref[...]
    lane = lax.broadcasted_iota(jnp.int32, cos.shape, 1)
    second = lane >= ROPE_LANE0 + HALF_ROPE
    sin_up = jnp.where(second, sin, 0.0)
    sin_dn = jnp.where(second, 0.0, -sin)

    def rope(t):
        return (t * cos + pltpu.roll(t, HALF_ROPE, 1) * sin_up
                + pltpu.roll(t, LANES - HALF_ROPE, 1) * sin_dn)

    kr = rope(k_rope)
    mla_scale = (MLA_NOPE_DIM + MLA_ROPE_DIM) ** -0.5 * LOG2_E
    for hd in range(MLA_HEADS):
        sl = slice(hd * LANES, (hd + 1) * LANES)
        mq_ref[:, sl] = (rope(q[:, sl]) * mla_scale).astype(BF16)
        mk_ref[:, sl] = (kv[:, sl] + kr).astype(BF16)
    mv_ref[...] = kv[:, MLA_HEADS * LANES:].astype(BF16)


def _proj_call(x2d, mod, cos_t, sin_t, w1, w2, qn, kvn, wuq, wukv, seq):
    n, d = x2d.shape
    tm = TOKEN_TILE
    per_b = seq // tm
    tok = lambda i: (i, 0)
    full = lambda i: (0, 0)
    widths = (SB_WIDTH, SB_WIDTH, SB_WIDTH, MLA_HEADS * LANES, MLA_HEADS * LANES, MLA_WIDTH)
    return pl.pallas_call(
        _proj_kernel,
        grid=(n // tm,),
        in_specs=[pl.BlockSpec((tm, d), tok),
                  pl.BlockSpec((None, 6, d), lambda i: (i // per_b, 0, 0)),
                  pl.BlockSpec((tm, LANES), tok),
                  pl.BlockSpec((tm, LANES), tok),
                  pl.BlockSpec(w1.shape, full),
                  pl.BlockSpec(w2.shape, full),
                  pl.BlockSpec(qn.shape, full),
                  pl.BlockSpec(kvn.shape, full),
                  pl.BlockSpec(wuq.shape, full),
                  pl.BlockSpec(wukv.shape, full)],
        out_specs=[pl.BlockSpec((tm, w), tok) for w in widths],
        out_shape=[jax.ShapeDtypeStruct((n, w), BF16) for w in widths],
        compiler_params=pltpu.CompilerParams(vmem_limit_bytes=VMEM_LIMIT),
        name="proj",
    )(x2d, mod, cos_t, sin_t, w1, w2, qn, kvn, wuq, wukv)


def _causal_tiles(i, step, state):
    state = step((i,), state, True)
    odd = i % 2
    state = lax.cond(odd == 1, lambda s: step((i - 1,), s, False), lambda s: s, state)
    top = i - odd - 1

    def pair(p, s):
        j = top - 2 * p
        return step((j, j - 1), s, False)

    return lax.fori_loop(0, (i - odd) // 2, pair, state)


def _sb_kernel(q_ref, k_ref, v_ref, tri_ref, o_ref):
    i = pl.program_id(2)
    t = ATT_TILE
    q = q_ref[...]
    tri2 = tri_ref[...]
    lane = lax.broadcasted_iota(jnp.int32, (1, LANES), 1)
    row = lax.broadcasted_iota(jnp.int32, (t, t), 0)
    col = lax.broadcasted_iota(jnp.int32, (t, t), 1)
    strict = col < row
    mine = [lane < SB_HEAD_DIM, lane >= SB_HEAD_DIM]
    qh = [jnp.where(m, q, jnp.zeros_like(q)) for m in mine]

    def head_tile(qh_h, k, v, ncarry, acc, diagonal):
        z = _dot_nt(qh_h, k)
        e = jnp.exp2(-jnp.abs(z))
        nlk = jnp.maximum(z, 0.0) + jnp.log(1.0 + e) * LOG2_E
        if diagonal:
            nlk = jnp.where(strict, nlk, 0.0)
        hi = lax.bitcast_convert_type(lax.bitcast_convert_type(nlk, jnp.uint32) & jnp.uint32(0xFFFF0000), F32)
        pieces = jnp.concatenate([hi.astype(BF16), (nlk - hi).astype(BF16)], axis=1)
        w = jnp.exp2((z - ncarry) - _dot(pieces, tri2))
        if diagonal:
            w = jnp.where(strict, w, 0.0)
        acc = acc + _dot(w.astype(BF16), v)
        ncarry = ncarry + jnp.sum(nlk, axis=1, keepdims=True)
        return ncarry, acc

    def step(js, state, diagonal):
        for j in js:
            k = k_ref[pl.ds(j * t, t), :]
            v = v_ref[pl.ds(j * t, t), :]
            state = tuple(head_tile(qh[h], k, v, state[h][0], state[h][1], diagonal) for h in range(2))
        return state

    init = tuple((jnp.zeros((t, 1), F32), jnp.zeros((t, LANES), F32)) for _ in range(2))
    state = _causal_tiles(i, step, init)
    o_ref[...] = jnp.where(mine[0], state[0][1], state[1][1]).astype(o_ref.dtype)


def _sb_call(q, k, v, tri, batch, seq):
    t = ATT_TILE
    nq = seq // t
    pairs = SB_WIDTH // LANES
    kv_spec = pl.BlockSpec((seq, LANES), lambda b, p, i: (b, p))
    return pl.pallas_call(
        _sb_kernel,
        grid=(batch, pairs, nq),
        in_specs=[pl.BlockSpec((t, LANES), lambda b, p, i: (b * nq + i, p)),
                  kv_spec, kv_spec,
                  pl.BlockSpec((2 * t, t), lambda b, p, i: (0, 0))],
        out_specs=pl.BlockSpec((t, LANES), lambda b, p, i: (b * nq + i, p)),
        out_shape=jax.ShapeDtypeStruct(q.shape, BF16),
        compiler_params=pltpu.CompilerParams(vmem_limit_bytes=VMEM_LIMIT),
        name="sb_attention",
    )(q, k, v, tri)


def _mla_kernel(q_ref, k_ref, v_ref, o_ref):
    i = pl.program_id(2)
    t = ATT_TILE
    lane = lax.broadcasted_iota(jnp.int32, (1, LANES), 1)
    row = lax.broadcasted_iota(jnp.int32, (t, t), 0)
    col = lax.broadcasted_iota(jnp.int32, (t, t), 1)
    causal = col <= row
    qh = [q_ref[:, hd * LANES:(hd + 1) * LANES] for hd in range(2)]

    def head_tiles(hd, js, m, l, acc, diagonal):
        ss = []
        for j in js:
            s = _dot_nt(qh[hd], k_ref[pl.ds(j * t, t), hd * LANES:(hd + 1) * LANES])
            ss.append(jnp.where(causal, s, -jnp.inf) if diagonal else s)
        m_new = m
        for s in ss:
            m_new = jnp.maximum(m_new, jnp.max(s, axis=1, keepdims=True))
        a = jnp.exp2(m - m_new)
        l = a * l
        acc = a * acc
        for j, s in zip(js, ss):
            p = jnp.exp2(s - m_new)
            l = l + jnp.sum(p, axis=1, keepdims=True)
            acc = acc + _dot(p.astype(BF16), v_ref[pl.ds(j * t, t), :])
        return m_new, l, acc

    def step(js, state, diagonal):
        return tuple(head_tiles(hd, js, *state[hd], diagonal) for hd in range(2))

    init = tuple((jnp.full((t, 1), -jnp.inf, F32), jnp.zeros((t, 1), F32), jnp.zeros((t, LANES), F32))
                 for _ in range(2))
    state = _causal_tiles(i, step, init)
    out = jnp.where(lane < MLA_V_DIM, state[0][2] / state[0][1], state[1][2] / state[1][1])
    o_ref[...] = out.astype(o_ref.dtype)


def _mla_call(q, k, v, batch, seq):
    t = ATT_TILE
    nq = seq // t
    pairs = MLA_WIDTH // LANES
    return pl.pallas_call(
        _mla_kernel,
        grid=(batch, pairs, nq),
        in_specs=[pl.BlockSpec((t, 2 * LANES), lambda b, p, i: (b * nq + i, p)),
                  pl.BlockSpec((seq, 2 * LANES), lambda b, p, i: (b, p)),
                  pl.BlockSpec((seq, LANES), lambda b, p, i: (b, p))],
        out_specs=pl.BlockSpec((t, LANES), lambda b, p, i: (b * nq + i, p)),
        out_shape=jax.ShapeDtypeStruct(v.shape, BF16),
        compiler_params=pltpu.CompilerParams(vmem_limit_bytes=VMEM_LIMIT),
        name="mla_attention",
    )(q, k, v)


def _top2_sum(a, b, c, d):
    hi1, lo1 = jnp.maximum(a, b), jnp.minimum(a, b)
    hi2, lo2 = jnp.maximum(c, d), jnp.minimum(c, d)
    return jnp.maximum(hi1, hi2) + jnp.maximum(jnp.minimum(hi1, hi2), jnp.maximum(lo1, lo2))


def _first_argmax4(v):
    m = jnp.maximum(jnp.maximum(v[0], v[1]), jnp.maximum(v[2], v[3]))
    return jnp.where(v[0] == m, 0, jnp.where(v[1] == m, 1, jnp.where(v[2] == m, 2, 3)))


def _pick4(idx, v):
    return jnp.where(idx == 0, v[0], jnp.where(idx == 1, v[1], jnp.where(idx == 2, v[2], v[3])))


def _post_kernel(sb_ref, mla_ref, x_ref, mod_ref, wo_sb_ref, wo_mla_ref, g_ref, b_ref, rw_ref,
                 rb_ref, ut_ref, x1_ref, h2_ref, route_ref, gate_ref, cnt_ref, run_ref):
    i = pl.program_id(0)
    tm = x_ref.shape[0]

    @pl.when(i == 0)
    def _():
        run_ref[...] = jnp.zeros_like(run_ref)

    mod = mod_ref[...]
    mix = _dot(sb_ref[...], wo_sb_ref[...]) + _dot(mla_ref[...], wo_mla_ref[...])
    x1 = _standardize(DEEPNORM_ALPHA * x_ref[...] + (1.0 + mod[2:3]) * mix) * g_ref[...] + b_ref[...]
    x1_ref[...] = x1
    h2 = _standardize(x1) * (1.0 + mod[4:5]) + mod[3:4]
    h2_ref[...] = h2

    h_hi, h_lo = _split_bf16(h2)
    rw = rw_ref[...]
    big = _dot(h_hi, rw)
    logits = big[:, :LANES] + big[:, LANES:] + _dot(h_lo, rw[:, :LANES])
    lt = logits.T[:N_EXPERTS]
    scores = jax.nn.sigmoid(lt)
    biased = scores + rb_ref[...]
    sc = [scores[N_GROUPS * p:N_GROUPS * (p + 1)] for p in range(EXPERTS_PER_GROUP)]
    bi = [biased[N_GROUPS * p:N_GROUPS * (p + 1)] for p in range(EXPERTS_PER_GROUP)]

    group_score = _top2_sum(*bi)
    gidx = lax.broadcasted_iota(jnp.int32, group_score.shape, 0)
    best = jnp.max(group_score, axis=0, keepdims=True)
    g_sel = jnp.min(jnp.where(group_score == best, gidx, N_GROUPS), axis=0, keepdims=True)
    in_sel = gidx == g_sel
    vb = [jnp.sum(jnp.where(in_sel, b, 0.0), axis=0, keepdims=True) for b in bi]
    vs = [jnp.sum(jnp.where(in_sel, s, 0.0), axis=0, keepdims=True) for s in sc]
    l1 = _first_argmax4(vb)
    vb2 = [jnp.where(l1 == p, -jnp.inf, vb[p]) for p in range(EXPERTS_PER_GROUP)]
    l2 = _first_argmax4(vb2)
    s1 = _pick4(l1, vs)
    s2 = _pick4(l2, vs)
    tot = s1 + s2
    e1 = g_sel * EXPERTS_PER_GROUP + l1
    e2 = g_sel * EXPERTS_PER_GROUP + l2

    eidx = lax.broadcasted_iota(jnp.int32, (N_EXPERTS, tm), 0)
    hit1 = eidx == e1
    hit2 = eidx == e2
    onehot = jnp.where(hit1 | hit2, 1.0, 0.0)
    before = _dot(onehot.astype(BF16), ut_ref[...]) + run_ref[:, 0:1]
    r1 = jnp.sum(jnp.where(hit1, before, 0.0), axis=0, keepdims=True).astype(jnp.int32)
    r2 = jnp.sum(jnp.where(hit2, before, 0.0), axis=0, keepdims=True).astype(jnp.int32)
    run_ref[...] = run_ref[...] + jnp.sum(onehot, axis=1, keepdims=True)
    cnt_ref[...] = run_ref[...].astype(jnp.int32)

    r8 = lax.broadcasted_iota(jnp.int32, (8, tm), 0)
    route_ref[...] = jnp.where(r8 == 0, e1, jnp.where(r8 == 1, e2, jnp.where(r8 == 2, r1, jnp.where(r8 == 3, r2, 0))))
    r128 = lax.broadcasted_iota(jnp.int32, (LANES, tm), 0)
    gate_rows = jnp.where(r128 == 0, s1 / tot, jnp.where(r128 == 1, s2 / tot, 0.0))
    gate_ref[...] = gate_rows.T


def _post_call(sb_out, mla_out, x2d, mod, wo_sb, wo_mla, g, b, rw, rb, ut, seq):
    n, d = x2d.shape
    tm = TOKEN_TILE
    per_b = seq // tm
    tok = lambda i: (i, 0)
    full = lambda i: (0, 0)
    return pl.pallas_call(
        _post_kernel,
        grid=(n // tm,),
        in_specs=[pl.BlockSpec((tm, SB_WIDTH), tok),
                  pl.BlockSpec((tm, MLA_WIDTH), tok),
                  pl.BlockSpec((tm, d), tok),
                  pl.BlockSpec((None, 6, d), lambda i: (i // per_b, 0, 0)),
                  pl.BlockSpec(wo_sb.shape, full),
                  pl.BlockSpec(wo_mla.shape, full),
                  pl.BlockSpec((1, d), full),
                  pl.BlockSpec((1, d), full),
                  pl.BlockSpec(rw.shape, full),
                  pl.BlockSpec(rb.shape, full),
                  pl.BlockSpec(ut.shape, full)],
        out_specs=[pl.BlockSpec((tm, d), tok),
                   pl.BlockSpec((tm, d), tok),
                   pl.BlockSpec((8, tm), lambda i: (0, i)),
                   pl.BlockSpec((tm, LANES), tok),
                   pl.BlockSpec((N_EXPERTS, LANES), full)],
        out_shape=[jax.ShapeDtypeStruct((n, d), F32),
                   jax.ShapeDtypeStruct((n, d), F32),
                   jax.ShapeDtypeStruct((8, n), jnp.int32),
                   jax.ShapeDtypeStruct((n, LANES), F32),
                   jax.ShapeDtypeStruct((N_EXPERTS, LANES), jnp.int32)],
        scratch_shapes=[pltpu.VMEM((N_EXPERTS, LANES), F32)],
        compiler_params=pltpu.CompilerParams(dimension_semantics=("arbitrary",),
                                             vmem_limit_bytes=VMEM_LIMIT),
        name="post_attention",
    )(sb_out, mla_out, x2d, mod, wo_sb, wo_mla, g, b, rw, rb, ut)


def _row_copy(src_ref, src_row, dst_ref, dst_row, sem):
    return pltpu.make_async_copy(src_ref.at[pl.ds(src_row, 1), :], dst_ref.at[pl.ds(dst_row, 1), :], sem)


def _dispatch_kernel(dest_ref, h_ref, buf_in_ref, buf_ref, sem):
    del buf_in_ref
    tg = h_ref.shape[0]

    def start(t, c):
        for k in range(TOP_K):
            _row_copy(h_ref, t, buf_ref, dest_ref[k, t], sem).start()
        return c

    lax.fori_loop(0, tg, start, 0)

    def wait(t, c):
        for k in range(TOP_K):
            _row_copy(h_ref, 0, buf_ref, 0, sem).wait()
        return c

    lax.fori_loop(0, tg, wait, 0)


def _dispatch_call(dest3, h2, buf_zero):
    n, d = h2.shape
    tg = DMA_TILE
    return pl.pallas_call(
        _dispatch_kernel,
        grid=(n // tg,),
        in_specs=[pl.BlockSpec((None, TOP_K, tg), lambda i: (i, 0, 0), memory_space=pltpu.SMEM),
                  pl.BlockSpec((tg, d), lambda i: (i, 0)),
                  pl.BlockSpec(memory_space=pl.ANY)],
        out_specs=pl.BlockSpec(memory_space=pl.ANY),
        out_shape=jax.ShapeDtypeStruct(buf_zero.shape, buf_zero.dtype),
        scratch_shapes=[pltpu.SemaphoreType.DMA(())],
        input_output_aliases={2: 0},
        compiler_params=pltpu.CompilerParams(dimension_semantics=("arbitrary",)),
        name="dispatch",
    )(dest3, h2, buf_zero)


def _ffn_kernel(be_ref, nb_ref, x_ref, wg_ref, wu_ref, wd_ref, y_ref, wg_s, wu_s, wd_s):
    i = pl.program_id(0)
    prev = be_ref[jnp.maximum(i - 1, 0)]

    @pl.when((i == 0) | (be_ref[i] != prev))
    def _():
        wg_s[...] = wg_ref[...].astype(BF16)
        wu_s[...] = wu_ref[...].astype(BF16)
        wd_s[...] = wd_ref[...].astype(BF16)

    @pl.when(i < nb_ref[0])
    def _():
        xb = x_ref[...].astype(BF16)
        g = _dot(xb, wg_s[...])
        u = _dot(xb, wu_s[...])
        a = g * jax.nn.sigmoid(g) * u
        y_ref[...] = _dot(a.astype(BF16), wd_s[...])

    @pl.when(i >= nb_ref[0])
    def _():
        y_ref[...] = jnp.zeros_like(y_ref)


def _ffn_call(block_e, n_used, buf, w_gate, w_up, w_down):
    rows, d = buf.shape
    bm = DISPATCH_BLOCK
    de = w_gate.shape[-1]
    grid_spec = pltpu.PrefetchScalarGridSpec(
        num_scalar_prefetch=2,
        grid=(rows // bm,),
        in_specs=[pl.BlockSpec((bm, d), lambda i, be, nb: (jnp.minimum(i, nb[0] - 1), 0)),
                  pl.BlockSpec((None, d, de), lambda i, be, nb: (be[i], 0, 0)),
                  pl.BlockSpec((None, d, de), lambda i, be, nb: (be[i], 0, 0)),
                  pl.BlockSpec((None, de, d), lambda i, be, nb: (be[i], 0, 0))],
        out_specs=pl.BlockSpec((bm, d), lambda i, be, nb: (i, 0)),
        scratch_shapes=[pltpu.VMEM((d, de), BF16), pltpu.VMEM((d, de), BF16), pltpu.VMEM((de, d), BF16)],
    )
    return pl.pallas_call(
        _ffn_kernel,
        grid_spec=grid_spec,
        out_shape=jax.ShapeDtypeStruct((rows, d), F32),
        compiler_params=pltpu.CompilerParams(dimension_semantics=("arbitrary",),
                                             vmem_limit_bytes=VMEM_LIMIT),
        name="expert_ffn",
    )(block_e, n_used, buf, w_gate, w_up, w_down)


def _combine_kernel(dest_ref, x1_ref, gate_ref, mod_ref, g_ref, b_ref, y_hbm_ref, o_ref, rows_ref, sem):
    tc = x1_ref.shape[0]

    def start(t, c):
        for k in range(TOP_K):
            _row_copy(y_hbm_ref, dest_ref[k, t], rows_ref.at[k], t, sem).start()
        return c

    lax.fori_loop(0, tc, start, 0)

    def wait(t, c):
        for k in range(TOP_K):
            _row_copy(y_hbm_ref, 0, rows_ref.at[k], 0, sem).wait()
        return c

    lax.fori_loop(0, tc, wait, 0)

    gate = gate_ref[...]
    ffn = gate[:, 0:1] * rows_ref[0] + gate[:, 1:2] * rows_ref[1]
    mod = mod_ref[...]
    o_ref[...] = _standardize(DEEPNORM_ALPHA * x1_ref[...] + (1.0 + mod[5:6]) * ffn) * g_ref[...] + b_ref[...]


def _combine_call(dest3, x1, gates, mod, g, b, y_buf, seq):
    n, d = x1.shape
    tc = DMA_TILE
    per_b = seq // tc
    tok = lambda i: (i, 0)
    full = lambda i: (0, 0)
    return pl.pallas_call(
        _combine_kernel,
        grid=(n // tc,),
        in_specs=[pl.BlockSpec((None, TOP_K, tc), lambda i: (i, 0, 0), memory_space=pltpu.SMEM),
                  pl.BlockSpec((tc, d), tok),
                  pl.BlockSpec((tc, LANES), tok),
                  pl.BlockSpec((None, 6, d), lambda i: (i // per_b, 0, 0)),
                  pl.BlockSpec((1, d), full),
                  pl.BlockSpec((1, d), full),
                  pl.BlockSpec(memory_space=pl.ANY)],
        out_specs=pl.BlockSpec((tc, d), tok),
        out_shape=jax.ShapeDtypeStruct((n, d), F32),
        scratch_shapes=[pltpu.VMEM((TOP_K, tc, d), F32), pltpu.SemaphoreType.DMA(())],
        compiler_params=pltpu.CompilerParams(dimension_semantics=("arbitrary",)),
        name="combine",
    )(dest3, x1, gates, mod, g, b, y_buf)


def _layer_weights(w_in, w_uq, w_ukv, w_o):
    d = w_in.shape[0]
    w1 = w_in[:, :3 * SB_WIDTH].astype(BF16)
    lat = 3 * SB_WIDTH + MLA_Q_RANK + MLA_KV_RANK
    w2 = jnp.concatenate([w_in[:, 3 * SB_WIDTH:lat], jnp.zeros((d, ROPE_LANE0), F32), w_in[:, lat:],
                          jnp.zeros((d, LANES - ROPE_LANE0 - MLA_ROPE_DIM), F32)], axis=1).astype(BF16)
    uq = w_uq.reshape(MLA_Q_RANK, MLA_HEADS, MLA_NOPE_DIM + MLA_ROPE_DIM)
    uq = jnp.pad(uq, ((0, 0), (0, 0), (0, LANES - MLA_NOPE_DIM - MLA_ROPE_DIM)))
    wuq = uq.reshape(MLA_Q_RANK, MLA_HEADS * LANES).astype(BF16)
    ukv = w_ukv.reshape(MLA_KV_RANK, MLA_HEADS, MLA_NOPE_DIM + MLA_V_DIM)
    uk = jnp.pad(ukv[:, :, :MLA_NOPE_DIM], ((0, 0), (0, 0), (0, LANES - MLA_NOPE_DIM)))
    wukv = jnp.concatenate([uk.reshape(MLA_KV_RANK, MLA_HEADS * LANES),
                            ukv[:, :, MLA_NOPE_DIM:].reshape(MLA_KV_RANK, MLA_WIDTH)], axis=1).astype(BF16)
    return w1, w2, wuq, wukv, w_o[:SB_WIDTH].astype(BF16), w_o[SB_WIDTH:].astype(BF16)


def _router_weights(router_w, router_bias):
    d = router_w.shape[0]
    rw = router_w.reshape(d, N_GROUPS, EXPERTS_PER_GROUP).transpose(0, 2, 1).reshape(d, N_EXPERTS)
    hi = rw.astype(BF16)
    lo = (rw - hi.astype(F32)).astype(BF16)
    pad = jnp.zeros((d, LANES - N_EXPERTS), BF16)
    rwcat = jnp.concatenate([hi, pad, lo, pad], axis=1)
    rb = router_bias.reshape(N_GROUPS, EXPERTS_PER_GROUP).T.reshape(N_EXPERTS, 1)
    return rwcat, rb


def _dispatch_plan(route, counts, n_blocks):
    bm = DISPATCH_BLOCK
    cnt = counts[:, 0]
    padded = (cnt + bm - 1) // bm * bm
    pend = jnp.cumsum(padded)
    pstart = pend - padded
    eids = jnp.arange(N_EXPERTS, dtype=jnp.int32)
    start_of = jnp.sum(jnp.where(route[:TOP_K, :, None] == eids, pstart, 0), axis=-1)
    dest = start_of + route[TOP_K:2 * TOP_K]
    block_row = jnp.arange(n_blocks, dtype=jnp.int32) * bm
    block_e = jnp.minimum(jnp.sum(block_row[:, None] >= pend[None, :], axis=1), N_EXPERTS - 1)
    n_used = (pend[-1] // bm).reshape(1)
    return dest.astype(jnp.int32), block_e.astype(jnp.int32), n_used.astype(jnp.int32)


def kernel(x, c, positions, ada_w, ada_b, w_in, q_norm, kv_norm, w_uq, w_ukv, w_o, ln1_g, ln1_b,
           router_w, router_bias, w_gate, w_up, w_down, ln2_g, ln2_b):
    batch, seq, d = x.shape
    n = batch * seq
    depth = ada_w.shape[0]
    assert seq % TOKEN_TILE == 0 and seq % ATT_TILE == 0 and seq % DMA_TILE == 0

    mod_all = _ada_call(c, ada_w, ada_b).reshape(depth, batch, 6, d)
    cos_t, sin_t = _rope_call(positions)
    rwcat, rb = _router_weights(router_w, router_bias)
    t = ATT_TILE
    tri = (jnp.arange(t)[:, None] >= jnp.arange(t)[None, :]).astype(BF16)
    tri = jnp.concatenate([tri, tri], axis=0)
    tm = TOKEN_TILE
    ut = (jnp.arange(tm)[:, None] < jnp.arange(tm)[None, :]).astype(BF16)
    n_blocks = -(-(n * TOP_K) // DISPATCH_BLOCK) + N_EXPERTS
    buf_zero = jnp.zeros((n_blocks * DISPATCH_BLOCK, d), F32)

    x2d = x.reshape(n, d)
    for l in range(depth):
        mod = mod_all[l]
        w1, w2, wuq, wukv, wo_sb, wo_mla = _layer_weights(w_in[l], w_uq[l], w_ukv[l], w_o[l])
        sbq, sbk, sbv, mq, mk, mv = _proj_call(x2d, mod, cos_t, sin_t, w1, w2, q_norm[l].reshape(1, -1),
                                               kv_norm[l].reshape(1, -1), wuq, wukv, seq)
        sb_out = _sb_call(sbq, sbk, sbv, tri, batch, seq)
        mla_out = _mla_call(mq, mk, mv, batch, seq)
        x1, h2, route, gates, counts = _post_call(sb_out, mla_out, x2d, mod, wo_sb, wo_mla,
                                                  ln1_g[l].reshape(1, d), ln1_b[l].reshape(1, d),
                                                  rwcat, rb, ut, seq)
        dest, block_e, n_used = _dispatch_plan(route, counts, n_blocks)
        dest3 = dest.reshape(TOP_K, n // DMA_TILE, DMA_TILE).transpose(1, 0, 2)
        buf = _dispatch_call(dest3, h2, buf_zero)
        y_buf = _ffn_call(block_e, n_used, buf, w_gate[l], w_up[l], w_down[l])
        x2d = _combine_call(dest3, x1, gates, mod, ln2_g[l].reshape(1, d), ln2_b[l].reshape(1, d), y_buf, seq)
    return x2d.reshape(batch, seq, d)
```

```python
import jax
import jax.numpy as jnp
from jax import lax
from jax.experimental import pallas as pl
from jax.experimental.pallas import tpu as pltpu

F32 = jnp.float32
BF16 = jnp.bfloat16

D_MODEL = 1024
DEPTH = 2
SB_HEADS = 8
SB_HEAD_DIM = 64
SB_WIDTH = SB_HEADS * SB_HEAD_DIM
MLA_HEADS = 8
MLA_NOPE_DIM = 64
MLA_ROPE_DIM = 32
MLA_V_DIM = 64
MLA_Q_RANK = 256
MLA_KV_RANK = 128
MLA_WIDTH = MLA_HEADS * MLA_V_DIM
ROPE_BASE = 10000.0
N_EXPERTS = 32
N_GROUPS = 8
EXPERTS_PER_GROUP = 4
TOP_K = 2
D_EXPERT = 256
DISPATCH_BLOCK = 256
DEEPNORM_ALPHA = (2 * DEPTH) ** 0.25
LN_EPS = 1e-5
RMS_EPS = 1e-6
LOG2_E = 1.4426950408889634

LANES = 128
HALF_ROPE = MLA_ROPE_DIM // 2
ROPE_LANE0 = MLA_NOPE_DIM
VMEM_LIMIT = 56 * 1024 * 1024

TOKEN_TILE = 512
ATT_TILE = 256
GRANULE = 8
GRANULE_SHIFT = 3
SLAB_ROWS = -(-(TOKEN_TILE * TOP_K + N_EXPERTS * (GRANULE - 1)) // DISPATCH_BLOCK) * DISPATCH_BLOCK


def _dot(a, b):
    return jnp.dot(a, b, preferred_element_type=F32)


def _dot_nt(a, b):
    return lax.dot_general(a, b, (((1,), (1,)), ((), ())), preferred_element_type=F32)


def _split_bf16(v):
    hi = v.astype(BF16)
    lo = (v - hi.astype(F32)).astype(BF16)
    return hi, lo


def _standardize(x):
    mu = jnp.mean(x, axis=-1, keepdims=True)
    xc = x - mu
    var = jnp.mean(xc * xc, axis=-1, keepdims=True)
    return xc * lax.rsqrt(var + LN_EPS)


def _ada_kernel(c_ref, w_ref, b_ref, o_ref):
    c = c_ref[...]
    ca = c * jax.nn.sigmoid(c)
    ca_hi, ca_lo = _split_bf16(ca)
    w_hi, w_lo = _split_bf16(w_ref[...])
    o_ref[...] = _dot(ca_hi, w_hi) + _dot(ca_lo, w_hi) + _dot(ca_hi, w_lo) + b_ref[...]


def _ada_call(c, ada_w, ada_b):
    depth, d, n6 = ada_w.shape
    b = c.shape[0]
    tn = 1536
    return pl.pallas_call(
        _ada_kernel,
        grid=(depth, n6 // tn),
        in_specs=[pl.BlockSpec((b, d), lambda l, j: (0, 0)),
                  pl.BlockSpec((None, d, tn), lambda l, j: (l, 0, j)),
                  pl.BlockSpec((None, 1, tn), lambda l, j: (l, 0, j))],
        out_specs=pl.BlockSpec((None, b, tn), lambda l, j: (l, 0, j)),
        out_shape=jax.ShapeDtypeStruct((depth, b, n6), F32),
        compiler_params=pltpu.CompilerParams(vmem_limit_bytes=VMEM_LIMIT),
        name="ada",
    )(c, ada_w, ada_b.reshape(depth, 1, n6))


def _rope_kernel(pos_ref, invf_ref, cos_ref, sin_ref):
    ang = pos_ref[...].astype(F32) * invf_ref[...]
    lane = lax.broadcasted_iota(jnp.int32, ang.shape, 1)
    rot = (lane >= ROPE_LANE0) & (lane < ROPE_LANE0 + MLA_ROPE_DIM)
    cos_ref[...] = jnp.where(rot, jnp.cos(ang), 1.0)
    sin_ref[...] = jnp.where(rot, jnp.sin(ang), 0.0)


def _rope_call(positions):
    n = positions.size
    tm = TOKEN_TILE
    inv_freq = ROPE_BASE ** (-jnp.arange(0, MLA_ROPE_DIM, 2, dtype=F32) / MLA_ROPE_DIM)
    invf = jnp.zeros((1, LANES), F32)
    invf = invf.at[0, ROPE_LANE0:ROPE_LANE0 + HALF_ROPE].set(inv_freq)
    invf = invf.at[0, ROPE_LANE0 + HALF_ROPE:ROPE_LANE0 + MLA_ROPE_DIM].set(inv_freq)
    return pl.pallas_call(
        _rope_kernel,
        grid=(n // tm,),
        in_specs=[pl.BlockSpec((tm, 1), lambda i: (i, 0)),
                  pl.BlockSpec((1, LANES), lambda i: (0, 0))],
        out_specs=[pl.BlockSpec((tm, LANES), lambda i: (i, 0)),
                   pl.BlockSpec((tm, LANES), lambda i: (i, 0))],
        out_shape=[jax.ShapeDtypeStruct((n, LANES), F32)] * 2,
        name="rope_tables",
    )(positions.reshape(n, 1), invf)


def _proj_kernel(x_ref, mod_ref, cos_ref, sin_ref, w1_ref, w2_ref, qn_ref, kvn_ref, wuq_ref,
                 wukv_ref, sbq_ref, sbk_ref, sbv_ref, mq_ref, mk_ref, mv_ref):
    mod = mod_ref[...]
    h = _standardize(x_ref[...]) * (1.0 + mod[1:2]) + mod[0:1]
    hb = h.astype(BF16)

    p1 = _dot(hb, w1_ref[...])
    sbq_ref[...] = (p1[:, :SB_WIDTH] * (SB_HEAD_DIM ** -0.5 * LOG2_E)).astype(BF16)
    sbk_ref[...] = p1[:, SB_WIDTH:2 * SB_WIDTH].astype(BF16)
    sbv_ref[...] = p1[:, 2 * SB_WIDTH:].astype(BF16)

    p2 = _dot(hb, w2_ref[...])
    q_lat = p2[:, :MLA_Q_RANK]
    kv_lat = p2[:, MLA_Q_RANK:MLA_Q_RANK + MLA_KV_RANK]
    k_rope = p2[:, MLA_Q_RANK + MLA_KV_RANK:]

    qn = q_lat * lax.rsqrt(jnp.mean(q_lat * q_lat, axis=-1, keepdims=True) + RMS_EPS) * qn_ref[...]
    kvn = kv_lat * lax.rsqrt(jnp.mean(kv_lat * kv_lat, axis=-1, keepdims=True) + RMS_EPS) * kvn_ref[...]
    q = _dot(qn.astype(BF16), wuq_ref[...])
    kv = _dot(kvn.astype(BF16), wukv_ref[...])

    cos = cos_ref[...]
    sin = sin_ref[...]
    lane = lax.broadcasted_iota(jnp.int32, cos.shape, 1)
    second = lane >= ROPE_LANE0 + HALF_ROPE
    sin_up = jnp.where(second, sin, 0.0)
    sin_dn = jnp.where(second, 0.0, -sin)

    def rope(t):
        return (t * cos + pltpu.roll(t, HALF_ROPE, 1) * sin_up
                + pltpu.roll(t, LANES - HALF_ROPE, 1) * sin_dn)

    kr = rope(k_rope)
    mla_scale = (MLA_NOPE_DIM + MLA_ROPE_DIM) ** -0.5 * LOG2_E
    for hd in range(MLA_HEADS):
        sl = slice(hd * LANES, (hd + 1) * LANES)
        mq_ref[:, sl] = (rope(q[:, sl]) * mla_scale).astype(BF16)
        mk_ref[:, sl] = (kv[:, sl] + kr).astype(BF16)
    mv_ref[...] = kv[:, MLA_HEADS * LANES:].astype(BF16)


def _proj_call(x2d, mod, cos_t, sin_t, w1, w2, qn, kvn, wuq, wukv, seq):
    n, d = x2d.shape
    tm = TOKEN_TILE
    per_b = seq // tm
    tok = lambda i: (i, 0)
    full = lambda i: (0, 0)
    widths = (SB_WIDTH, SB_WIDTH, SB_WIDTH, MLA_HEADS * LANES, MLA_HEADS * LANES, MLA_WIDTH)
    return pl.pallas_call(
        _proj_kernel,
        grid=(n // tm,),
        in_specs=[pl.BlockSpec((tm, d), tok),
                  pl.BlockSpec((None, 6, d), lambda i: (i // per_b, 0, 0)),
                  pl.BlockSpec((tm, LANES), tok),
                  pl.BlockSpec((tm, LANES), tok),
                  pl.BlockSpec(w1.shape, full),
                  pl.BlockSpec(w2.shape, full),
                  pl.BlockSpec(qn.shape, full),
                  pl.BlockSpec(kvn.shape, full),
                  pl.BlockSpec(wuq.shape, full),
                  pl.BlockSpec(wukv.shape, full)],
        out_specs=[pl.BlockSpec((tm, w), tok) for w in widths],
        out_shape=[jax.ShapeDtypeStruct((n, w), BF16) for w in widths],
        compiler_params=pltpu.CompilerParams(vmem_limit_bytes=VMEM_LIMIT),
        name="proj",
    )(x2d, mod, cos_t, sin_t, w1, w2, qn, kvn, wuq, wukv)


def _causal_tiles(i, step, state):
    odd = i % 2
    state = lax.cond(odd == 1,
                     lambda s: step((i, i - 1), s, (True, False)),
                     lambda s: step((i,), s, (True,)), state)
    top = i - odd - 1

    def pair(p, s):
        j = top - 2 * p
        return step((j, j - 1), s, (False, False))

    return lax.fori_loop(0, (i - odd) // 2, pair, state)


def _sb_kernel(q_ref, k_ref, v_ref, tri_ref, o_ref):
    t = ATT_TILE
    tri2 = tri_ref[...]
    lane = lax.broadcasted_iota(jnp.int32, (1, LANES), 1)
    row = lax.broadcasted_iota(jnp.int32, (t, t), 0)
    col = lax.broadcasted_iota(jnp.int32, (t, t), 1)
    strict = col < row
    mine = [lane < SB_HEAD_DIM, lane >= SB_HEAD_DIM]

    def step(qh, js, state, diagonal):
        chains = [(j, h, d) for j, d in zip(js, diagonal) for h in range(2)]
        ks = [k_ref[pl.ds(j * t, t), :] for j in js]
        zs = [_dot_nt(qh[h], ks[n]) for n in range(len(js)) for h in range(2)]
        nlks, pieces = [], []
        for (_, _, diag), z in zip(chains, zs):
            nlk = jnp.maximum(z, 0.0) + jnp.log(1.0 + jnp.exp2(-jnp.abs(z))) * LOG2_E
            if diag:
                nlk = jnp.where(strict, nlk, 0.0)
            nlks.append(nlk)
            pieces.append(jnp.concatenate(_split_bf16(nlk), axis=1))
        laters = [_dot(p, tri2) for p in pieces]
        ncarry = [state[h][0] for h in range(2)]
        ws = []
        for (j, h, diag), z, nlk, later in zip(chains, zs, nlks, laters):
            w = jnp.exp2((z - ncarry[h]) - later)
            if diag:
                w = jnp.where(strict, w, 0.0)
            ws.append(w.astype(BF16))
            ncarry[h] = ncarry[h] + jnp.sum(nlk, axis=1, keepdims=True)
        acc = [state[h][1] for h in range(2)]
        for (j, h, _), w in zip(chains, ws):
            acc[h] = acc[h] + _dot(w, v_ref[pl.ds(j * t, t), :])
        return tuple((ncarry[h], acc[h]) for h in range(2))

    def q_tile(i, carry):
        rows = pl.ds(pl.multiple_of(i * t, t), t)
        q = q_ref[rows, :]
        qh = [jnp.where(m, q, jnp.zeros_like(q)) for m in mine]
        init = tuple((jnp.zeros((t, 1), F32), jnp.zeros((t, LANES), F32)) for _ in range(2))
        state = _causal_tiles(i, lambda js, s, d: step(qh, js, s, d), init)
        o_ref[rows, :] = jnp.where(mine[0], state[0][1], state[1][1]).astype(o_ref.dtype)
        return carry

    lax.fori_loop(0, q_ref.shape[0] // t, q_tile, 0)


def _sb_call(q, k, v, tri, batch, seq):
    t = ATT_TILE
    pairs = SB_WIDTH // LANES
    seq_spec = pl.BlockSpec((seq, LANES), lambda b, p: (b, p))
    return pl.pallas_call(
        _sb_kernel,
        grid=(batch, pairs),
        in_specs=[seq_spec, seq_spec, seq_spec,
                  pl.BlockSpec((2 * t, t), lambda b, p: (0, 0))],
        out_specs=seq_spec,
        out_shape=jax.ShapeDtypeStruct(q.shape, BF16),
        compiler_params=pltpu.CompilerParams(vmem_limit_bytes=VMEM_LIMIT),
        name="sb_attention",
    )(q, k, v, tri)


def _mla_kernel(q_ref, k_ref, v_ref, o_ref):
    t = ATT_TILE
    lane = lax.broadcasted_iota(jnp.int32, (1, LANES), 1)
    row = lax.broadcasted_iota(jnp.int32, (t, t), 0)
    col = lax.broadcasted_iota(jnp.int32, (t, t), 1)
    causal = col <= row

    def step(qh, js, state, diagonal):
        ss = []
        for hd in range(2):
            for j, diag in zip(js, diagonal):
                s = _dot_nt(qh[hd], k_ref[pl.ds(j * t, t), hd * LANES:(hd + 1) * LANES])
                ss.append(jnp.where(causal, s, -jnp.inf) if diag else s)
        n = len(js)
        ps, new_state = [], []
        for hd in range(2):
            m, l, acc = state[hd]
            mine_s = ss[hd * n:(hd + 1) * n]
            m_new = m
            for s in mine_s:
                m_new = jnp.maximum(m_new, jnp.max(s, axis=1, keepdims=True))
            a = jnp.exp2(m - m_new)
            l = a * l
            for s in mine_s:
                p = jnp.exp2(s - m_new)
                l = l + jnp.sum(p, axis=1, keepdims=True)
                ps.append(p.astype(BF16))
            new_state.append([m_new, l, a * acc])
        for hd in range(2):
            for idx, j in enumerate(js):
                new_state[hd][2] = new_state[hd][2] + _dot(ps[hd * n + idx], v_ref[pl.ds(j * t, t), :])
        return tuple(tuple(st) for st in new_state)

    def q_tile(i, carry):
        rows = pl.ds(pl.multiple_of(i * t, t), t)
        qh = [q_ref[rows, hd * LANES:(hd + 1) * LANES] for hd in range(2)]
        init = tuple((jnp.full((t, 1), -jnp.inf, F32), jnp.zeros((t, 1), F32), jnp.zeros((t, LANES), F32))
                     for _ in range(2))
        state = _causal_tiles(i, lambda js, s, d: step(qh, js, s, d), init)
        out = jnp.where(lane < MLA_V_DIM, state[0][2] / state[0][1], state[1][2] / state[1][1])
        o_ref[rows, :] = out.astype(o_ref.dtype)
        return carry

    lax.fori_loop(0, q_ref.shape[0] // t, q_tile, 0)


def _mla_call(q, k, v, batch, seq):
    pairs = MLA_WIDTH // LANES
    return pl.pallas_call(
        _mla_kernel,
        grid=(batch, pairs),
        in_specs=[pl.BlockSpec((seq, 2 * LANES), lambda b, p: (b, p)),
                  pl.BlockSpec((seq, 2 * LANES), lambda b, p: (b, p)),
                  pl.BlockSpec((seq, LANES), lambda b, p: (b, p))],
        out_specs=pl.BlockSpec((seq, LANES), lambda b, p: (b, p)),
        out_shape=jax.ShapeDtypeStruct(v.shape, BF16),
        compiler_params=pltpu.CompilerParams(vmem_limit_bytes=VMEM_LIMIT),
        name="mla_attention",
    )(q, k, v)


def _top2_sum(a, b, c, d):
    hi1, lo1 = jnp.maximum(a, b), jnp.minimum(a, b)
    hi2, lo2 = jnp.maximum(c, d), jnp.minimum(c, d)
    return jnp.maximum(hi1, hi2) + jnp.maximum(jnp.minimum(hi1, hi2), jnp.maximum(lo1, lo2))


def _first_argmax4(v):
    m = jnp.maximum(jnp.maximum(v[0], v[1]), jnp.maximum(v[2], v[3]))
    return jnp.where(v[0] == m, 0, jnp.where(v[1] == m, 1, jnp.where(v[2] == m, 2, 3)))


def _pick4(idx, v):
    return jnp.where(idx == 0, v[0], jnp.where(idx == 1, v[1], jnp.where(idx == 2, v[2], v[3])))


def _post_kernel(sb_ref, mla_ref, x_ref, mod_ref, wo_sb_ref, wo_mla_ref, g_ref, b_ref, rw_ref,
                 rb_ref, ut_ref, x1_ref, h2_ref, route_ref, gate_ref, cnt_ref):
    tm = x_ref.shape[0]
    mod = mod_ref[...]
    mix = _dot(sb_ref[...], wo_sb_ref[...]) + _dot(mla_ref[...], wo_mla_ref[...])
    x1 = _standardize(DEEPNORM_ALPHA * x_ref[...] + (1.0 + mod[2:3]) * mix) * g_ref[...] + b_ref[...]
    x1_ref[...] = x1
    h2 = _standardize(x1) * (1.0 + mod[4:5]) + mod[3:4]

    h_hi, h_lo = _split_bf16(h2)
    h2_ref[...] = h_hi
    rw = rw_ref[...]
    big = _dot(h_hi, rw)
    logits = big[:, :LANES] + big[:, LANES:] + _dot(h_lo, rw[:, :LANES])
    lt = logits.T[:N_EXPERTS]
    scores = jax.nn.sigmoid(lt)
    biased = scores + rb_ref[...]
    sc = [scores[N_GROUPS * p:N_GROUPS * (p + 1)] for p in range(EXPERTS_PER_GROUP)]
    bi = [biased[N_GROUPS * p:N_GROUPS * (p + 1)] for p in range(EXPERTS_PER_GROUP)]

    group_score = _top2_sum(*bi)
    gidx = lax.broadcasted_iota(jnp.int32, group_score.shape, 0)
    best = jnp.max(group_score, axis=0, keepdims=True)
    g_sel = jnp.min(jnp.where(group_score == best, gidx, N_GROUPS), axis=0, keepdims=True)
    in_sel = gidx == g_sel
    vb = [jnp.sum(jnp.where(in_sel, b, 0.0), axis=0, keepdims=True) for b in bi]
    vs = [jnp.sum(jnp.where(in_sel, s, 0.0), axis=0, keepdims=True) for s in sc]
    l1 = _first_argmax4(vb)
    vb2 = [jnp.where(l1 == p, -jnp.inf, vb[p]) for p in range(EXPERTS_PER_GROUP)]
    l2 = _first_argmax4(vb2)
    s1 = _pick4(l1, vs)
    s2 = _pick4(l2, vs)
    tot = s1 + s2
    e1 = g_sel * EXPERTS_PER_GROUP + l1
    e2 = g_sel * EXPERTS_PER_GROUP + l2

    eidx = lax.broadcasted_iota(jnp.int32, (N_EXPERTS, tm), 0)
    hit1 = eidx == e1
    hit2 = eidx == e2
    onehot = jnp.where(hit1 | hit2, 1.0, 0.0)
    before = _dot(onehot.astype(BF16), ut_ref[...])
    count = jnp.sum(onehot, axis=1, keepdims=True).astype(jnp.int32)
    granules = lax.shift_right_logical(count + (GRANULE - 1), GRANULE_SHIFT)
    lower = (lax.broadcasted_iota(jnp.int32, (N_EXPERTS, N_EXPERTS), 1)
             < lax.broadcasted_iota(jnp.int32, (N_EXPERTS, N_EXPERTS), 0))
    gran_f = jnp.broadcast_to(granules.astype(F32), (N_EXPERTS, LANES)).astype(BF16)
    start = _dot(jnp.where(lower, 1.0, 0.0).astype(BF16), gran_f)[:, 0:1] * GRANULE
    row_of = before + start
    p1 = jnp.sum(jnp.where(hit1, row_of, 0.0), axis=0, keepdims=True)
    p2 = jnp.sum(jnp.where(hit2, row_of, 0.0), axis=0, keepdims=True)
    cnt_ref[...] = jnp.broadcast_to(count, cnt_ref.shape)

    r8 = lax.broadcasted_iota(jnp.int32, (8, tm), 0)
    route_ref[...] = jnp.where(r8 == 0, p1.astype(jnp.int32), jnp.where(r8 == 1, p2.astype(jnp.int32), 0))
    r128 = lax.broadcasted_iota(jnp.int32, (LANES, tm), 0)
    cols = jnp.where(r128 == 0, s1 / tot, jnp.where(r128 == 1, s2 / tot,
                                                     jnp.where(r128 == 2, p1, jnp.where(r128 == 3, p2, 0.0))))
    gate_ref[...] = cols.T


def _post_call(sb_out, mla_out, x2d, mod, wo_sb, wo_mla, g, b, rw, rb, ut, seq):
    n, d = x2d.shape
    tm = TOKEN_TILE
    per_b = seq // tm
    tok = lambda i: (i, 0)
    full = lambda i: (0, 0)
    return pl.pallas_call(
        _post_kernel,
        grid=(n // tm,),
        in_specs=[pl.BlockSpec((tm, SB_WIDTH), tok),
                  pl.BlockSpec((tm, MLA_WIDTH), tok),
                  pl.BlockSpec((tm, d), tok),
                  pl.BlockSpec((None, 6, d), lambda i: (i // per_b, 0, 0)),
                  pl.BlockSpec(wo_sb.shape, full),
                  pl.BlockSpec(wo_mla.shape, full),
                  pl.BlockSpec((1, d), full),
                  pl.BlockSpec((1, d), full),
                  pl.BlockSpec(rw.shape, full),
                  pl.BlockSpec(rb.shape, full),
                  pl.BlockSpec(ut.shape, full)],
        out_specs=[pl.BlockSpec((tm, d), tok),
                   pl.BlockSpec((tm, d), tok),
                   pl.BlockSpec((8, tm), lambda i: (0, i)),
                   pl.BlockSpec((tm, LANES), tok),
                   pl.BlockSpec((None, N_EXPERTS, LANES), lambda i: (i, 0, 0))],
        out_shape=[jax.ShapeDtypeStruct((n, d), F32),
                   jax.ShapeDtypeStruct((n, d), BF16),
                   jax.ShapeDtypeStruct((8, n), jnp.int32),
                   jax.ShapeDtypeStruct((n, LANES), F32),
                   jax.ShapeDtypeStruct((n // tm, N_EXPERTS, LANES), jnp.int32)],
        compiler_params=pltpu.CompilerParams(vmem_limit_bytes=VMEM_LIMIT),
        name="post_attention",
    )(sb_out, mla_out, x2d, mod, wo_sb, wo_mla, g, b, rw, rb, ut)


def _granule_copy(src_ref, src_g, dst_ref, dst_g, sem):
    src = src_ref.at[pl.ds(pl.multiple_of(src_g * GRANULE, GRANULE), GRANULE), :]
    dst = dst_ref.at[pl.ds(pl.multiple_of(dst_g * GRANULE, GRANULE), GRANULE), :]
    return pltpu.make_async_copy(src, dst, sem)


def _wait_granules(count, vmem_ref, hbm_ref, sem):
    def wait(g, carry):
        _granule_copy(vmem_ref, 0, hbm_ref, 0, sem).wait()
        return carry

    lax.fori_loop(0, count, wait, 0)


def _block_copy(zero_ref, buf_ref, block, sem):
    dst = buf_ref.at[pl.ds(pl.multiple_of(block * DISPATCH_BLOCK, DISPATCH_BLOCK), DISPATCH_BLOCK), :]
    return pltpu.make_async_copy(zero_ref, dst, sem)


def _dispatch_kernel(ng_ref, ls_ref, gs_ref, tot_ref, tail_start_ref, tail_n_ref, n_used_ref, route_ref, h_ref,
                     buf_ref, slab_ref, zero_ref, sem):
    c = pl.program_id(0)
    last = pl.num_programs(0) - 1
    slot = c % 2
    tm = h_ref.shape[0]
    rows = slab_ref.shape[1]

    @pl.when(c == 0)
    def _():
        zero_ref[...] = jnp.zeros_like(zero_ref)

        def expert_tail(e, total):
            def granule(g, inner):
                _granule_copy(zero_ref, 0, buf_ref, tail_start_ref[e] + g, sem.at[2]).start()
                return inner

            lax.fori_loop(0, tail_n_ref[e], granule, 0)
            return total + tail_n_ref[e]

        n_tail = lax.fori_loop(0, N_EXPERTS, expert_tail, 0)
        n_blocks = buf_ref.shape[0] // DISPATCH_BLOCK

        def unused_block(b, carry):
            _block_copy(zero_ref, buf_ref, b, sem.at[3]).start()
            return carry

        lax.fori_loop(n_used_ref[0], n_blocks, unused_block, 0)
        _wait_granules(n_tail, zero_ref, buf_ref, sem.at[2])

        def wait_block(b, carry):
            _block_copy(zero_ref, buf_ref, 0, sem.at[3]).wait()
            return carry

        lax.fori_loop(n_used_ref[0], n_blocks, wait_block, 0)

    @pl.when(c >= 2)
    def _():
        _wait_granules(tot_ref[c - 2], slab_ref.at[slot], buf_ref, sem.at[slot])

    row = lax.broadcasted_iota(jnp.int32, (rows, tm), 0)
    route = route_ref[...]
    onehot = (row == route[0:1]) | (row == route[1:2])
    slab_ref[slot] = _dot(jnp.where(onehot, 1.0, 0.0).astype(BF16), h_ref[...])

    def expert(e, carry):
        idx = c * N_EXPERTS + e
        ls = ls_ref[idx]
        gs = gs_ref[idx]

        def granule(g, inner):
            _granule_copy(slab_ref.at[slot], ls + g, buf_ref, gs + g, sem.at[slot]).start()
            return inner

        lax.fori_loop(0, ng_ref[idx], granule, 0)
        return carry

    lax.fori_loop(0, N_EXPERTS, expert, 0)

    @pl.when(c == last)
    def _():
        _wait_granules(tot_ref[c], slab_ref.at[slot], buf_ref, sem.at[slot])

        @pl.when(c >= 1)
        def _():
            _wait_granules(tot_ref[c - 1], slab_ref.at[1 - slot], buf_ref, sem.at[1 - slot])


def _dispatch_call(tables, tails, n_used, route, h2, buf_rows):
    n, d = h2.shape
    tm = TOKEN_TILE
    grid_spec = pltpu.PrefetchScalarGridSpec(
        num_scalar_prefetch=7,
        grid=(n // tm,),
        in_specs=[pl.BlockSpec((8, tm), lambda i, *_: (0, i)),
                  pl.BlockSpec((tm, d), lambda i, *_: (i, 0))],
        out_specs=pl.BlockSpec(memory_space=pl.ANY),
        scratch_shapes=[pltpu.VMEM((2, SLAB_ROWS, d), F32), pltpu.VMEM((DISPATCH_BLOCK, d), F32),
                        pltpu.SemaphoreType.DMA((4,))],
    )
    return pl.pallas_call(
        _dispatch_kernel,
        grid_spec=grid_spec,
        out_shape=jax.ShapeDtypeStruct((buf_rows, d), F32),
        compiler_params=pltpu.CompilerParams(dimension_semantics=("arbitrary",),
                                             vmem_limit_bytes=VMEM_LIMIT),
        name="dispatch",
    )(*tables, *tails, n_used, route, h2)


def _ffn_kernel(be_ref, nb_ref, x_ref, wg_ref, wu_ref, wd_ref, y_ref, wg_s, wu_s, wd_s):
    i = pl.program_id(0)
    prev = be_ref[jnp.maximum(i - 1, 0)]

    @pl.when((i == 0) | (be_ref[i] != prev))
    def _():
        wg_s[...] = wg_ref[...].astype(BF16)
        wu_s[...] = wu_ref[...].astype(BF16)
        wd_s[...] = wd_ref[...].astype(BF16)

    @pl.when(i < nb_ref[0])
    def _():
        xb = x_ref[...].astype(BF16)
        g = _dot(xb, wg_s[...])
        u = _dot(xb, wu_s[...])
        a = g * jax.nn.sigmoid(g) * u
        y_ref[...] = _dot(a.astype(BF16), wd_s[...])

    @pl.when(i >= nb_ref[0])
    def _():
        y_ref[...] = jnp.zeros_like(y_ref)


def _ffn_call(block_e, n_used, buf, w_gate, w_up, w_down, layer):
    rows, d = buf.shape
    bm = DISPATCH_BLOCK
    de = w_gate.shape[-1]
    grid_spec = pltpu.PrefetchScalarGridSpec(
        num_scalar_prefetch=2,
        grid=(rows // bm,),
        in_specs=[pl.BlockSpec((bm, d), lambda i, be, nb: (jnp.minimum(i, nb[0] - 1), 0)),
                  pl.BlockSpec((None, None, d, de), lambda i, be, nb: (layer, be[i], 0, 0)),
                  pl.BlockSpec((None, None, d, de), lambda i, be, nb: (layer, be[i], 0, 0)),
                  pl.BlockSpec((None, None, de, d), lambda i, be, nb: (layer, be[i], 0, 0))],
        out_specs=pl.BlockSpec((bm, d), lambda i, be, nb: (i, 0)),
        scratch_shapes=[pltpu.VMEM((d, de), BF16), pltpu.VMEM((d, de), BF16), pltpu.VMEM((de, d), BF16)],
    )
    return pl.pallas_call(
        _ffn_kernel,
        grid_spec=grid_spec,
        out_shape=jax.ShapeDtypeStruct((rows, d), F32),
        compiler_params=pltpu.CompilerParams(dimension_semantics=("arbitrary",),
                                             vmem_limit_bytes=VMEM_LIMIT),
        name="expert_ffn",
    )(block_e, n_used, buf, w_gate, w_up, w_down)


def _combine_kernel(ng_ref, ls_ref, gs_ref, tot_ref, x1_ref, gate_ref, mod_ref, g_ref, b_ref, y_hbm_ref,
                    o_ref, slab_ref, sem):
    c = pl.program_id(0)
    last = pl.num_programs(0) - 1
    slot = c % 2
    tm = x1_ref.shape[0]
    rows = slab_ref.shape[1]

    def fetch(tile, into):
        def expert(e, carry):
            idx = tile * N_EXPERTS + e
            ls = ls_ref[idx]
            gs = gs_ref[idx]

            def granule(g, inner):
                _granule_copy(y_hbm_ref, gs + g, slab_ref.at[into], ls + g, sem.at[into]).start()
                return inner

            lax.fori_loop(0, ng_ref[idx], granule, 0)
            return carry

        lax.fori_loop(0, N_EXPERTS, expert, 0)

    @pl.when(c == 0)
    def _():
        slab_ref[...] = jnp.zeros_like(slab_ref)
        fetch(0, 0)

    @pl.when(c < last)
    def _():
        fetch(c + 1, 1 - slot)

    def wait(g, carry):
        _granule_copy(y_hbm_ref, 0, slab_ref.at[slot], 0, sem.at[slot]).wait()
        return carry

    lax.fori_loop(0, tot_ref[c], wait, 0)

    cols = gate_ref[...]
    lane = lax.broadcasted_iota(jnp.int32, (tm, rows), 1)
    ys = slab_ref[slot].astype(BF16)
    pick1 = jnp.where(lane == cols[:, 2:3].astype(jnp.int32), 1.0, 0.0).astype(BF16)
    pick2 = jnp.where(lane == cols[:, 3:4].astype(jnp.int32), 1.0, 0.0).astype(BF16)
    ffn = cols[:, 0:1] * _dot(pick1, ys) + cols[:, 1:2] * _dot(pick2, ys)
    mod = mod_ref[...]
    o_ref[...] = _standardize(DEEPNORM_ALPHA * x1_ref[...] + (1.0 + mod[5:6]) * ffn) * g_ref[...] + b_ref[...]


def _combine_call(tables, x1, gates, mod, g, b, y_buf, seq):
    n, d = x1.shape
    tm = TOKEN_TILE
    per_b = seq // tm
    tok = lambda i, *_: (i, 0)
    full = lambda i, *_: (0, 0)
    grid_spec = pltpu.PrefetchScalarGridSpec(
        num_scalar_prefetch=4,
        grid=(n // tm,),
        in_specs=[pl.BlockSpec((tm, d), tok),
                  pl.BlockSpec((tm, LANES), tok),
                  pl.BlockSpec((None, 6, d), lambda i, *_: (i // per_b, 0, 0)),
                  pl.BlockSpec((1, d), full),
                  pl.BlockSpec((1, d), full),
                  pl.BlockSpec(memory_space=pl.ANY)],
        out_specs=pl.BlockSpec((tm, d), tok),
        scratch_shapes=[pltpu.VMEM((2, SLAB_ROWS, d), F32), pltpu.SemaphoreType.DMA((2,))],
    )
    return pl.pallas_call(
        _combine_kernel,
        grid_spec=grid_spec,
        out_shape=jax.ShapeDtypeStruct((n, d), F32),
        compiler_params=pltpu.CompilerParams(dimension_semantics=("arbitrary",),
                                             vmem_limit_bytes=VMEM_LIMIT),
        name="combine",
    )(*tables, x1, gates, mod, g, b, y_buf)


def _layer_weights(w_in, w_uq, w_ukv, w_o):
    d = w_in.shape[0]
    w1 = w_in[:, :3 * SB_WIDTH].astype(BF16)
    lat = 3 * SB_WIDTH + MLA_Q_RANK + MLA_KV_RANK
    w2 = jnp.concatenate([w_in[:, 3 * SB_WIDTH:lat], jnp.zeros((d, ROPE_LANE0), F32), w_in[:, lat:],
                          jnp.zeros((d, LANES - ROPE_LANE0 - MLA_ROPE_DIM), F32)], axis=1).astype(BF16)
    uq = w_uq.reshape(MLA_Q_RANK, MLA_HEADS, MLA_NOPE_DIM + MLA_ROPE_DIM)
    uq = jnp.pad(uq, ((0, 0), (0, 0), (0, LANES - MLA_NOPE_DIM - MLA_ROPE_DIM)))
    wuq = uq.reshape(MLA_Q_RANK, MLA_HEADS * LANES).astype(BF16)
    ukv = w_ukv.reshape(MLA_KV_RANK, MLA_HEADS, MLA_NOPE_DIM + MLA_V_DIM)
    uk = jnp.pad(ukv[:, :, :MLA_NOPE_DIM], ((0, 0), (0, 0), (0, LANES - MLA_NOPE_DIM)))
    wukv = jnp.concatenate([uk.reshape(MLA_KV_RANK, MLA_HEADS * LANES),
                            ukv[:, :, MLA_NOPE_DIM:].reshape(MLA_KV_RANK, MLA_WIDTH)], axis=1).astype(BF16)
    return w1, w2, wuq, wukv, w_o[:SB_WIDTH].astype(BF16), w_o[SB_WIDTH:].astype(BF16)


def _router_weights(router_w, router_bias):
    d = router_w.shape[0]
    rw = router_w.reshape(d, N_GROUPS, EXPERTS_PER_GROUP).transpose(0, 2, 1).reshape(d, N_EXPERTS)
    hi = rw.astype(BF16)
    lo = (rw - hi.astype(F32)).astype(BF16)
    pad = jnp.zeros((d, LANES - N_EXPERTS), BF16)
    rwcat = jnp.concatenate([hi, pad, lo, pad], axis=1)
    rb = router_bias.reshape(N_GROUPS, EXPERTS_PER_GROUP).T.reshape(N_EXPERTS, 1)
    return rwcat, rb


def _dispatch_plan(counts, n_blocks):
    bm = DISPATCH_BLOCK
    gran = (counts + GRANULE - 1) // GRANULE
    slab_start = jnp.cumsum(gran, axis=1) - gran
    per_expert = jnp.sum(gran, axis=0) * GRANULE
    padded = (per_expert + bm - 1) // bm * bm
    pend = jnp.cumsum(padded)
    buf_start = (pend - padded) // GRANULE + jnp.cumsum(gran, axis=0) - gran
    tables = (gran.reshape(-1), slab_start.reshape(-1), buf_start.reshape(-1), jnp.sum(gran, axis=1))
    tails = ((pend - padded + per_expert) // GRANULE, (padded - per_expert) // GRANULE)
    block_row = jnp.arange(n_blocks, dtype=jnp.int32) * bm
    block_e = jnp.minimum(jnp.sum(block_row[:, None] >= pend[None, :], axis=1), N_EXPERTS - 1)
    n_used = (pend[-1] // bm).reshape(1)
    as_i32 = lambda ts: tuple(t.astype(jnp.int32) for t in ts)
    return as_i32(tables), as_i32(tails), block_e.astype(jnp.int32), n_used.astype(jnp.int32)


def kernel(x, c, positions, ada_w, ada_b, w_in, q_norm, kv_norm, w_uq, w_ukv, w_o, ln1_g, ln1_b,
           router_w, router_bias, w_gate, w_up, w_down, ln2_g, ln2_b):
    batch, seq, d = x.shape
    n = batch * seq
    depth = ada_w.shape[0]
    assert seq % TOKEN_TILE == 0 and seq % ATT_TILE == 0

    mod_all = _ada_call(c, ada_w, ada_b).reshape(depth, batch, 6, d)
    cos_t, sin_t = _rope_call(positions)
    rwcat, rb = _router_weights(router_w, router_bias)
    t = ATT_TILE
    tri = (jnp.arange(t)[:, None] >= jnp.arange(t)[None, :]).astype(BF16)
    tri = jnp.concatenate([tri, tri], axis=0)
    tm = TOKEN_TILE
    ut = (jnp.arange(tm)[:, None] < jnp.arange(tm)[None, :]).astype(BF16)
    max_rows = n * TOP_K + (n // tm) * N_EXPERTS * (GRANULE - 1) + N_EXPERTS * (DISPATCH_BLOCK - 1)
    n_blocks = -(-max_rows // DISPATCH_BLOCK)

    x2d = x.reshape(n, d)
    for l in range(depth):
        mod = mod_all[l]
        w1, w2, wuq, wukv, wo_sb, wo_mla = _layer_weights(w_in[l], w_uq[l], w_ukv[l], w_o[l])
        sbq, sbk, sbv, mq, mk, mv = _proj_call(x2d, mod, cos_t, sin_t, w1, w2, q_norm[l].reshape(1, -1),
                                               kv_norm[l].reshape(1, -1), wuq, wukv, seq)
        sb_out = _sb_call(sbq, sbk, sbv, tri, batch, seq)
        mla_out = _mla_call(mq, mk, mv, batch, seq)
        x1, h2, route, gates, counts = _post_call(sb_out, mla_out, x2d, mod, wo_sb, wo_mla,
                                                  ln1_g[l].reshape(1, d), ln1_b[l].reshape(1, d),
                                                  rwcat, rb, ut, seq)
        tables, tails, block_e, n_used = _dispatch_plan(counts[:, :, 0], n_blocks)
        buf = _dispatch_call(tables, tails, n_used, route, h2, n_blocks * DISPATCH_BLOCK)
        y_buf = _ffn_call(block_e, n_used, buf, w_gate, w_up, w_down, l)
        x2d = _combine_call(tables, x1, gates, mod, ln2_g[l].reshape(1, d), ln2_b[l].reshape(1, d), y_buf, seq)
    return x2d.reshape(batch, seq, d)
```

```python
import jax
import jax.numpy as jnp
from jax import lax
from jax.experimental import pallas as pl
from jax.experimental.pallas import tpu as pltpu

F32 = jnp.float32
BF16 = jnp.bfloat16

D_MODEL = 1024
DEPTH = 2
SB_HEADS = 8
SB_HEAD_DIM = 64
SB_WIDTH = SB_HEADS * SB_HEAD_DIM
MLA_HEADS = 8
MLA_NOPE_DIM = 64
MLA_ROPE_DIM = 32
MLA_V_DIM = 64
MLA_Q_RANK = 256
MLA_KV_RANK = 128
MLA_WIDTH = MLA_HEADS * MLA_V_DIM
ROPE_BASE = 10000.0
N_EXPERTS = 32
N_GROUPS = 8
EXPERTS_PER_GROUP = 4
TOP_K = 2
D_EXPERT = 256
DISPATCH_BLOCK = 512
DEEPNORM_ALPHA = (2 * DEPTH) ** 0.25
LN_EPS = 1e-5
RMS_EPS = 1e-6
LOG2_E = 1.4426950408889634

LANES = 128
HALF_ROPE = MLA_ROPE_DIM // 2
ROPE_LANE0 = MLA_NOPE_DIM
VMEM_LIMIT = 56 * 1024 * 1024

TOKEN_TILE = 512
ATT_TILE = 256
GRANULE = 8
GRANULE_SHIFT = 3
TILE_GROUP = 4
MXU_DIM = 256
SLAB_ROWS = -(-(TOKEN_TILE * TOP_K + N_EXPERTS * (GRANULE - 1)) // MXU_DIM) * MXU_DIM


def _dot(a, b):
    return jnp.dot(a, b, preferred_element_type=F32)


def _dot_nt(a, b):
    return lax.dot_general(a, b, (((1,), (1,)), ((), ())), preferred_element_type=F32)


def _split_bf16(v):
    hi = v.astype(BF16)
    lo = (v - hi.astype(F32)).astype(BF16)
    return hi, lo


def _standardize(x):
    mu = jnp.mean(x, axis=-1, keepdims=True)
    xc = x - mu
    var = jnp.mean(xc * xc, axis=-1, keepdims=True)
    return xc * lax.rsqrt(var + LN_EPS)


def _ada_kernel(c_ref, w_ref, b_ref, o_ref):
    c = c_ref[...]
    ca = c * jax.nn.sigmoid(c)
    ca_hi, ca_lo = _split_bf16(ca)
    w_hi, w_lo = _split_bf16(w_ref[...])
    o_ref[...] = _dot(ca_hi, w_hi) + _dot(ca_lo, w_hi) + _dot(ca_hi, w_lo) + b_ref[...]


def _ada_call(c, ada_w, ada_b):
    depth, d, n6 = ada_w.shape
    b = c.shape[0]
    tn = 1536
    return pl.pallas_call(
        _ada_kernel,
        grid=(depth, n6 // tn),
        in_specs=[pl.BlockSpec((b, d), lambda l, j: (0, 0)),
                  pl.BlockSpec((None, d, tn), lambda l, j: (l, 0, j)),
                  pl.BlockSpec((None, 1, tn), lambda l, j: (l, 0, j))],
        out_specs=pl.BlockSpec((None, b, tn), lambda l, j: (l, 0, j)),
        out_shape=jax.ShapeDtypeStruct((depth, b, n6), F32),
        compiler_params=pltpu.CompilerParams(vmem_limit_bytes=VMEM_LIMIT),
        name="ada",
    )(c, ada_w, ada_b.reshape(depth, 1, n6))


def _rope_kernel(pos_ref, invf_ref, cos_ref, sin_ref):
    ang = pos_ref[...].astype(F32) * invf_ref[...]
    lane = lax.broadcasted_iota(jnp.int32, ang.shape, 1)
    rot = (lane >= ROPE_LANE0) & (lane < ROPE_LANE0 + MLA_ROPE_DIM)
    cos_ref[...] = jnp.where(rot, jnp.cos(ang), 1.0)
    sin_ref[...] = jnp.where(rot, jnp.sin(ang), 0.0)


def _rope_call(positions):
    n = positions.size
    tm = TOKEN_TILE
    inv_freq = ROPE_BASE ** (-jnp.arange(0, MLA_ROPE_DIM, 2, dtype=F32) / MLA_ROPE_DIM)
    invf = jnp.zeros((1, LANES), F32)
    invf = invf.at[0, ROPE_LANE0:ROPE_LANE0 + HALF_ROPE].set(inv_freq)
    invf = invf.at[0, ROPE_LANE0 + HALF_ROPE:ROPE_LANE0 + MLA_ROPE_DIM].set(inv_freq)
    return pl.pallas_call(
        _rope_kernel,
        grid=(n // tm,),
        in_specs=[pl.BlockSpec((tm, 1), lambda i: (i, 0)),
                  pl.BlockSpec((1, LANES), lambda i: (0, 0))],
        out_specs=[pl.BlockSpec((tm, LANES), lambda i: (i, 0)),
                   pl.BlockSpec((tm, LANES), lambda i: (i, 0))],
        out_shape=[jax.ShapeDtypeStruct((n, LANES), F32)] * 2,
        name="rope_tables",
    )(positions.reshape(n, 1), invf)


def _proj_kernel(x_ref, mod_ref, cos_ref, sin_ref, w1_ref, w2_ref, qn_ref, kvn_ref, wuq_ref,
                 wukv_ref, sbq_ref, sbk_ref, sbv_ref, mq_ref, mk_ref, mv_ref):
    mod = mod_ref[...]
    h = _standardize(x_ref[...]) * (1.0 + mod[1:2]) + mod[0:1]
    hb = h.astype(BF16)

    p1 = _dot(hb, w1_ref[...])
    sbq_ref[...] = (p1[:, :SB_WIDTH] * (SB_HEAD_DIM ** -0.5 * LOG2_E)).astype(BF16)
    sbk_ref[...] = p1[:, SB_WIDTH:2 * SB_WIDTH].astype(BF16)
    sbv_ref[...] = p1[:, 2 * SB_WIDTH:].astype(BF16)

    p2 = _dot(hb, w2_ref[...])
    q_lat = p2[:, :MLA_Q_RANK]
    kv_lat = p2[:, MLA_Q_RANK:MLA_Q_RANK + MLA_KV_RANK]
    k_rope = p2[:, MLA_Q_RANK + MLA_KV_RANK:]

    qn = q_lat * lax.rsqrt(jnp.mean(q_lat * q_lat, axis=-1, keepdims=True) + RMS_EPS) * qn_ref[...]
    kvn = kv_lat * lax.rsqrt(jnp.mean(kv_lat * kv_lat, axis=-1, keepdims=True) + RMS_EPS) * kvn_ref[...]
    q = _dot(qn.astype(BF16), wuq_ref[...])
    kv = _dot(kvn.astype(BF16), wukv_ref[...])

    cos = cos_ref[...]
    sin = sin_ref[...]
    lane = lax.broadcasted_iota(jnp.int32, cos.shape, 1)
    second = lane >= ROPE_LANE0 + HALF_ROPE
    sin_up = jnp.where(second, sin, 0.0)
    sin_dn = jnp.where(second, 0.0, -sin)

    def rope(t):
        return (t * cos + pltpu.roll(t, HALF_ROPE, 1) * sin_up
                + pltpu.roll(t, LANES - HALF_ROPE, 1) * sin_dn)

    kr = rope(k_rope)
    mla_scale = (MLA_NOPE_DIM + MLA_ROPE_DIM) ** -0.5 * LOG2_E
    for hd in range(MLA_HEADS):
        sl = slice(hd * LANES, (hd + 1) * LANES)
        mq_ref[:, sl] = (rope(q[:, sl]) * mla_scale).astype(BF16)
        mk_ref[:, sl] = (kv[:, sl] + kr).astype(BF16)
    mv_ref[...] = kv[:, MLA_HEADS * LANES:].astype(BF16)


def _proj_call(x2d, mod, cos_t, sin_t, w1, w2, qn, kvn, wuq, wukv, seq):
    n, d = x2d.shape
    tm = TOKEN_TILE
    per_b = seq // tm
    tok = lambda i: (i, 0)
    full = lambda i: (0, 0)
    widths = (SB_WIDTH, SB_WIDTH, SB_WIDTH, MLA_HEADS * LANES, MLA_HEADS * LANES, MLA_WIDTH)
    return pl.pallas_call(
        _proj_kernel,
        grid=(n // tm,),
        in_specs=[pl.BlockSpec((tm, d), tok),
                  pl.BlockSpec((None, 6, d), lambda i: (i // per_b, 0, 0)),
                  pl.BlockSpec((tm, LANES), tok),
                  pl.BlockSpec((tm, LANES), tok),
                  pl.BlockSpec(w1.shape, full),
                  pl.BlockSpec(w2.shape, full),
                  pl.BlockSpec(qn.shape, full),
                  pl.BlockSpec(kvn.shape, full),
                  pl.BlockSpec(wuq.shape, full),
                  pl.BlockSpec(wukv.shape, full)],
        out_specs=[pl.BlockSpec((tm, w), tok) for w in widths],
        out_shape=[jax.ShapeDtypeStruct((n, w), BF16) for w in widths],
        compiler_params=pltpu.CompilerParams(vmem_limit_bytes=VMEM_LIMIT),
        name="proj",
    )(x2d, mod, cos_t, sin_t, w1, w2, qn, kvn, wuq, wukv)


def _causal_tiles(i, step, state):
    first = i % TILE_GROUP + 1

    def first_step(count):
        tiles = tuple(i - d for d in range(count))
        return lambda s: step(tiles, s, (True,) + (False,) * (count - 1))

    state = lax.switch(first - 1, [first_step(c) for c in range(1, TILE_GROUP + 1)], state)
    top = i - first

    def group(p, s):
        j = top - TILE_GROUP * p
        return step(tuple(j - d for d in range(TILE_GROUP)), s, (False,) * TILE_GROUP)

    return lax.fori_loop(0, (i + 1 - first) // TILE_GROUP, group, state)


def _sb_kernel(q_ref, k_ref, v_ref, tri_ref, o_ref):
    t = ATT_TILE
    tri2 = tri_ref[...]
    lane = lax.broadcasted_iota(jnp.int32, (1, LANES), 1)
    row = lax.broadcasted_iota(jnp.int32, (t, t), 0)
    col = lax.broadcasted_iota(jnp.int32, (t, t), 1)
    strict = col < row
    mine = [lane < SB_HEAD_DIM, lane >= SB_HEAD_DIM]

    def step(qh, js, state, diagonal):
        chains = [(j, h, d) for j, d in zip(js, diagonal) for h in range(2)]
        ks = [k_ref[pl.ds(j * t, t), :] for j in js]
        zs = [_dot_nt(qh[h], ks[n]) for n in range(len(js)) for h in range(2)]
        nlks, pieces = [], []
        for (_, _, diag), z in zip(chains, zs):
            nlk = jnp.maximum(z, 0.0) + jnp.log(1.0 + jnp.exp2(-jnp.abs(z))) * LOG2_E
            if diag:
                nlk = jnp.where(strict, nlk, 0.0)
            nlks.append(nlk)
            pieces.append(jnp.concatenate(_split_bf16(nlk), axis=1))
        laters = [_dot(p, tri2) for p in pieces]
        ncarry = [state[h][0] for h in range(2)]
        ws = []
        for (j, h, diag), z, nlk, later in zip(chains, zs, nlks, laters):
            w = jnp.exp2((z - ncarry[h]) - later)
            if diag:
                w = jnp.where(strict, w, 0.0)
            ws.append(w.astype(BF16))
            ncarry[h] = ncarry[h] + jnp.sum(nlk, axis=1, keepdims=True)
        acc = [state[h][1] for h in range(2)]
        for (j, h, _), w in zip(chains, ws):
            acc[h] = acc[h] + _dot(w, v_ref[pl.ds(j * t, t), :])
        return tuple((ncarry[h], acc[h]) for h in range(2))

    def q_tile(i, carry):
        rows = pl.ds(pl.multiple_of(i * t, t), t)
        q = q_ref[rows, :]
        qh = [jnp.where(m, q, jnp.zeros_like(q)) for m in mine]
        init = tuple((jnp.zeros((t, 1), F32), jnp.zeros((t, LANES), F32)) for _ in range(2))
        state = _causal_tiles(i, lambda js, s, d: step(qh, js, s, d), init)
        o_ref[rows, :] = jnp.where(mine[0], state[0][1], state[1][1]).astype(o_ref.dtype)
        return carry

    lax.fori_loop(0, q_ref.shape[0] // t, q_tile, 0)


def _sb_call(q, k, v, tri, batch, seq):
    t = ATT_TILE
    pairs = SB_WIDTH // LANES
    seq_spec = pl.BlockSpec((seq, LANES), lambda b, p: (b, p))
    return pl.pallas_call(
        _sb_kernel,
        grid=(batch, pairs),
        in_specs=[seq_spec, seq_spec, seq_spec,
                  pl.BlockSpec((2 * t, t), lambda b, p: (0, 0))],
        out_specs=seq_spec,
        out_shape=jax.ShapeDtypeStruct(q.shape, BF16),
        compiler_params=pltpu.CompilerParams(vmem_limit_bytes=VMEM_LIMIT),
        name="sb_attention",
    )(q, k, v, tri)


def _mla_kernel(q_ref, k_ref, v_ref, o_ref):
    t = ATT_TILE
    lane = lax.broadcasted_iota(jnp.int32, (1, LANES), 1)
    row = lax.broadcasted_iota(jnp.int32, (t, t), 0)
    col = lax.broadcasted_iota(jnp.int32, (t, t), 1)
    causal = col <= row

    def step(qh, js, state, diagonal):
        ss = []
        for hd in range(2):
            for j, diag in zip(js, diagonal):
                s = _dot_nt(qh[hd], k_ref[pl.ds(j * t, t), hd * LANES:(hd + 1) * LANES])
                ss.append(jnp.where(causal, s, -jnp.inf) if diag else s)
        n = len(js)
        ps, new_state = [], []
        for hd in range(2):
            m, l, acc = state[hd]
            mine_s = ss[hd * n:(hd + 1) * n]
            m_new = m
            for s in mine_s:
                m_new = jnp.maximum(m_new, jnp.max(s, axis=1, keepdims=True))
            a = jnp.exp2(m - m_new)
            l = a * l
            for s in mine_s:
                p = jnp.exp2(s - m_new)
                l = l + jnp.sum(p, axis=1, keepdims=True)
                ps.append(p.astype(BF16))
            new_state.append([m_new, l, a * acc])
        for hd in range(2):
            for idx, j in enumerate(js):
                new_state[hd][2] = new_state[hd][2] + _dot(ps[hd * n + idx], v_ref[pl.ds(j * t, t), :])
        return tuple(tuple(st) for st in new_state)

    def q_tile(i, carry):
        rows = pl.ds(pl.multiple_of(i * t, t), t)
        qh = [q_ref[rows, hd * LANES:(hd + 1) * LANES] for hd in range(2)]
        init = tuple((jnp.full((t, 1), -jnp.inf, F32), jnp.zeros((t, 1), F32), jnp.zeros((t, LANES), F32))
                     for _ in range(2))
        state = _causal_tiles(i, lambda js, s, d: step(qh, js, s, d), init)
        out = jnp.where(lane < MLA_V_DIM, state[0][2] / state[0][1], state[1][2] / state[1][1])
        o_ref[rows, :] = out.astype(o_ref.dtype)
        return carry

    lax.fori_loop(0, q_ref.shape[0] // t, q_tile, 0)


def _mla_call(q, k, v, batch, seq):
    pairs = MLA_WIDTH // LANES
    return pl.pallas_call(
        _mla_kernel,
        grid=(batch, pairs),
        in_specs=[pl.BlockSpec((seq, 2 * LANES), lambda b, p: (b, p)),
                  pl.BlockSpec((seq, 2 * LANES), lambda b, p: (b, p)),
                  pl.BlockSpec((seq, LANES), lambda b, p: (b, p))],
        out_specs=pl.BlockSpec((seq, LANES), lambda b, p: (b, p)),
        out_shape=jax.ShapeDtypeStruct(v.shape, BF16),
        compiler_params=pltpu.CompilerParams(vmem_limit_bytes=VMEM_LIMIT),
        name="mla_attention",
    )(q, k, v)


def _top2_sum(a, b, c, d):
    hi1, lo1 = jnp.maximum(a, b), jnp.minimum(a, b)
    hi2, lo2 = jnp.maximum(c, d), jnp.minimum(c, d)
    return jnp.maximum(hi1, hi2) + jnp.maximum(jnp.minimum(hi1, hi2), jnp.maximum(lo1, lo2))


def _first_argmax4(v):
    m = jnp.maximum(jnp.maximum(v[0], v[1]), jnp.maximum(v[2], v[3]))
    return jnp.where(v[0] == m, 0, jnp.where(v[1] == m, 1, jnp.where(v[2] == m, 2, 3)))


def _pick4(idx, v):
    return jnp.where(idx == 0, v[0], jnp.where(idx == 1, v[1], jnp.where(idx == 2, v[2], v[3])))


def _post_kernel(sb_ref, mla_ref, x_ref, mod_ref, wo_sb_ref, wo_mla_ref, g_ref, b_ref, rw_ref,
                 rb_ref, ut_ref, x1_ref, h2_ref, route_ref, gate_ref, cnt_ref):
    tm = x_ref.shape[0]
    mod = mod_ref[...]
    mix = _dot(sb_ref[...], wo_sb_ref[...]) + _dot(mla_ref[...], wo_mla_ref[...])
    x1 = _standardize(DEEPNORM_ALPHA * x_ref[...] + (1.0 + mod[2:3]) * mix) * g_ref[...] + b_ref[...]
    x1_ref[...] = x1
    h2 = _standardize(x1) * (1.0 + mod[4:5]) + mod[3:4]

    h_hi, h_lo = _split_bf16(h2)
    h2_ref[...] = h_hi
    rw = rw_ref[...]
    big = _dot(h_hi, rw)
    logits = big[:, :LANES] + big[:, LANES:] + _dot(h_lo, rw[:, :LANES])
    lt = logits.T[:N_EXPERTS]
    scores = jax.nn.sigmoid(lt)
    biased = scores + rb_ref[...]
    sc = [scores[N_GROUPS * p:N_GROUPS * (p + 1)] for p in range(EXPERTS_PER_GROUP)]
    bi = [biased[N_GROUPS * p:N_GROUPS * (p + 1)] for p in range(EXPERTS_PER_GROUP)]

    group_score = _top2_sum(*bi)
    gidx = lax.broadcasted_iota(jnp.int32, group_score.shape, 0)
    best = jnp.max(group_score, axis=0, keepdims=True)
    g_sel = jnp.min(jnp.where(group_score == best, gidx, N_GROUPS), axis=0, keepdims=True)
    in_sel = gidx == g_sel
    vb = [jnp.sum(jnp.where(in_sel, b, 0.0), axis=0, keepdims=True) for b in bi]
    vs = [jnp.sum(jnp.where(in_sel, s, 0.0), axis=0, keepdims=True) for s in sc]
    l1 = _first_argmax4(vb)
    vb2 = [jnp.where(l1 == p, -jnp.inf, vb[p]) for p in range(EXPERTS_PER_GROUP)]
    l2 = _first_argmax4(vb2)
    s1 = _pick4(l1, vs)
    s2 = _pick4(l2, vs)
    tot = s1 + s2
    e1 = g_sel * EXPERTS_PER_GROUP + l1
    e2 = g_sel * EXPERTS_PER_GROUP + l2

    eidx = lax.broadcasted_iota(jnp.int32, (N_EXPERTS, tm), 0)
    hit1 = eidx == e1
    hit2 = eidx == e2
    onehot = jnp.where(hit1 | hit2, 1.0, 0.0)
    before = _dot(onehot.astype(BF16), ut_ref[...])
    count = jnp.sum(onehot, axis=1, keepdims=True).astype(jnp.int32)
    granules = lax.shift_right_logical(count + (GRANULE - 1), GRANULE_SHIFT)
    lower = (lax.broadcasted_iota(jnp.int32, (N_EXPERTS, N_EXPERTS), 1)
             < lax.broadcasted_iota(jnp.int32, (N_EXPERTS, N_EXPERTS), 0))
    gran_f = jnp.broadcast_to(granules.astype(F32), (N_EXPERTS, LANES)).astype(BF16)
    start = _dot(jnp.where(lower, 1.0, 0.0).astype(BF16), gran_f)[:, 0:1] * GRANULE
    row_of = before + start
    p1 = jnp.sum(jnp.where(hit1, row_of, 0.0), axis=0, keepdims=True)
    p2 = jnp.sum(jnp.where(hit2, row_of, 0.0), axis=0, keepdims=True)
    cnt_ref[...] = jnp.broadcast_to(count, cnt_ref.shape)

    r8 = lax.broadcasted_iota(jnp.int32, (8, tm), 0)
    route_ref[...] = jnp.where(r8 == 0, p1.astype(jnp.int32), jnp.where(r8 == 1, p2.astype(jnp.int32), 0))
    r128 = lax.broadcasted_iota(jnp.int32, (LANES, tm), 0)
    cols = jnp.where(r128 == 0, s1 / tot, jnp.where(r128 == 1, s2 / tot,
                                                     jnp.where(r128 == 2, p1, jnp.where(r128 == 3, p2, 0.0))))
    gate_ref[...] = cols.T


def _post_call(sb_out, mla_out, x2d, mod, wo_sb, wo_mla, g, b, rw, rb, ut, seq):
    n, d = x2d.shape
    tm = TOKEN_TILE
    per_b = seq // tm
    tok = lambda i: (i, 0)
    full = lambda i: (0, 0)
    return pl.pallas_call(
        _post_kernel,
        grid=(n // tm,),
        in_specs=[pl.BlockSpec((tm, SB_WIDTH), tok),
                  pl.BlockSpec((tm, MLA_WIDTH), tok),
                  pl.BlockSpec((tm, d), tok),
                  pl.BlockSpec((None, 6, d), lambda i: (i // per_b, 0, 0)),
                  pl.BlockSpec(wo_sb.shape, full),
                  pl.BlockSpec(wo_mla.shape, full),
                  pl.BlockSpec((1, d), full),
                  pl.BlockSpec((1, d), full),
                  pl.BlockSpec(rw.shape, full),
                  pl.BlockSpec(rb.shape, full),
                  pl.BlockSpec(ut.shape, full)],
        out_specs=[pl.BlockSpec((tm, d), tok),
                   pl.BlockSpec((tm, d), tok),
                   pl.BlockSpec((8, tm), lambda i: (0, i)),
                   pl.BlockSpec((tm, LANES), tok),
                   pl.BlockSpec((None, N_EXPERTS, LANES), lambda i: (i, 0, 0))],
        out_shape=[jax.ShapeDtypeStruct((n, d), F32),
                   jax.ShapeDtypeStruct((n, d), BF16),
                   jax.ShapeDtypeStruct((8, n), jnp.int32),
                   jax.ShapeDtypeStruct((n, LANES), F32),
                   jax.ShapeDtypeStruct((n // tm, N_EXPERTS, LANES), jnp.int32)],
        compiler_params=pltpu.CompilerParams(vmem_limit_bytes=VMEM_LIMIT),
        name="post_attention",
    )(sb_out, mla_out, x2d, mod, wo_sb, wo_mla, g, b, rw, rb, ut)


def _granule_copy(src_ref, src_g, dst_ref, dst_g, sem):
    src = src_ref.at[pl.ds(pl.multiple_of(src_g * GRANULE, GRANULE), GRANULE), :]
    dst = dst_ref.at[pl.ds(pl.multiple_of(dst_g * GRANULE, GRANULE), GRANULE), :]
    return pltpu.make_async_copy(src, dst, sem)


def _wait_granules(count, vmem_ref, hbm_ref, sem):
    def wait(g, carry):
        _granule_copy(vmem_ref, 0, hbm_ref, 0, sem).wait()
        return carry

    lax.fori_loop(0, count, wait, 0)


def _block_copy(zero_ref, buf_ref, block, sem):
    dst = buf_ref.at[pl.ds(pl.multiple_of(block * DISPATCH_BLOCK, DISPATCH_BLOCK), DISPATCH_BLOCK), :]
    return pltpu.make_async_copy(zero_ref, dst, sem)


def _dispatch_kernel(ng_ref, ls_ref, gs_ref, tot_ref, tail_start_ref, tail_n_ref, n_used_ref, route_ref, h_ref,
                     buf_ref, slab_ref, zero_ref, sem):
    c = pl.program_id(0)
    last = pl.num_programs(0) - 1
    slot = c % 2
    tm = h_ref.shape[0]
    rows = slab_ref.shape[1]

    @pl.when(c == 0)
    def _():
        zero_ref[...] = jnp.zeros_like(zero_ref)

        def expert_tail(e, total):
            def granule(g, inner):
                _granule_copy(zero_ref, 0, buf_ref, tail_start_ref[e] + g, sem.at[2]).start()
                return inner

            lax.fori_loop(0, tail_n_ref[e], granule, 0)
            return total + tail_n_ref[e]

        n_tail = lax.fori_loop(0, N_EXPERTS, expert_tail, 0)
        n_blocks = buf_ref.shape[0] // DISPATCH_BLOCK

        def unused_block(b, carry):
            _block_copy(zero_ref, buf_ref, b, sem.at[3]).start()
            return carry

        lax.fori_loop(n_used_ref[0], n_blocks, unused_block, 0)
        _wait_granules(n_tail, zero_ref, buf_ref, sem.at[2])

        def wait_block(b, carry):
            _block_copy(zero_ref, buf_ref, 0, sem.at[3]).wait()
            return carry

        lax.fori_loop(n_used_ref[0], n_blocks, wait_block, 0)

    @pl.when(c >= 2)
    def _():
        _wait_granules(tot_ref[c - 2], slab_ref.at[slot], buf_ref, sem.at[slot])

    row = lax.broadcasted_iota(jnp.int32, (rows, tm), 0)
    route = route_ref[...]
    onehot = (row == route[0:1]) | (row == route[1:2])
    slab_ref[slot] = _dot(jnp.where(onehot, 1.0, 0.0).astype(BF16), h_ref[...])

    def expert(e, carry):
        idx = c * N_EXPERTS + e
        ls = ls_ref[idx]
        gs = gs_ref[idx]

        def granule(g, inner):
            _granule_copy(slab_ref.at[slot], ls + g, buf_ref, gs + g, sem.at[slot]).start()
            return inner

        lax.fori_loop(0, ng_ref[idx], granule, 0)
        return carry

    lax.fori_loop(0, N_EXPERTS, expert, 0)

    @pl.when(c == last)
    def _():
        _wait_granules(tot_ref[c], slab_ref.at[slot], buf_ref, sem.at[slot])

        @pl.when(c >= 1)
        def _():
            _wait_granules(tot_ref[c - 1], slab_ref.at[1 - slot], buf_ref, sem.at[1 - slot])


def _dispatch_call(tables, tails, n_used, route, h2, buf_rows):
    n, d = h2.shape
    tm = TOKEN_TILE
    grid_spec = pltpu.PrefetchScalarGridSpec(
        num_scalar_prefetch=7,
        grid=(n // tm,),
        in_specs=[pl.BlockSpec((8, tm), lambda i, *_: (0, i)),
                  pl.BlockSpec((tm, d), lambda i, *_: (i, 0))],
        out_specs=pl.BlockSpec(memory_space=pl.ANY),
        scratch_shapes=[pltpu.VMEM((2, SLAB_ROWS, d), F32), pltpu.VMEM((DISPATCH_BLOCK, d), F32),
                        pltpu.SemaphoreType.DMA((4,))],
    )
    return pl.pallas_call(
        _dispatch_kernel,
        grid_spec=grid_spec,
        out_shape=jax.ShapeDtypeStruct((buf_rows, d), F32),
        compiler_params=pltpu.CompilerParams(dimension_semantics=("arbitrary",),
                                             vmem_limit_bytes=VMEM_LIMIT),
        name="dispatch",
    )(*tables, *tails, n_used, route, h2)


def _ffn_kernel(be_ref, nb_ref, x_ref, wg_ref, wu_ref, wd_ref, y_ref, wg_s, wu_s, wd_s):
    i = pl.program_id(0)
    prev = be_ref[jnp.maximum(i - 1, 0)]

    @pl.when((i == 0) | (be_ref[i] != prev))
    def _():
        wg_s[...] = wg_ref[...].astype(BF16)
        wu_s[...] = wu_ref[...].astype(BF16)
        wd_s[...] = wd_ref[...].astype(BF16)

    @pl.when(i < nb_ref[0])
    def _():
        xb = x_ref[...].astype(BF16)
        g = _dot(xb, wg_s[...])
        u = _dot(xb, wu_s[...])
        a = g * jax.nn.sigmoid(g) * u
        y_ref[...] = _dot(a.astype(BF16), wd_s[...])

    @pl.when(i >= nb_ref[0])
    def _():
        y_ref[...] = jnp.zeros_like(y_ref)


def _ffn_call(block_e, n_used, buf, w_gate, w_up, w_down, layer):
    rows, d = buf.shape
    bm = DISPATCH_BLOCK
    de = w_gate.shape[-1]
    grid_spec = pltpu.PrefetchScalarGridSpec(
        num_scalar_prefetch=2,
        grid=(rows // bm,),
        in_specs=[pl.BlockSpec((bm, d), lambda i, be, nb: (jnp.minimum(i, nb[0] - 1), 0)),
                  pl.BlockSpec((None, None, d, de), lambda i, be, nb: (layer, be[i], 0, 0)),
                  pl.BlockSpec((None, None, d, de), lambda i, be, nb: (layer, be[i], 0, 0)),
                  pl.BlockSpec((None, None, de, d), lambda i, be, nb: (layer, be[i], 0, 0))],
        out_specs=pl.BlockSpec((bm, d), lambda i, be, nb: (i, 0)),
        scratch_shapes=[pltpu.VMEM((d, de), BF16), pltpu.VMEM((d, de), BF16), pltpu.VMEM((de, d), BF16)],
    )
    return pl.pallas_call(
        _ffn_kernel,
        grid_spec=grid_spec,
        out_shape=jax.ShapeDtypeStruct((rows, d), F32),
        compiler_params=pltpu.CompilerParams(dimension_semantics=("arbitrary",),
                                             vmem_limit_bytes=VMEM_LIMIT),
        name="expert_ffn",
    )(block_e, n_used, buf, w_gate, w_up, w_down)


def _combine_kernel(ng_ref, ls_ref, gs_ref, tot_ref, x1_ref, gate_ref, mod_ref, g_ref, b_ref, y_hbm_ref,
                    o_ref, slab_ref, sem):
    c = pl.program_id(0)
    last = pl.num_programs(0) - 1
    slot = c % 2
    tm = x1_ref.shape[0]
    rows = slab_ref.shape[1]

    def fetch(tile, into):
        def expert(e, carry):
            idx = tile * N_EXPERTS + e
            ls = ls_ref[idx]
            gs = gs_ref[idx]

            def granule(g, inner):
                _granule_copy(y_hbm_ref, gs + g, slab_ref.at[into], ls + g, sem.at[into]).start()
                return inner

            lax.fori_loop(0, ng_ref[idx], granule, 0)
            return carry

        lax.fori_loop(0, N_EXPERTS, expert, 0)

    @pl.when(c == 0)
    def _():
        slab_ref[...] = jnp.zeros_like(slab_ref)
        fetch(0, 0)

    @pl.when(c < last)
    def _():
        fetch(c + 1, 1 - slot)

    def wait(g, carry):
        _granule_copy(y_hbm_ref, 0, slab_ref.at[slot], 0, sem.at[slot]).wait()
        return carry

    lax.fori_loop(0, tot_ref[c], wait, 0)

    cols = gate_ref[...]
    lane = lax.broadcasted_iota(jnp.int32, (tm, rows), 1)
    ys = slab_ref[slot].astype(BF16)
    pick1 = jnp.where(lane == cols[:, 2:3].astype(jnp.int32), 1.0, 0.0).astype(BF16)
    pick2 = jnp.where(lane == cols[:, 3:4].astype(jnp.int32), 1.0, 0.0).astype(BF16)
    ffn = cols[:, 0:1] * _dot(pick1, ys) + cols[:, 1:2] * _dot(pick2, ys)
    mod = mod_ref[...]
    o_ref[...] = _standardize(DEEPNORM_ALPHA * x1_ref[...] + (1.0 + mod[5:6]) * ffn) * g_ref[...] + b_ref[...]


def _combine_call(tables, x1, gates, mod, g, b, y_buf, seq):
    n, d = x1.shape
    tm = TOKEN_TILE
    per_b = seq // tm
    tok = lambda i, *_: (i, 0)
    full = lambda i, *_: (0, 0)
    grid_spec = pltpu.PrefetchScalarGridSpec(
        num_scalar_prefetch=4,
        grid=(n // tm,),
        in_specs=[pl.BlockSpec((tm, d), tok),
                  pl.BlockSpec((tm, LANES), tok),
                  pl.BlockSpec((None, 6, d), lambda i, *_: (i // per_b, 0, 0)),
                  pl.BlockSpec((1, d), full),
                  pl.BlockSpec((1, d), full),
                  pl.BlockSpec(memory_space=pl.ANY)],
        out_specs=pl.BlockSpec((tm, d), tok),
        scratch_shapes=[pltpu.VMEM((2, SLAB_ROWS, d), F32), pltpu.SemaphoreType.DMA((2,))],
    )
    return pl.pallas_call(
        _combine_kernel,
        grid_spec=grid_spec,
        out_shape=jax.ShapeDtypeStruct((n, d), F32),
        compiler_params=pltpu.CompilerParams(dimension_semantics=("arbitrary",),
                                             vmem_limit_bytes=VMEM_LIMIT),
        name="combine",
    )(*tables, x1, gates, mod, g, b, y_buf)


def _layer_weights(w_in, w_uq, w_ukv, w_o):
    d = w_in.shape[0]
    w1 = w_in[:, :3 * SB_WIDTH].astype(BF16)
    lat = 3 * SB_WIDTH + MLA_Q_RANK + MLA_KV_RANK
    w2 = jnp.concatenate([w_in[:, 3 * SB_WIDTH:lat], jnp.zeros((d, ROPE_LANE0), F32), w_in[:, lat:],
                          jnp.zeros((d, LANES - ROPE_LANE0 - MLA_ROPE_DIM), F32)], axis=1).astype(BF16)
    uq = w_uq.reshape(MLA_Q_RANK, MLA_HEADS, MLA_NOPE_DIM + MLA_ROPE_DIM)
    uq = jnp.pad(uq, ((0, 0), (0, 0), (0, LANES - MLA_NOPE_DIM - MLA_ROPE_DIM)))
    wuq = uq.reshape(MLA_Q_RANK, MLA_HEADS * LANES).astype(BF16)
    ukv = w_ukv.reshape(MLA_KV_RANK, MLA_HEADS, MLA_NOPE_DIM + MLA_V_DIM)
    uk = jnp.pad(ukv[:, :, :MLA_NOPE_DIM], ((0, 0), (0, 0), (0, LANES - MLA_NOPE_DIM)))
    wukv = jnp.concatenate([uk.reshape(MLA_KV_RANK, MLA_HEADS * LANES),
                            ukv[:, :, MLA_NOPE_DIM:].reshape(MLA_KV_RANK, MLA_WIDTH)], axis=1).astype(BF16)
    return w1, w2, wuq, wukv, w_o[:SB_WIDTH].astype(BF16), w_o[SB_WIDTH:].astype(BF16)


def _router_weights(router_w, router_bias):
    d = router_w.shape[0]
    rw = router_w.reshape(d, N_GROUPS, EXPERTS_PER_GROUP).transpose(0, 2, 1).reshape(d, N_EXPERTS)
    hi = rw.astype(BF16)
    lo = (rw - hi.astype(F32)).astype(BF16)
    pad = jnp.zeros((d, LANES - N_EXPERTS), BF16)
    rwcat = jnp.concatenate([hi, pad, lo, pad], axis=1)
    rb = router_bias.reshape(N_GROUPS, EXPERTS_PER_GROUP).T.reshape(N_EXPERTS, 1)
    return rwcat, rb


def _dispatch_plan(counts, n_blocks):
    bm = DISPATCH_BLOCK
    gran = (counts + GRANULE - 1) // GRANULE
    slab_start = jnp.cumsum(gran, axis=1) - gran
    per_expert = jnp.sum(gran, axis=0) * GRANULE
    padded = (per_expert + bm - 1) // bm * bm
    pend = jnp.cumsum(padded)
    buf_start = (pend - padded) // GRANULE + jnp.cumsum(gran, axis=0) - gran
    tables = (gran.reshape(-1), slab_start.reshape(-1), buf_start.reshape(-1), jnp.sum(gran, axis=1))
    tails = ((pend - padded + per_expert) // GRANULE, (padded - per_expert) // GRANULE)
    block_row = jnp.arange(n_blocks, dtype=jnp.int32) * bm
    block_e = jnp.minimum(jnp.sum(block_row[:, None] >= pend[None, :], axis=1), N_EXPERTS - 1)
    n_used = (pend[-1] // bm).reshape(1)
    as_i32 = lambda ts: tuple(t.astype(jnp.int32) for t in ts)
    return as_i32(tables), as_i32(tails), block_e.astype(jnp.int32), n_used.astype(jnp.int32)


def kernel(x, c, positions, ada_w, ada_b, w_in, q_norm, kv_norm, w_uq, w_ukv, w_o, ln1_g, ln1_b,
           router_w, router_bias, w_gate, w_up, w_down, ln2_g, ln2_b):
    batch, seq, d = x.shape
    n = batch * seq
    depth = ada_w.shape[0]
    assert seq % TOKEN_TILE == 0 and seq % ATT_TILE == 0

    mod_all = _ada_call(c, ada_w, ada_b).reshape(depth, batch, 6, d)
    cos_t, sin_t = _rope_call(positions)
    rwcat, rb = _router_weights(router_w, router_bias)
    t = ATT_TILE
    tri = (jnp.arange(t)[:, None] >= jnp.arange(t)[None, :]).astype(BF16)
    tri = jnp.concatenate([tri, tri], axis=0)
    tm = TOKEN_TILE
    ut = (jnp.arange(tm)[:, None] < jnp.arange(tm)[None, :]).astype(BF16)
    max_rows = n * TOP_K + (n // tm) * N_EXPERTS * (GRANULE - 1) + N_EXPERTS * (DISPATCH_BLOCK - 1)
    n_blocks = -(-max_rows // DISPATCH_BLOCK)

    x2d = x.reshape(n, d)
    for l in range(depth):
        mod = mod_all[l]
        w1, w2, wuq, wukv, wo_sb, wo_mla = _layer_weights(w_in[l], w_uq[l], w_ukv[l], w_o[l])
        sbq, sbk, sbv, mq, mk, mv = _proj_call(x2d, mod, cos_t, sin_t, w1, w2, q_norm[l].reshape(1, -1),
                                               kv_norm[l].reshape(1, -1), wuq, wukv, seq)
        sb_out = _sb_call(sbq, sbk, sbv, tri, batch, seq)
        mla_out = _mla_call(mq, mk, mv, batch, seq)
        x1, h2, route, gates, counts = _post_call(sb_out, mla_out, x2d, mod, wo_sb, wo_mla,
                                                  ln1_g[l].reshape(1, d), ln1_b[l].reshape(1, d),
                                                  rwcat, rb, ut, seq)
        tables, tails, block_e, n_used = _dispatch_plan(counts[:, :, 0], n_blocks)
        buf = _dispatch_call(tables, tails, n_used, route, h2, n_blocks * DISPATCH_BLOCK)
        y_buf = _ffn_call(block_e, n_used, buf, w_gate, w_up, w_down, l)
        x2d = _combine_call(tables, x1, gates, mod, ln2_g[l].reshape(1, d), ln2_b[l].reshape(1, d), y_buf, seq)
    return x2d.reshape(batch, seq, d)
```

```python
import jax
import jax.numpy as jnp
from jax import lax
from jax.experimental import pallas as pl
from jax.experimental.pallas import tpu as pltpu

F32 = jnp.float32
BF16 = jnp.bfloat16

D_MODEL = 1024
DEPTH = 2
SB_HEADS = 8
SB_HEAD_DIM = 64
SB_WIDTH = SB_HEADS * SB_HEAD_DIM
MLA_HEADS = 8
MLA_NOPE_DIM = 64
MLA_ROPE_DIM = 32
MLA_V_DIM = 64
MLA_Q_RANK = 256
MLA_KV_RANK = 128
MLA_WIDTH = MLA_HEADS * MLA_V_DIM
ROPE_BASE = 10000.0
N_EXPERTS = 32
N_GROUPS = 8
EXPERTS_PER_GROUP = 4
TOP_K = 2
D_EXPERT = 256
DISPATCH_BLOCK = 512
DEEPNORM_ALPHA = (2 * DEPTH) ** 0.25
LN_EPS = 1e-5
RMS_EPS = 1e-6
LOG2_E = 1.4426950408889634
UNDERFLOW_LOG2 = 160.0

LANES = 128
HALF_ROPE = MLA_ROPE_DIM // 2
ROPE_LANE0 = MLA_NOPE_DIM
VMEM_LIMIT = 56 * 1024 * 1024

TOKEN_TILE = 512
ATT_TILE = 256
GRANULE = 8
GRANULE_SHIFT = 3
TILE_GROUP = 4
MXU_DIM = 256
SLAB_ROWS = -(-(TOKEN_TILE * TOP_K + N_EXPERTS * (GRANULE - 1)) // MXU_DIM) * MXU_DIM


def _dot(a, b):
    return jnp.dot(a, b, preferred_element_type=F32)


def _dot_nt(a, b):
    return lax.dot_general(a, b, (((1,), (1,)), ((), ())), preferred_element_type=F32)


def _split_bf16(v):
    hi = v.astype(BF16)
    lo = (v - hi.astype(F32)).astype(BF16)
    return hi, lo


def _standardize(x):
    mu = jnp.mean(x, axis=-1, keepdims=True)
    xc = x - mu
    var = jnp.mean(xc * xc, axis=-1, keepdims=True)
    return xc * lax.rsqrt(var + LN_EPS)


def _ada_kernel(c_ref, w_ref, b_ref, o_ref):
    c = c_ref[...]
    ca = c * jax.nn.sigmoid(c)
    ca_hi, ca_lo = _split_bf16(ca)
    w_hi, w_lo = _split_bf16(w_ref[...])
    o_ref[...] = _dot(ca_hi, w_hi) + _dot(ca_lo, w_hi) + _dot(ca_hi, w_lo) + b_ref[...]


def _ada_call(c, ada_w, ada_b):
    depth, d, n6 = ada_w.shape
    b = c.shape[0]
    tn = 1536
    return pl.pallas_call(
        _ada_kernel,
        grid=(depth, n6 // tn),
        in_specs=[pl.BlockSpec((b, d), lambda l, j: (0, 0)),
                  pl.BlockSpec((None, d, tn), lambda l, j: (l, 0, j)),
                  pl.BlockSpec((None, 1, tn), lambda l, j: (l, 0, j))],
        out_specs=pl.BlockSpec((None, b, tn), lambda l, j: (l, 0, j)),
        out_shape=jax.ShapeDtypeStruct((depth, b, n6), F32),
        compiler_params=pltpu.CompilerParams(vmem_limit_bytes=VMEM_LIMIT),
        name="ada",
    )(c, ada_w, ada_b.reshape(depth, 1, n6))


def _rope_kernel(pos_ref, invf_ref, cos_ref, sin_ref):
    ang = pos_ref[...].astype(F32) * invf_ref[...]
    lane = lax.broadcasted_iota(jnp.int32, ang.shape, 1)
    rot = (lane >= ROPE_LANE0) & (lane < ROPE_LANE0 + MLA_ROPE_DIM)
    cos_ref[...] = jnp.where(rot, jnp.cos(ang), 1.0)
    sin_ref[...] = jnp.where(rot, jnp.sin(ang), 0.0)


def _rope_call(positions):
    n = positions.size
    tm = TOKEN_TILE
    inv_freq = ROPE_BASE ** (-jnp.arange(0, MLA_ROPE_DIM, 2, dtype=F32) / MLA_ROPE_DIM)
    invf = jnp.zeros((1, LANES), F32)
    invf = invf.at[0, ROPE_LANE0:ROPE_LANE0 + HALF_ROPE].set(inv_freq)
    invf = invf.at[0, ROPE_LANE0 + HALF_ROPE:ROPE_LANE0 + MLA_ROPE_DIM].set(inv_freq)
    return pl.pallas_call(
        _rope_kernel,
        grid=(n // tm,),
        in_specs=[pl.BlockSpec((tm, 1), lambda i: (i, 0)),
                  pl.BlockSpec((1, LANES), lambda i: (0, 0))],
        out_specs=[pl.BlockSpec((tm, LANES), lambda i: (i, 0)),
                   pl.BlockSpec((tm, LANES), lambda i: (i, 0))],
        out_shape=[jax.ShapeDtypeStruct((n, LANES), F32)] * 2,
        name="rope_tables",
    )(positions.reshape(n, 1), invf)


def _proj_kernel(x_ref, mod_ref, cos_ref, sin_ref, w1_ref, w2_ref, qn_ref, kvn_ref, wuq_ref,
                 wukv_ref, sbq_ref, sbk_ref, sbv_ref, mq_ref, mk_ref, mv_ref):
    mod = mod_ref[...]
    h = _standardize(x_ref[...]) * (1.0 + mod[1:2]) + mod[0:1]
    hb = h.astype(BF16)

    p1 = _dot(hb, w1_ref[...])
    sbq_ref[...] = (p1[:, :SB_WIDTH] * (SB_HEAD_DIM ** -0.5 * LOG2_E)).astype(BF16)
    sbk_ref[...] = p1[:, SB_WIDTH:2 * SB_WIDTH].astype(BF16)
    sbv_ref[...] = p1[:, 2 * SB_WIDTH:].astype(BF16)

    p2 = _dot(hb, w2_ref[...])
    q_lat = p2[:, :MLA_Q_RANK]
    kv_lat = p2[:, MLA_Q_RANK:MLA_Q_RANK + MLA_KV_RANK]
    k_rope = p2[:, MLA_Q_RANK + MLA_KV_RANK:]

    qn = q_lat * lax.rsqrt(jnp.mean(q_lat * q_lat, axis=-1, keepdims=True) + RMS_EPS) * qn_ref[...]
    kvn = kv_lat * lax.rsqrt(jnp.mean(kv_lat * kv_lat, axis=-1, keepdims=True) + RMS_EPS) * kvn_ref[...]
    q = _dot(qn.astype(BF16), wuq_ref[...])
    kv = _dot(kvn.astype(BF16), wukv_ref[...])

    cos = cos_ref[...]
    sin = sin_ref[...]
    lane = lax.broadcasted_iota(jnp.int32, cos.shape, 1)
    second = lane >= ROPE_LANE0 + HALF_ROPE
    sin_up = jnp.where(second, sin, 0.0)
    sin_dn = jnp.where(second, 0.0, -sin)

    def rope(t):
        return (t * cos + pltpu.roll(t, HALF_ROPE, 1) * sin_up
                + pltpu.roll(t, LANES - HALF_ROPE, 1) * sin_dn)

    kr = rope(k_rope)
    mla_scale = (MLA_NOPE_DIM + MLA_ROPE_DIM) ** -0.5 * LOG2_E
    for hd in range(MLA_HEADS):
        sl = slice(hd * LANES, (hd + 1) * LANES)
        mq_ref[:, sl] = (rope(q[:, sl]) * mla_scale).astype(BF16)
        mk_ref[:, sl] = (kv[:, sl] + kr).astype(BF16)
    mv_ref[...] = kv[:, MLA_HEADS * LANES:].astype(BF16)


def _proj_call(x2d, mod, cos_t, sin_t, w1, w2, qn, kvn, wuq, wukv, seq):
    n, d = x2d.shape
    tm = TOKEN_TILE
    per_b = seq // tm
    tok = lambda i: (i, 0)
    full = lambda i: (0, 0)
    widths = (SB_WIDTH, SB_WIDTH, SB_WIDTH, MLA_HEADS * LANES, MLA_HEADS * LANES, MLA_WIDTH)
    return pl.pallas_call(
        _proj_kernel,
        grid=(n // tm,),
        in_specs=[pl.BlockSpec((tm, d), tok),
                  pl.BlockSpec((None, 6, d), lambda i: (i // per_b, 0, 0)),
                  pl.BlockSpec((tm, LANES), tok),
                  pl.BlockSpec((tm, LANES), tok),
                  pl.BlockSpec(w1.shape, full),
                  pl.BlockSpec(w2.shape, full),
                  pl.BlockSpec(qn.shape, full),
                  pl.BlockSpec(kvn.shape, full),
                  pl.BlockSpec(wuq.shape, full),
                  pl.BlockSpec(wukv.shape, full)],
        out_specs=[pl.BlockSpec((tm, w), tok) for w in widths],
        out_shape=[jax.ShapeDtypeStruct((n, w), BF16) for w in widths],
        compiler_params=pltpu.CompilerParams(vmem_limit_bytes=VMEM_LIMIT),
        name="proj",
    )(x2d, mod, cos_t, sin_t, w1, w2, qn, kvn, wuq, wukv)


def _causal_tiles(i, step, state):
    first = i % TILE_GROUP + 1

    def first_step(count):
        tiles = tuple(i - d for d in range(count))
        return lambda s: step(tiles, s, (True,) + (False,) * (count - 1))

    state = lax.switch(first - 1, [first_step(c) for c in range(1, TILE_GROUP + 1)], state)
    top = i - first

    def group(p, s):
        j = top - TILE_GROUP * p
        return step(tuple(j - d for d in range(TILE_GROUP)), s, (False,) * TILE_GROUP)

    return lax.fori_loop(0, (i + 1 - first) // TILE_GROUP, group, state)


def _causal_tiles_until(i, step, state, live):
    state = lax.cond(i >= 1,
                     lambda s: step((i, i - 1), s, (True, False)),
                     lambda s: step((i,), s, (True,)), state)

    def more(carry):
        return (carry[0] >= 1) & live(carry[1])

    def pair(carry):
        j, s = carry
        return j - 2, step((j, j - 1), s, (False, False))

    j, state = lax.while_loop(more, pair, (i - 2, state))
    return lax.cond((j == 0) & live(state), lambda s: step((0,), s, (False,)), lambda s: s, state)


def _sb_kernel(q_ref, k_ref, v_ref, tri_ref, o_ref):
    t = ATT_TILE
    tri2 = tri_ref[...]
    lane = lax.broadcasted_iota(jnp.int32, (1, LANES), 1)
    row = lax.broadcasted_iota(jnp.int32, (t, t), 0)
    col = lax.broadcasted_iota(jnp.int32, (t, t), 1)
    strict = col < row
    mine = [lane < SB_HEAD_DIM, lane >= SB_HEAD_DIM]

    def step(qh, js, state, diagonal):
        chains = [(j, h, d) for j, d in zip(js, diagonal) for h in range(2)]
        ks = [k_ref[pl.ds(j * t, t), :] for j in js]
        zs = [_dot_nt(qh[h], ks[n]) for n in range(len(js)) for h in range(2)]
        nlks, pieces = [], []
        for (_, _, diag), z in zip(chains, zs):
            nlk = jnp.maximum(z, 0.0) + jnp.log(1.0 + jnp.exp2(-jnp.abs(z))) * LOG2_E
            if diag:
                nlk = jnp.where(strict, nlk, 0.0)
            nlks.append(nlk)
            pieces.append(jnp.concatenate(_split_bf16(nlk), axis=1))
        laters = [_dot(p, tri2) for p in pieces]
        ncarry = [state[h][0] for h in range(2)]
        ws = []
        for (j, h, diag), z, nlk, later in zip(chains, zs, nlks, laters):
            w = jnp.exp2((z - ncarry[h]) - later)
            if diag:
                w = jnp.where(strict, w, 0.0)
            ws.append(w.astype(BF16))
            ncarry[h] = ncarry[h] + jnp.sum(nlk, axis=1, keepdims=True)
        acc = [state[h][1] for h in range(2)]
        for (j, h, _), w in zip(chains, ws):
            acc[h] = acc[h] + _dot(w, v_ref[pl.ds(j * t, t), :])
        return tuple((ncarry[h], acc[h]) for h in range(2))

    def live(state):
        return jnp.minimum(jnp.min(state[0][0]), jnp.min(state[1][0])) < UNDERFLOW_LOG2

    def q_tile(i, carry):
        rows = pl.ds(pl.multiple_of(i * t, t), t)
        q = q_ref[rows, :]
        qh = [jnp.where(m, q, jnp.zeros_like(q)) for m in mine]
        init = tuple((jnp.zeros((t, 1), F32), jnp.zeros((t, LANES), F32)) for _ in range(2))
        state = _causal_tiles_until(i, lambda js, s, d: step(qh, js, s, d), init, live)
        o_ref[rows, :] = jnp.where(mine[0], state[0][1], state[1][1]).astype(o_ref.dtype)
        return carry

    lax.fori_loop(0, q_ref.shape[0] // t, q_tile, 0)


def _sb_call(q, k, v, tri, batch, seq):
    t = ATT_TILE
    pairs = SB_WIDTH // LANES
    seq_spec = pl.BlockSpec((seq, LANES), lambda b, p: (b, p))
    return pl.pallas_call(
        _sb_kernel,
        grid=(batch, pairs),
        in_specs=[seq_spec, seq_spec, seq_spec,
                  pl.BlockSpec((2 * t, t), lambda b, p: (0, 0))],
        out_specs=seq_spec,
        out_shape=jax.ShapeDtypeStruct(q.shape, BF16),
        compiler_params=pltpu.CompilerParams(vmem_limit_bytes=VMEM_LIMIT),
        name="sb_attention",
    )(q, k, v, tri)


def _mla_kernel(q_ref, k_ref, v_ref, o_ref):
    t = ATT_TILE
    lane = lax.broadcasted_iota(jnp.int32, (1, LANES), 1)
    row = lax.broadcasted_iota(jnp.int32, (t, t), 0)
    col = lax.broadcasted_iota(jnp.int32, (t, t), 1)
    causal = col <= row

    def step(qh, js, state, diagonal):
        ss = []
        for hd in range(2):
            for j, diag in zip(js, diagonal):
                s = _dot_nt(qh[hd], k_ref[pl.ds(j * t, t), hd * LANES:(hd + 1) * LANES])
                ss.append(jnp.where(causal, s, -jnp.inf) if diag else s)
        n = len(js)
        ps, new_state = [], []
        for hd in range(2):
            m, l, acc = state[hd]
            mine_s = ss[hd * n:(hd + 1) * n]
            m_new = m
            for s in mine_s:
                m_new = jnp.maximum(m_new, jnp.max(s, axis=1, keepdims=True))
            a = jnp.exp2(m - m_new)
            l = a * l
            for s in mine_s:
                p = jnp.exp2(s - m_new)
                l = l + jnp.sum(p, axis=1, keepdims=True)
                ps.append(p.astype(BF16))
            new_state.append([m_new, l, a * acc])
        for hd in range(2):
            for idx, j in enumerate(js):
                new_state[hd][2] = new_state[hd][2] + _dot(ps[hd * n + idx], v_ref[pl.ds(j * t, t), :])
        return tuple(tuple(st) for st in new_state)

    def q_tile(i, carry):
        rows = pl.ds(pl.multiple_of(i * t, t), t)
        qh = [q_ref[rows, hd * LANES:(hd + 1) * LANES] for hd in range(2)]
        init = tuple((jnp.full((t, 1), -jnp.inf, F32), jnp.zeros((t, 1), F32), jnp.zeros((t, LANES), F32))
                     for _ in range(2))
        state = _causal_tiles(i, lambda js, s, d: step(qh, js, s, d), init)
        out = jnp.where(lane < MLA_V_DIM, state[0][2] / state[0][1], state[1][2] / state[1][1])
        o_ref[rows, :] = out.astype(o_ref.dtype)
        return carry

    lax.fori_loop(0, q_ref.shape[0] // t, q_tile, 0)


def _mla_call(q, k, v, batch, seq):
    pairs = MLA_WIDTH // LANES
    return pl.pallas_call(
        _mla_kernel,
        grid=(batch, pairs),
        in_specs=[pl.BlockSpec((seq, 2 * LANES), lambda b, p: (b, p)),
                  pl.BlockSpec((seq, 2 * LANES), lambda b, p: (b, p)),
                  pl.BlockSpec((seq, LANES), lambda b, p: (b, p))],
        out_specs=pl.BlockSpec((seq, LANES), lambda b, p: (b, p)),
        out_shape=jax.ShapeDtypeStruct(v.shape, BF16),
        compiler_params=pltpu.CompilerParams(vmem_limit_bytes=VMEM_LIMIT),
        name="mla_attention",
    )(q, k, v)


def _top2_sum(a, b, c, d):
    hi1, lo1 = jnp.maximum(a, b), jnp.minimum(a, b)
    hi2, lo2 = jnp.maximum(c, d), jnp.minimum(c, d)
    return jnp.maximum(hi1, hi2) + jnp.maximum(jnp.minimum(hi1, hi2), jnp.maximum(lo1, lo2))


def _first_argmax4(v):
    m = jnp.maximum(jnp.maximum(v[0], v[1]), jnp.maximum(v[2], v[3]))
    return jnp.where(v[0] == m, 0, jnp.where(v[1] == m, 1, jnp.where(v[2] == m, 2, 3)))


def _pick4(idx, v):
    return jnp.where(idx == 0, v[0], jnp.where(idx == 1, v[1], jnp.where(idx == 2, v[2], v[3])))


def _post_kernel(sb_ref, mla_ref, x_ref, mod_ref, wo_sb_ref, wo_mla_ref, g_ref, b_ref, rw_ref,
                 rb_ref, ut_ref, x1_ref, h2_ref, route_ref, gate_ref, cnt_ref):
    tm = x_ref.shape[0]
    mod = mod_ref[...]
    mix = _dot(sb_ref[...], wo_sb_ref[...]) + _dot(mla_ref[...], wo_mla_ref[...])
    x1 = _standardize(DEEPNORM_ALPHA * x_ref[...] + (1.0 + mod[2:3]) * mix) * g_ref[...] + b_ref[...]
    x1_ref[...] = x1
    h2 = _standardize(x1) * (1.0 + mod[4:5]) + mod[3:4]

    h_hi, h_lo = _split_bf16(h2)
    h2_ref[...] = h_hi
    rw = rw_ref[...]
    big = _dot(h_hi, rw)
    logits = big[:, :LANES] + big[:, LANES:] + _dot(h_lo, rw[:, :LANES])
    lt = logits.T[:N_EXPERTS]
    scores = jax.nn.sigmoid(lt)
    biased = scores + rb_ref[...]
    sc = [scores[N_GROUPS * p:N_GROUPS * (p + 1)] for p in range(EXPERTS_PER_GROUP)]
    bi = [biased[N_GROUPS * p:N_GROUPS * (p + 1)] for p in range(EXPERTS_PER_GROUP)]

    group_score = _top2_sum(*bi)
    gidx = lax.broadcasted_iota(jnp.int32, group_score.shape, 0)
    best = jnp.max(group_score, axis=0, keepdims=True)
    g_sel = jnp.min(jnp.where(group_score == best, gidx, N_GROUPS), axis=0, keepdims=True)
    in_sel = gidx == g_sel
    vb = [jnp.sum(jnp.where(in_sel, b, 0.0), axis=0, keepdims=True) for b in bi]
    vs = [jnp.sum(jnp.where(in_sel, s, 0.0), axis=0, keepdims=True) for s in sc]
    l1 = _first_argmax4(vb)
    vb2 = [jnp.where(l1 == p, -jnp.inf, vb[p]) for p in range(EXPERTS_PER_GROUP)]
    l2 = _first_argmax4(vb2)
    s1 = _pick4(l1, vs)
    s2 = _pick4(l2, vs)
    tot = s1 + s2
    e1 = g_sel * EXPERTS_PER_GROUP + l1
    e2 = g_sel * EXPERTS_PER_GROUP + l2

    eidx = lax.broadcasted_iota(jnp.int32, (N_EXPERTS, tm), 0)
    hit1 = eidx == e1
    hit2 = eidx == e2
    onehot = jnp.where(hit1 | hit2, 1.0, 0.0)
    before = _dot(onehot.astype(BF16), ut_ref[...])
    count = jnp.sum(onehot, axis=1, keepdims=True).astype(jnp.int32)
    granules = lax.shift_right_logical(count + (GRANULE - 1), GRANULE_SHIFT)
    lower = (lax.broadcasted_iota(jnp.int32, (N_EXPERTS, N_EXPERTS), 1)
             < lax.broadcasted_iota(jnp.int32, (N_EXPERTS, N_EXPERTS), 0))
    gran_f = jnp.broadcast_to(granules.astype(F32), (N_EXPERTS, LANES)).astype(BF16)
    start = _dot(jnp.where(lower, 1.0, 0.0).astype(BF16), gran_f)[:, 0:1] * GRANULE
    row_of = before + start
    p1 = jnp.sum(jnp.where(hit1, row_of, 0.0), axis=0, keepdims=True)
    p2 = jnp.sum(jnp.where(hit2, row_of, 0.0), axis=0, keepdims=True)
    cnt_ref[...] = jnp.broadcast_to(count, cnt_ref.shape)

    r8 = lax.broadcasted_iota(jnp.int32, (8, tm), 0)
    route_ref[...] = jnp.where(r8 == 0, p1.astype(jnp.int32), jnp.where(r8 == 1, p2.astype(jnp.int32), 0))
    r128 = lax.broadcasted_iota(jnp.int32, (LANES, tm), 0)
    cols = jnp.where(r128 == 0, s1 / tot, jnp.where(r128 == 1, s2 / tot,
                                                     jnp.where(r128 == 2, p1, jnp.where(r128 == 3, p2, 0.0))))
    gate_ref[...] = cols.T


def _post_call(sb_out, mla_out, x2d, mod, wo_sb, wo_mla, g, b, rw, rb, ut, seq):
    n, d = x2d.shape
    tm = TOKEN_TILE
    per_b = seq // tm
    tok = lambda i: (i, 0)
    full = lambda i: (0, 0)
    return pl.pallas_call(
        _post_kernel,
        grid=(n // tm,),
        in_specs=[pl.BlockSpec((tm, SB_WIDTH), tok),
                  pl.BlockSpec((tm, MLA_WIDTH), tok),
                  pl.BlockSpec((tm, d), tok),
                  pl.BlockSpec((None, 6, d), lambda i: (i // per_b, 0, 0)),
                  pl.BlockSpec(wo_sb.shape, full),
                  pl.BlockSpec(wo_mla.shape, full),
                  pl.BlockSpec((1, d), full),
                  pl.BlockSpec((1, d), full),
                  pl.BlockSpec(rw.shape, full),
                  pl.BlockSpec(rb.shape, full),
                  pl.BlockSpec(ut.shape, full)],
        out_specs=[pl.BlockSpec((tm, d), tok),
                   pl.BlockSpec((tm, d), tok),
                   pl.BlockSpec((8, tm), lambda i: (0, i)),
                   pl.BlockSpec((tm, LANES), tok),
                   pl.BlockSpec((None, N_EXPERTS, LANES), lambda i: (i, 0, 0))],
        out_shape=[jax.ShapeDtypeStruct((n, d), F32),
                   jax.ShapeDtypeStruct((n, d), BF16),
                   jax.ShapeDtypeStruct((8, n), jnp.int32),
                   jax.ShapeDtypeStruct((n, LANES), F32),
                   jax.ShapeDtypeStruct((n // tm, N_EXPERTS, LANES), jnp.int32)],
        compiler_params=pltpu.CompilerParams(vmem_limit_bytes=VMEM_LIMIT),
        name="post_attention",
    )(sb_out, mla_out, x2d, mod, wo_sb, wo_mla, g, b, rw, rb, ut)


def _granule_copy(src_ref, src_g, dst_ref, dst_g, sem):
    src = src_ref.at[pl.ds(pl.multiple_of(src_g * GRANULE, GRANULE), GRANULE), :]
    dst = dst_ref.at[pl.ds(pl.multiple_of(dst_g * GRANULE, GRANULE), GRANULE), :]
    return pltpu.make_async_copy(src, dst, sem)


def _start_granules(count, copy):
    def even(p, carry):
        copy(2 * p).start(priority=0)
        return carry

    def odd(p, carry):
        copy(2 * p + 1).start(priority=1)
        return carry

    lax.fori_loop(0, lax.shift_right_logical(count + 1, 1), even, 0)
    lax.fori_loop(0, lax.shift_right_logical(count, 1), odd, 0)


def _wait_granules(count, vmem_ref, hbm_ref, sem):
    def wait(g, carry):
        _granule_copy(vmem_ref, 0, hbm_ref, 0, sem).wait()
        return carry

    lax.fori_loop(0, count, wait, 0)


def _block_copy(zero_ref, buf_ref, block, sem):
    dst = buf_ref.at[pl.ds(pl.multiple_of(block * DISPATCH_BLOCK, DISPATCH_BLOCK), DISPATCH_BLOCK), :]
    return pltpu.make_async_copy(zero_ref, dst, sem)


def _dispatch_kernel(ng_ref, ls_ref, gs_ref, tot_ref, tail_start_ref, tail_n_ref, n_used_ref, route_ref, h_ref,
                     buf_ref, slab_ref, zero_ref, sem):
    c = pl.program_id(0)
    last = pl.num_programs(0) - 1
    slot = c % 2
    tm = h_ref.shape[0]
    rows = slab_ref.shape[1]

    @pl.when(c == 0)
    def _():
        zero_ref[...] = jnp.zeros_like(zero_ref)

        def expert_tail(e, total):
            def granule(g, inner):
                _granule_copy(zero_ref, 0, buf_ref, tail_start_ref[e] + g, sem.at[2]).start()
                return inner

            lax.fori_loop(0, tail_n_ref[e], granule, 0)
            return total + tail_n_ref[e]

        n_tail = lax.fori_loop(0, N_EXPERTS, expert_tail, 0)
        n_blocks = buf_ref.shape[0] // DISPATCH_BLOCK

        def unused_block(b, carry):
            _block_copy(zero_ref, buf_ref, b, sem.at[3]).start()
            return carry

        lax.fori_loop(n_used_ref[0], n_blocks, unused_block, 0)
        _wait_granules(n_tail, zero_ref, buf_ref, sem.at[2])

        def wait_block(b, carry):
            _block_copy(zero_ref, buf_ref, 0, sem.at[3]).wait()
            return carry

        lax.fori_loop(n_used_ref[0], n_blocks, wait_block, 0)

    @pl.when(c >= 2)
    def _():
        _wait_granules(tot_ref[c - 2], slab_ref.at[slot], buf_ref, sem.at[slot])

    row = lax.broadcasted_iota(jnp.int32, (rows, tm), 0)
    route = route_ref[...]
    onehot = (row == route[0:1]) | (row == route[1:2])
    slab_ref[slot] = _dot(jnp.where(onehot, 1.0, 0.0).astype(BF16), h_ref[...])

    def expert(e, carry):
        idx = c * N_EXPERTS + e
        ls = ls_ref[idx]
        gs = gs_ref[idx]

        _start_granules(ng_ref[idx],
                        lambda g: _granule_copy(slab_ref.at[slot], ls + g, buf_ref, gs + g, sem.at[slot]))
        return carry

    lax.fori_loop(0, N_EXPERTS, expert, 0)

    @pl.when(c == last)
    def _():
        _wait_granules(tot_ref[c], slab_ref.at[slot], buf_ref, sem.at[slot])

        @pl.when(c >= 1)
        def _():
            _wait_granules(tot_ref[c - 1], slab_ref.at[1 - slot], buf_ref, sem.at[1 - slot])


def _dispatch_call(tables, tails, n_used, route, h2, buf_rows):
    n, d = h2.shape
    tm = TOKEN_TILE
    grid_spec = pltpu.PrefetchScalarGridSpec(
        num_scalar_prefetch=7,
        grid=(n // tm,),
        in_specs=[pl.BlockSpec((8, tm), lambda i, *_: (0, i)),
                  pl.BlockSpec((tm, d), lambda i, *_: (i, 0))],
        out_specs=pl.BlockSpec(memory_space=pl.ANY),
        scratch_shapes=[pltpu.VMEM((2, SLAB_ROWS, d), F32), pltpu.VMEM((DISPATCH_BLOCK, d), F32),
                        pltpu.SemaphoreType.DMA((4,))],
    )
    return pl.pallas_call(
        _dispatch_kernel,
        grid_spec=grid_spec,
        out_shape=jax.ShapeDtypeStruct((buf_rows, d), F32),
        compiler_params=pltpu.CompilerParams(dimension_semantics=("arbitrary",),
                                             vmem_limit_bytes=VMEM_LIMIT),
        name="dispatch",
    )(*tables, *tails, n_used, route, h2)


def _ffn_kernel(be_ref, nb_ref, x_ref, wg_ref, wu_ref, wd_ref, y_ref, wg_s, wu_s, wd_s):
    i = pl.program_id(0)
    prev = be_ref[jnp.maximum(i - 1, 0)]

    @pl.when((i == 0) | (be_ref[i] != prev))
    def _():
        wg_s[...] = wg_ref[...].astype(BF16)
        wu_s[...] = wu_ref[...].astype(BF16)
        wd_s[...] = wd_ref[...].astype(BF16)

    @pl.when(i < nb_ref[0])
    def _():
        xb = x_ref[...].astype(BF16)
        g = _dot(xb, wg_s[...])
        u = _dot(xb, wu_s[...])
        a = g * jax.nn.sigmoid(g) * u
        y_ref[...] = _dot(a.astype(BF16), wd_s[...])

    @pl.when(i >= nb_ref[0])
    def _():
        y_ref[...] = jnp.zeros_like(y_ref)


def _ffn_call(block_e, n_used, buf, w_gate, w_up, w_down, layer):
    rows, d = buf.shape
    bm = DISPATCH_BLOCK
    de = w_gate.shape[-1]
    grid_spec = pltpu.PrefetchScalarGridSpec(
        num_scalar_prefetch=2,
        grid=(rows // bm,),
        in_specs=[pl.BlockSpec((bm, d), lambda i, be, nb: (jnp.minimum(i, nb[0] - 1), 0)),
                  pl.BlockSpec((None, None, d, de), lambda i, be, nb: (layer, be[i], 0, 0)),
                  pl.BlockSpec((None, None, d, de), lambda i, be, nb: (layer, be[i], 0, 0)),
                  pl.BlockSpec((None, None, de, d), lambda i, be, nb: (layer, be[i], 0, 0))],
        out_specs=pl.BlockSpec((bm, d), lambda i, be, nb: (i, 0)),
        scratch_shapes=[pltpu.VMEM((d, de), BF16), pltpu.VMEM((d, de), BF16), pltpu.VMEM((de, d), BF16)],
    )
    return pl.pallas_call(
        _ffn_kernel,
        grid_spec=grid_spec,
        out_shape=jax.ShapeDtypeStruct((rows, d), F32),
        compiler_params=pltpu.CompilerParams(dimension_semantics=("arbitrary",),
                                             vmem_limit_bytes=VMEM_LIMIT),
        name="expert_ffn",
    )(block_e, n_used, buf, w_gate, w_up, w_down)


def _combine_kernel(ng_ref, ls_ref, gs_ref, tot_ref, x1_ref, gate_ref, mod_ref, g_ref, b_ref, y_hbm_ref,
                    o_ref, slab_ref, sem):
    c = pl.program_id(0)
    last = pl.num_programs(0) - 1
    slot = c % 2
    tm = x1_ref.shape[0]
    rows = slab_ref.shape[1]

    def fetch(tile, into):
        def expert(e, carry):
            idx = tile * N_EXPERTS + e
            ls = ls_ref[idx]
            gs = gs_ref[idx]

            _start_granules(ng_ref[idx],
                            lambda g: _granule_copy(y_hbm_ref, gs + g, slab_ref.at[into], ls + g, sem.at[into]))
            return carry

        lax.fori_loop(0, N_EXPERTS, expert, 0)

    @pl.when(c == 0)
    def _():
        slab_ref[...] = jnp.zeros_like(slab_ref)
        fetch(0, 0)

    @pl.when(c < last)
    def _():
        fetch(c + 1, 1 - slot)

    def wait(g, carry):
        _granule_copy(y_hbm_ref, 0, slab_ref.at[slot], 0, sem.at[slot]).wait()
        return carry

    lax.fori_loop(0, tot_ref[c], wait, 0)

    cols = gate_ref[...]
    lane = lax.broadcasted_iota(jnp.int32, (tm, rows), 1)
    ys = slab_ref[slot].astype(BF16)
    pick1 = jnp.where(lane == cols[:, 2:3].astype(jnp.int32), 1.0, 0.0).astype(BF16)
    pick2 = jnp.where(lane == cols[:, 3:4].astype(jnp.int32), 1.0, 0.0).astype(BF16)
    ffn = cols[:, 0:1] * _dot(pick1, ys) + cols[:, 1:2] * _dot(pick2, ys)
    mod = mod_ref[...]
    o_ref[...] = _standardize(DEEPNORM_ALPHA * x1_ref[...] + (1.0 + mod[5:6]) * ffn) * g_ref[...] + b_ref[...]


def _combine_call(tables, x1, gates, mod, g, b, y_buf, seq):
    n, d = x1.shape
    tm = TOKEN_TILE
    per_b = seq // tm
    tok = lambda i, *_: (i, 0)
    full = lambda i, *_: (0, 0)
    grid_spec = pltpu.PrefetchScalarGridSpec(
        num_scalar_prefetch=4,
        grid=(n // tm,),
        in_specs=[pl.BlockSpec((tm, d), tok),
                  pl.BlockSpec((tm, LANES), tok),
                  pl.BlockSpec((None, 6, d), lambda i, *_: (i // per_b, 0, 0)),
                  pl.BlockSpec((1, d), full),
                  pl.BlockSpec((1, d), full),
                  pl.BlockSpec(memory_space=pl.ANY)],
        out_specs=pl.BlockSpec((tm, d), tok),
        scratch_shapes=[pltpu.VMEM((2, SLAB_ROWS, d), F32), pltpu.SemaphoreType.DMA((2,))],
    )
    return pl.pallas_call(
        _combine_kernel,
        grid_spec=grid_spec,
        out_shape=jax.ShapeDtypeStruct((n, d), F32),
        compiler_params=pltpu.CompilerParams(dimension_semantics=("arbitrary",),
                                             vmem_limit_bytes=VMEM_LIMIT),
        name="combine",
    )(*tables, x1, gates, mod, g, b, y_buf)


def _layer_weights(w_in, w_uq, w_ukv, w_o):
    d = w_in.shape[0]
    w1 = w_in[:, :3 * SB_WIDTH].astype(BF16)
    lat = 3 * SB_WIDTH + MLA_Q_RANK + MLA_KV_RANK
    w2 = jnp.concatenate([w_in[:, 3 * SB_WIDTH:lat], jnp.zeros((d, ROPE_LANE0), F32), w_in[:, lat:],
                          jnp.zeros((d, LANES - ROPE_LANE0 - MLA_ROPE_DIM), F32)], axis=1).astype(BF16)
    uq = w_uq.reshape(MLA_Q_RANK, MLA_HEADS, MLA_NOPE_DIM + MLA_ROPE_DIM)
    uq = jnp.pad(uq, ((0, 0), (0, 0), (0, LANES - MLA_NOPE_DIM - MLA_ROPE_DIM)))
    wuq = uq.reshape(MLA_Q_RANK, MLA_HEADS * LANES).astype(BF16)
    ukv = w_ukv.reshape(MLA_KV_RANK, MLA_HEADS, MLA_NOPE_DIM + MLA_V_DIM)
    uk = jnp.pad(ukv[:, :, :MLA_NOPE_DIM], ((0, 0), (0, 0), (0, LANES - MLA_NOPE_DIM)))
    wukv = jnp.concatenate([uk.reshape(MLA_KV_RANK, MLA_HEADS * LANES),
                            ukv[:, :, MLA_NOPE_DIM:].reshape(MLA_KV_RANK, MLA_WIDTH)], axis=1).astype(BF16)
    return w1, w2, wuq, wukv, w_o[:SB_WIDTH].astype(BF16), w_o[SB_WIDTH:].astype(BF16)


def _router_weights(router_w, router_bias):
    d = router_w.shape[0]
    rw = router_w.reshape(d, N_GROUPS, EXPERTS_PER_GROUP).transpose(0, 2, 1).reshape(d, N_EXPERTS)
    hi = rw.astype(BF16)
    lo = (rw - hi.astype(F32)).astype(BF16)
    pad = jnp.zeros((d, LANES - N_EXPERTS), BF16)
    rwcat = jnp.concatenate([hi, pad, lo, pad], axis=1)
    rb = router_bias.reshape(N_GROUPS, EXPERTS_PER_GROUP).T.reshape(N_EXPERTS, 1)
    return rwcat, rb


def _dispatch_plan(counts, n_blocks):
    bm = DISPATCH_BLOCK
    gran = (counts + GRANULE - 1) // GRANULE
    slab_start = jnp.cumsum(gran, axis=1) - gran
    per_expert = jnp.sum(gran, axis=0) * GRANULE
    padded = (per_expert + bm - 1) // bm * bm
    pend = jnp.cumsum(padded)
    buf_start = (pend - padded) // GRANULE + jnp.cumsum(gran, axis=0) - gran
    tables = (gran.reshape(-1), slab_start.reshape(-1), buf_start.reshape(-1), jnp.sum(gran, axis=1))
    tails = ((pend - padded + per_expert) // GRANULE, (padded - per_expert) // GRANULE)
    block_row = jnp.arange(n_blocks, dtype=jnp.int32) * bm
    block_e = jnp.minimum(jnp.sum(block_row[:, None] >= pend[None, :], axis=1), N_EXPERTS - 1)
    n_used = (pend[-1] // bm).reshape(1)
    as_i32 = lambda ts: tuple(t.astype(jnp.int32) for t in ts)
    return as_i32(tables), as_i32(tails), block_e.astype(jnp.int32), n_used.astype(jnp.int32)


def kernel(x, c, positions, ada_w, ada_b, w_in, q_norm, kv_norm, w_uq, w_ukv, w_o, ln1_g, ln1_b,
           router_w, router_bias, w_gate, w_up, w_down, ln2_g, ln2_b):
    batch, seq, d = x.shape
    n = batch * seq
    depth = ada_w.shape[0]
    assert seq % TOKEN_TILE == 0 and seq % ATT_TILE == 0

    mod_all = _ada_call(c, ada_w, ada_b).reshape(depth, batch, 6, d)
    cos_t, sin_t = _rope_call(positions)
    rwcat, rb = _router_weights(router_w, router_bias)
    t = ATT_TILE
    tri = (jnp.arange(t)[:, None] >= jnp.arange(t)[None, :]).astype(BF16)
    tri = jnp.concatenate([tri, tri], axis=0)
    tm = TOKEN_TILE
    ut = (jnp.arange(tm)[:, None] < jnp.arange(tm)[None, :]).astype(BF16)
    max_rows = n * TOP_K + (n // tm) * N_EXPERTS * (GRANULE - 1) + N_EXPERTS * (DISPATCH_BLOCK - 1)
    n_blocks = -(-max_rows // DISPATCH_BLOCK)

    x2d = x.reshape(n, d)
    for l in range(depth):
        mod = mod_all[l]
        w1, w2, wuq, wukv, wo_sb, wo_mla = _layer_weights(w_in[l], w_uq[l], w_ukv[l], w_o[l])
        sbq, sbk, sbv, mq, mk, mv = _proj_call(x2d, mod, cos_t, sin_t, w1, w2, q_norm[l].reshape(1, -1),
                                               kv_norm[l].reshape(1, -1), wuq, wukv, seq)
        sb_out = _sb_call(sbq, sbk, sbv, tri, batch, seq)
        mla_out = _mla_call(mq, mk, mv, batch, seq)
        x1, h2, route, gates, counts = _post_call(sb_out, mla_out, x2d, mod, wo_sb, wo_mla,
                                                  ln1_g[l].reshape(1, d), ln1_b[l].reshape(1, d),
                                                  rwcat, rb, ut, seq)
        tables, tails, block_e, n_used = _dispatch_plan(counts[:, :, 0], n_blocks)
        buf = _dispatch_call(tables, tails, n_used, route, h2, n_blocks * DISPATCH_BLOCK)
        y_buf = _ffn_call(block_e, n_used, buf, w_gate, w_up, w_down, l)
        x2d = _combine_call(tables, x1, gates, mod, ln2_g[l].reshape(1, d), ln2_b[l].reshape(1, d), y_buf, seq)
    return x2d.reshape(batch, seq, d)
```

```python
import jax
import jax.numpy as jnp
from jax import lax
from jax.experimental import pallas as pl
from jax.experimental.pallas import tpu as pltpu

F32 = jnp.float32
BF16 = jnp.bfloat16

D_MODEL = 1024
DEPTH = 2
SB_HEADS = 8
SB_HEAD_DIM = 64
SB_WIDTH = SB_HEADS * SB_HEAD_DIM
MLA_HEADS = 8
MLA_NOPE_DIM = 64
MLA_ROPE_DIM = 32
MLA_V_DIM = 64
MLA_Q_RANK = 256
MLA_KV_RANK = 128
MLA_WIDTH = MLA_HEADS * MLA_V_DIM
ROPE_BASE = 10000.0
N_EXPERTS = 32
N_GROUPS = 8
EXPERTS_PER_GROUP = 4
TOP_K = 2
D_EXPERT = 256
DISPATCH_BLOCK = 512
DEEPNORM_ALPHA = (2 * DEPTH) ** 0.25
LN_EPS = 1e-5
RMS_EPS = 1e-6
LOG2_E = 1.4426950408889634
UNDERFLOW_LOG2 = 160.0

LANES = 128
HALF_ROPE = MLA_ROPE_DIM // 2
ROPE_LANE0 = MLA_NOPE_DIM
VMEM_LIMIT = 56 * 1024 * 1024

TOKEN_TILE = 512
ATT_TILE = 256
GRANULE = 8
GRANULE_SHIFT = 3
TILE_GROUP = 4
MXU_DIM = 256
SLAB_ROWS = -(-(TOKEN_TILE * TOP_K + N_EXPERTS * (GRANULE - 1)) // MXU_DIM) * MXU_DIM


def _dot(a, b):
    return jnp.dot(a, b, preferred_element_type=F32)


def _dot_nt(a, b):
    return lax.dot_general(a, b, (((1,), (1,)), ((), ())), preferred_element_type=F32)


def _split_bf16(v):
    hi = v.astype(BF16)
    lo = (v - hi.astype(F32)).astype(BF16)
    return hi, lo


def _standardize(x):
    mu = jnp.mean(x, axis=-1, keepdims=True)
    xc = x - mu
    var = jnp.mean(xc * xc, axis=-1, keepdims=True)
    return xc * lax.rsqrt(var + LN_EPS)


def _ada_kernel(c_ref, w_ref, b_ref, o_ref):
    c = c_ref[...]
    ca = c * jax.nn.sigmoid(c)
    ca_hi, ca_lo = _split_bf16(ca)
    w_hi, w_lo = _split_bf16(w_ref[...])
    o_ref[...] = _dot(ca_hi, w_hi) + _dot(ca_lo, w_hi) + _dot(ca_hi, w_lo) + b_ref[...]


def _ada_call(c, ada_w, ada_b):
    depth, d, n6 = ada_w.shape
    b = c.shape[0]
    tn = 1536
    return pl.pallas_call(
        _ada_kernel,
        grid=(depth, n6 // tn),
        in_specs=[pl.BlockSpec((b, d), lambda l, j: (0, 0)),
                  pl.BlockSpec((None, d, tn), lambda l, j: (l, 0, j)),
                  pl.BlockSpec((None, 1, tn), lambda l, j: (l, 0, j))],
        out_specs=pl.BlockSpec((None, b, tn), lambda l, j: (l, 0, j)),
        out_shape=jax.ShapeDtypeStruct((depth, b, n6), F32),
        compiler_params=pltpu.CompilerParams(vmem_limit_bytes=VMEM_LIMIT),
        name="ada",
    )(c, ada_w, ada_b.reshape(depth, 1, n6))


def _rope_kernel(pos_ref, invf_ref, cos_ref, sin_ref):
    ang = pos_ref[...].astype(F32) * invf_ref[...]
    lane = lax.broadcasted_iota(jnp.int32, ang.shape, 1)
    rot = (lane >= ROPE_LANE0) & (lane < ROPE_LANE0 + MLA_ROPE_DIM)
    cos_ref[...] = jnp.where(rot, jnp.cos(ang), 1.0)
    sin_ref[...] = jnp.where(rot, jnp.sin(ang), 0.0)


def _rope_call(positions):
    n = positions.size
    tm = TOKEN_TILE
    inv_freq = ROPE_BASE ** (-jnp.arange(0, MLA_ROPE_DIM, 2, dtype=F32) / MLA_ROPE_DIM)
    invf = jnp.zeros((1, LANES), F32)
    invf = invf.at[0, ROPE_LANE0:ROPE_LANE0 + HALF_ROPE].set(inv_freq)
    invf = invf.at[0, ROPE_LANE0 + HALF_ROPE:ROPE_LANE0 + MLA_ROPE_DIM].set(inv_freq)
    return pl.pallas_call(
        _rope_kernel,
        grid=(n // tm,),
        in_specs=[pl.BlockSpec((tm, 1), lambda i: (i, 0)),
                  pl.BlockSpec((1, LANES), lambda i: (0, 0))],
        out_specs=[pl.BlockSpec((tm, LANES), lambda i: (i, 0)),
                   pl.BlockSpec((tm, LANES), lambda i: (i, 0))],
        out_shape=[jax.ShapeDtypeStruct((n, LANES), F32)] * 2,
        name="rope_tables",
    )(positions.reshape(n, 1), invf)


def _proj_kernel(x_ref, mod_ref, cos_ref, sin_ref, w1_ref, w2_ref, qn_ref, kvn_ref, wuq_ref,
                 wukv_ref, sbq_ref, sbk_ref, sbv_ref, mq_ref, mk_ref, mv_ref):
    mod = mod_ref[...]
    h = _standardize(x_ref[...]) * (1.0 + mod[1:2]) + mod[0:1]
    hb = h.astype(BF16)

    p1 = _dot(hb, w1_ref[...])
    sbq_ref[...] = (p1[:, :SB_WIDTH] * (SB_HEAD_DIM ** -0.5 * LOG2_E)).astype(BF16)
    sbk_ref[...] = p1[:, SB_WIDTH:2 * SB_WIDTH].astype(BF16)
    sbv_ref[...] = p1[:, 2 * SB_WIDTH:].astype(BF16)

    p2 = _dot(hb, w2_ref[...])
    q_lat = p2[:, :MLA_Q_RANK]
    kv_lat = p2[:, MLA_Q_RANK:MLA_Q_RANK + MLA_KV_RANK]
    k_rope = p2[:, MLA_Q_RANK + MLA_KV_RANK:]

    qn = q_lat * lax.rsqrt(jnp.mean(q_lat * q_lat, axis=-1, keepdims=True) + RMS_EPS) * qn_ref[...]
    kvn = kv_lat * lax.rsqrt(jnp.mean(kv_lat * kv_lat, axis=-1, keepdims=True) + RMS_EPS) * kvn_ref[...]
    q = _dot(qn.astype(BF16), wuq_ref[...])
    kv = _dot(kvn.astype(BF16), wukv_ref[...])

    cos = cos_ref[...]
    sin = sin_ref[...]
    lane = lax.broadcasted_iota(jnp.int32, cos.shape, 1)
    second = lane >= ROPE_LANE0 + HALF_ROPE
    sin_up = jnp.where(second, sin, 0.0)
    sin_dn = jnp.where(second, 0.0, -sin)

    def rope(t):
        return (t * cos + pltpu.roll(t, HALF_ROPE, 1) * sin_up
                + pltpu.roll(t, LANES - HALF_ROPE, 1) * sin_dn)

    kr = rope(k_rope)
    mla_scale = (MLA_NOPE_DIM + MLA_ROPE_DIM) ** -0.5 * LOG2_E
    for hd in range(MLA_HEADS):
        sl = slice(hd * LANES, (hd + 1) * LANES)
        mq_ref[:, sl] = (rope(q[:, sl]) * mla_scale).astype(BF16)
        mk_ref[:, sl] = (kv[:, sl] + kr).astype(BF16)
    vlane = lax.broadcasted_iota(jnp.int32, (1, MLA_HEADS * LANES), 1)
    quarter = lax.shift_right_logical(vlane, 6) & 3
    ones = jnp.where((quarter == 1) | (quarter == 2), 1.0, 0.0)
    mv_ref[...] = (kv[:, MLA_HEADS * LANES:] + ones).astype(BF16)


def _proj_call(x2d, mod, cos_t, sin_t, w1, w2, qn, kvn, wuq, wukv, seq):
    n, d = x2d.shape
    tm = TOKEN_TILE
    per_b = seq // tm
    tok = lambda i: (i, 0)
    full = lambda i: (0, 0)
    widths = (SB_WIDTH, SB_WIDTH, SB_WIDTH, MLA_HEADS * LANES, MLA_HEADS * LANES, MLA_HEADS * LANES)
    return pl.pallas_call(
        _proj_kernel,
        grid=(n // tm,),
        in_specs=[pl.BlockSpec((tm, d), tok),
                  pl.BlockSpec((None, 6, d), lambda i: (i // per_b, 0, 0)),
                  pl.BlockSpec((tm, LANES), tok),
                  pl.BlockSpec((tm, LANES), tok),
                  pl.BlockSpec(w1.shape, full),
                  pl.BlockSpec(w2.shape, full),
                  pl.BlockSpec(qn.shape, full),
                  pl.BlockSpec(kvn.shape, full),
                  pl.BlockSpec(wuq.shape, full),
                  pl.BlockSpec(wukv.shape, full)],
        out_specs=[pl.BlockSpec((tm, w), tok) for w in widths],
        out_shape=[jax.ShapeDtypeStruct((n, w), BF16) for w in widths],
        compiler_params=pltpu.CompilerParams(vmem_limit_bytes=VMEM_LIMIT),
        name="proj",
    )(x2d, mod, cos_t, sin_t, w1, w2, qn, kvn, wuq, wukv)


def _causal_tiles(i, step, state):
    first = i % TILE_GROUP + 1

    def first_step(count):
        tiles = tuple(i - d for d in range(count))
        return lambda s: step(tiles, s, (True,) + (False,) * (count - 1))

    state = lax.switch(first - 1, [first_step(c) for c in range(1, TILE_GROUP + 1)], state)
    top = i - first

    def group(p, s):
        j = top - TILE_GROUP * p
        return step(tuple(j - d for d in range(TILE_GROUP)), s, (False,) * TILE_GROUP)

    return lax.fori_loop(0, (i + 1 - first) // TILE_GROUP, group, state)


def _causal_tiles_until(i, step, state, live):
    state = lax.cond(i >= 1,
                     lambda s: step((i, i - 1), s, (True, False)),
                     lambda s: step((i,), s, (True,)), state)

    def more(carry):
        return (carry[0] >= 1) & live(carry[1])

    def pair(carry):
        j, s = carry
        return j - 2, step((j, j - 1), s, (False, False))

    j, state = lax.while_loop(more, pair, (i - 2, state))
    return lax.cond((j == 0) & live(state), lambda s: step((0,), s, (False,)), lambda s: s, state)


def _sb_kernel(q_ref, k_ref, v_ref, tri_ref, o_ref):
    t = ATT_TILE
    tri2 = tri_ref[...]
    lane = lax.broadcasted_iota(jnp.int32, (1, LANES), 1)
    row = lax.broadcasted_iota(jnp.int32, (t, t), 0)
    col = lax.broadcasted_iota(jnp.int32, (t, t), 1)
    strict = col < row
    mine = [lane < SB_HEAD_DIM, lane >= SB_HEAD_DIM]

    def step(qh, js, state, diagonal):
        chains = [(j, h, d) for j, d in zip(js, diagonal) for h in range(2)]
        ks = [k_ref[pl.ds(j * t, t), :] for j in js]
        zs = [_dot_nt(qh[h], ks[n]) for n in range(len(js)) for h in range(2)]
        nlks, pieces = [], []
        for (_, _, diag), z in zip(chains, zs):
            nlk = jnp.maximum(z, 0.0) + jnp.log(1.0 + jnp.exp2(-jnp.abs(z))) * LOG2_E
            if diag:
                nlk = jnp.where(strict, nlk, 0.0)
            nlks.append(nlk)
            pieces.append(jnp.concatenate(_split_bf16(nlk), axis=1))
        laters = [_dot(p, tri2) for p in pieces]
        ncarry = [state[h][0] for h in range(2)]
        ws = []
        for (j, h, diag), z, nlk, later in zip(chains, zs, nlks, laters):
            w = jnp.exp2((z - ncarry[h]) - later)
            if diag:
                w = jnp.where(strict, w, 0.0)
            ws.append(w.astype(BF16))
            ncarry[h] = ncarry[h] + jnp.sum(nlk, axis=1, keepdims=True)
        acc = [state[h][1] for h in range(2)]
        for (j, h, _), w in zip(chains, ws):
            acc[h] = acc[h] + _dot(w, v_ref[pl.ds(j * t, t), :])
        return tuple((ncarry[h], acc[h]) for h in range(2))

    def live(state):
        return jnp.minimum(jnp.min(state[0][0]), jnp.min(state[1][0])) < UNDERFLOW_LOG2

    def q_tile(i, carry):
        rows = pl.ds(pl.multiple_of(i * t, t), t)
        q = q_ref[rows, :]
        qh = [jnp.where(m, q, jnp.zeros_like(q)) for m in mine]
        init = tuple((jnp.zeros((t, 1), F32), jnp.zeros((t, LANES), F32)) for _ in range(2))
        state = _causal_tiles_until(i, lambda js, s, d: step(qh, js, s, d), init, live)
        o_ref[rows, :] = jnp.where(mine[0], state[0][1], state[1][1]).astype(o_ref.dtype)
        return carry

    lax.fori_loop(0, q_ref.shape[0] // t, q_tile, 0)


def _sb_call(q, k, v, tri, batch, seq):
    t = ATT_TILE
    pairs = SB_WIDTH // LANES
    seq_spec = pl.BlockSpec((seq, LANES), lambda b, p: (b, p))
    return pl.pallas_call(
        _sb_kernel,
        grid=(batch, pairs),
        in_specs=[seq_spec, seq_spec, seq_spec,
                  pl.BlockSpec((2 * t, t), lambda b, p: (0, 0))],
        out_specs=seq_spec,
        out_shape=jax.ShapeDtypeStruct(q.shape, BF16),
        compiler_params=pltpu.CompilerParams(vmem_limit_bytes=VMEM_LIMIT),
        name="sb_attention",
    )(q, k, v, tri)


def _mla_kernel(q_ref, k_ref, v_ref, o_ref):
    t = ATT_TILE
    lane = lax.broadcasted_iota(jnp.int32, (1, LANES), 1)
    row = lax.broadcasted_iota(jnp.int32, (t, t), 0)
    col = lax.broadcasted_iota(jnp.int32, (t, t), 1)
    causal = col <= row

    def step(qh, js, state, diagonal):
        ss = []
        for hd in range(2):
            for j, diag in zip(js, diagonal):
                s = _dot_nt(qh[hd], k_ref[pl.ds(j * t, t), hd * LANES:(hd + 1) * LANES])
                ss.append(jnp.where(causal, s, -jnp.inf) if diag else s)
        n = len(js)
        ps, new_state = [], []
        for hd in range(2):
            m, acc = state[hd]
            mine_s = ss[hd * n:(hd + 1) * n]
            m_new = m
            for s in mine_s:
                m_new = jnp.maximum(m_new, jnp.max(s, axis=1, keepdims=True))
            for s in mine_s:
                ps.append(jnp.exp2(s - m_new).astype(BF16))
            new_state.append([m_new, jnp.exp2(m - m_new) * acc])
        for hd in range(2):
            for idx, j in enumerate(js):
                v = v_ref[pl.ds(j * t, t), hd * LANES:(hd + 1) * LANES]
                new_state[hd][1] = new_state[hd][1] + _dot(ps[hd * n + idx], v)
        return tuple(tuple(st) for st in new_state)

    def q_tile(i, carry):
        rows = pl.ds(pl.multiple_of(i * t, t), t)
        qh = [q_ref[rows, hd * LANES:(hd + 1) * LANES] for hd in range(2)]
        init = tuple((jnp.full((t, 1), -jnp.inf, F32), jnp.zeros((t, LANES), F32)) for _ in range(2))
        state = _causal_tiles(i, lambda js, s, d: step(qh, js, s, d), init)
        acc0, acc1 = state[0][1], state[1][1]
        out = jnp.where(lane < MLA_V_DIM, acc0 / pltpu.roll(acc0, MLA_V_DIM, 1), acc1 / pltpu.roll(acc1, MLA_V_DIM, 1))
        o_ref[rows, :] = out.astype(o_ref.dtype)
        return carry

    lax.fori_loop(0, q_ref.shape[0] // t, q_tile, 0)


def _mla_call(q, k, v, batch, seq):
    pairs = MLA_WIDTH // LANES
    return pl.pallas_call(
        _mla_kernel,
        grid=(batch, pairs),
        in_specs=[pl.BlockSpec((seq, 2 * LANES), lambda b, p: (b, p)),
                  pl.BlockSpec((seq, 2 * LANES), lambda b, p: (b, p)),
                  pl.BlockSpec((seq, 2 * LANES), lambda b, p: (b, p))],
        out_specs=pl.BlockSpec((seq, LANES), lambda b, p: (b, p)),
        out_shape=jax.ShapeDtypeStruct((q.shape[0], MLA_WIDTH), BF16),
        compiler_params=pltpu.CompilerParams(vmem_limit_bytes=VMEM_LIMIT),
        name="mla_attention",
    )(q, k, v)


def _top2_sum(a, b, c, d):
    hi1, lo1 = jnp.maximum(a, b), jnp.minimum(a, b)
    hi2, lo2 = jnp.maximum(c, d), jnp.minimum(c, d)
    return jnp.maximum(hi1, hi2) + jnp.maximum(jnp.minimum(hi1, hi2), jnp.maximum(lo1, lo2))


def _first_argmax4(v):
    m = jnp.maximum(jnp.maximum(v[0], v[1]), jnp.maximum(v[2], v[3]))
    return jnp.where(v[0] == m, 0, jnp.where(v[1] == m, 1, jnp.where(v[2] == m, 2, 3)))


def _pick4(idx, v):
    return jnp.where(idx == 0, v[0], jnp.where(idx == 1, v[1], jnp.where(idx == 2, v[2], v[3])))


def _post_kernel(sb_ref, mla_ref, x_ref, mod_ref, wo_sb_ref, wo_mla_ref, g_ref, b_ref, rw_ref,
                 rb_ref, ut_ref, x1_ref, h2_ref, route_ref, gate_ref, cnt_ref):
    tm = x_ref.shape[0]
    mod = mod_ref[...]
    mix = _dot(sb_ref[...], wo_sb_ref[...]) + _dot(mla_ref[...], wo_mla_ref[...])
    x1 = _standardize(DEEPNORM_ALPHA * x_ref[...] + (1.0 + mod[2:3]) * mix) * g_ref[...] + b_ref[...]
    x1_ref[...] = x1
    h2 = _standardize(x1) * (1.0 + mod[4:5]) + mod[3:4]

    h_hi, h_lo = _split_bf16(h2)
    h2_ref[...] = h_hi
    rw = rw_ref[...]
    big = _dot(h_hi, rw)
    logits = big[:, :LANES] + big[:, LANES:] + _dot(h_lo, rw[:, :LANES])
    lt = logits.T[:N_EXPERTS]
    scores = jax.nn.sigmoid(lt)
    biased = scores + rb_ref[...]
    sc = [scores[N_GROUPS * p:N_GROUPS * (p + 1)] for p in range(EXPERTS_PER_GROUP)]
    bi = [biased[N_GROUPS * p:N_GROUPS * (p + 1)] for p in range(EXPERTS_PER_GROUP)]

    group_score = _top2_sum(*bi)
    gidx = lax.broadcasted_iota(jnp.int32, group_score.shape, 0)
    best = jnp.max(group_score, axis=0, keepdims=True)
    g_sel = jnp.min(jnp.where(group_score == best, gidx, N_GROUPS), axis=0, keepdims=True)
    in_sel = gidx == g_sel
    vb = [jnp.sum(jnp.where(in_sel, b, 0.0), axis=0, keepdims=True) for b in bi]
    vs = [jnp.sum(jnp.where(in_sel, s, 0.0), axis=0, keepdims=True) for s in sc]
    l1 = _first_argmax4(vb)
    vb2 = [jnp.where(l1 == p, -jnp.inf, vb[p]) for p in range(EXPERTS_PER_GROUP)]
    l2 = _first_argmax4(vb2)
    s1 = _pick4(l1, vs)
    s2 = _pick4(l2, vs)
    tot = s1 + s2
    e1 = g_sel * EXPERTS_PER_GROUP + l1
    e2 = g_sel * EXPERTS_PER_GROUP + l2

    eidx = lax.broadcasted_iota(jnp.int32, (N_EXPERTS, tm), 0)
    hit1 = eidx == e1
    hit2 = eidx == e2
    onehot = jnp.where(hit1 | hit2, 1.0, 0.0)
    before = _dot(onehot.astype(BF16), ut_ref[...])
    count = jnp.sum(onehot, axis=1, keepdims=True).astype(jnp.int32)
    granules = lax.shift_right_logical(count + (GRANULE - 1), GRANULE_SHIFT)
    lower = (lax.broadcasted_iota(jnp.int32, (N_EXPERTS, N_EXPERTS), 1)
             < lax.broadcasted_iota(jnp.int32, (N_EXPERTS, N_EXPERTS), 0))
    gran_f = jnp.broadcast_to(granules.astype(F32), (N_EXPERTS, LANES)).astype(BF16)
    start = _dot(jnp.where(lower, 1.0, 0.0).astype(BF16), gran_f)[:, 0:1] * GRANULE
    row_of = before + start
    p1 = jnp.sum(jnp.where(hit1, row_of, 0.0), axis=0, keepdims=True)
    p2 = jnp.sum(jnp.where(hit2, row_of, 0.0), axis=0, keepdims=True)
    cnt_ref[...] = jnp.broadcast_to(count, cnt_ref.shape)

    r8 = lax.broadcasted_iota(jnp.int32, (8, tm), 0)
    route_ref[...] = jnp.where(r8 == 0, p1.astype(jnp.int32), jnp.where(r8 == 1, p2.astype(jnp.int32), 0))
    r128 = lax.broadcasted_iota(jnp.int32, (LANES, tm), 0)
    cols = jnp.where(r128 == 0, s1 / tot, jnp.where(r128 == 1, s2 / tot,
                                                     jnp.where(r128 == 2, p1, jnp.where(r128 == 3, p2, 0.0))))
    gate_ref[...] = cols.T


def _post_call(sb_out, mla_out, x2d, mod, wo_sb, wo_mla, g, b, rw, rb, ut, seq):
    n, d = x2d.shape
    tm = TOKEN_TILE
    per_b = seq // tm
    tok = lambda i: (i, 0)
    full = lambda i: (0, 0)
    return pl.pallas_call(
        _post_kernel,
        grid=(n // tm,),
        in_specs=[pl.BlockSpec((tm, SB_WIDTH), tok),
                  pl.BlockSpec((tm, MLA_WIDTH), tok),
                  pl.BlockSpec((tm, d), tok),
                  pl.BlockSpec((None, 6, d), lambda i: (i // per_b, 0, 0)),
                  pl.BlockSpec(wo_sb.shape, full),
                  pl.BlockSpec(wo_mla.shape, full),
                  pl.BlockSpec((1, d), full),
                  pl.BlockSpec((1, d), full),
                  pl.BlockSpec(rw.shape, full),
                  pl.BlockSpec(rb.shape, full),
                  pl.BlockSpec(ut.shape, full)],
        out_specs=[pl.BlockSpec((tm, d), tok),
                   pl.BlockSpec((tm, d), tok),
                   pl.BlockSpec((8, tm), lambda i: (0, i)),
                   pl.BlockSpec((tm, LANES), tok),
                   pl.BlockSpec((None, N_EXPERTS, LANES), lambda i: (i, 0, 0))],
        out_shape=[jax.ShapeDtypeStruct((n, d), F32),
                   jax.ShapeDtypeStruct((n, d), BF16),
                   jax.ShapeDtypeStruct((8, n), jnp.int32),
                   jax.ShapeDtypeStruct((n, LANES), F32),
                   jax.ShapeDtypeStruct((n // tm, N_EXPERTS, LANES), jnp.int32)],
        compiler_params=pltpu.CompilerParams(vmem_limit_bytes=VMEM_LIMIT),
        name="post_attention",
    )(sb_out, mla_out, x2d, mod, wo_sb, wo_mla, g, b, rw, rb, ut)


def _granule_copy(src_ref, src_g, dst_ref, dst_g, sem, granules=1):
    rows = granules * GRANULE
    src = src_ref.at[pl.ds(pl.multiple_of(src_g * GRANULE, GRANULE), rows), :]
    dst = dst_ref.at[pl.ds(pl.multiple_of(dst_g * GRANULE, GRANULE), rows), :]
    return pltpu.make_async_copy(src, dst, sem)


def _start_segment(count, copy):
    def pair(p, carry):
        copy(2 * p, 2, 0).start()
        return carry

    lax.fori_loop(0, lax.shift_right_logical(count, 1), pair, 0)

    @pl.when((count & 1) == 1)
    def _():
        copy(count - 1, 1, 1).start()


def _wait_copies(count, copy):
    def wait(g, carry):
        copy().wait()
        return carry

    lax.fori_loop(0, count, wait, 0)


def _block_copy(zero_ref, buf_ref, block, sem):
    dst = buf_ref.at[pl.ds(pl.multiple_of(block * DISPATCH_BLOCK, DISPATCH_BLOCK), DISPATCH_BLOCK), :]
    return pltpu.make_async_copy(zero_ref, dst, sem)


def _dispatch_kernel(ng_ref, ls_ref, gs_ref, pairs_ref, singles_ref, tail_start_ref, tail_n_ref, n_used_ref,
                     route_ref, h_ref, buf_ref, slab_ref, zero_ref, sem):
    c = pl.program_id(0)
    last = pl.num_programs(0) - 1
    slot = c % 2
    tm = h_ref.shape[0]
    rows = slab_ref.shape[1]

    def wait_slab(tile, s):
        _wait_copies(pairs_ref[tile], lambda: _granule_copy(slab_ref.at[s], 0, buf_ref, 0, sem.at[2 * s], 2))
        _wait_copies(singles_ref[tile], lambda: _granule_copy(slab_ref.at[s], 0, buf_ref, 0, sem.at[2 * s + 1]))

    @pl.when(c == 0)
    def _():
        zero_ref[...] = jnp.zeros_like(zero_ref)

        def expert_tail(e, total):
            def granule(g, inner):
                _granule_copy(zero_ref, 0, buf_ref, tail_start_ref[e] + g, sem.at[4]).start()
                return inner

            lax.fori_loop(0, tail_n_ref[e], granule, 0)
            return total + tail_n_ref[e]

        n_tail = lax.fori_loop(0, N_EXPERTS, expert_tail, 0)
        n_blocks = buf_ref.shape[0] // DISPATCH_BLOCK

        def unused_block(b, carry):
            _block_copy(zero_ref, buf_ref, b, sem.at[5]).start()
            return carry

        lax.fori_loop(n_used_ref[0], n_blocks, unused_block, 0)
        _wait_copies(n_tail, lambda: _granule_copy(zero_ref, 0, buf_ref, 0, sem.at[4]))
        _wait_copies(n_blocks - n_used_ref[0], lambda: _block_copy(zero_ref, buf_ref, 0, sem.at[5]))

    @pl.when(c >= 2)
    def _():
        wait_slab(c - 2, slot)

    row = lax.broadcasted_iota(jnp.int32, (rows, tm), 0)
    route = route_ref[...]
    onehot = (row == route[0:1]) | (row == route[1:2])
    slab_ref[slot] = _dot(jnp.where(onehot, 1.0, 0.0).astype(BF16), h_ref[...])

    def expert(e, carry):
        idx = c * N_EXPERTS + e
        ls = ls_ref[idx]
        gs = gs_ref[idx]

        _start_segment(ng_ref[idx], lambda g, n, k: _granule_copy(
            slab_ref.at[slot], ls + g, buf_ref, gs + g, sem.at[2 * slot + k], n))
        return carry

    lax.fori_loop(0, N_EXPERTS, expert, 0)

    @pl.when(c == last)
    def _():
        wait_slab(c, slot)

        @pl.when(c >= 1)
        def _():
            wait_slab(c - 1, 1 - slot)


def _dispatch_call(tables, tails, n_used, route, h2, buf_rows):
    n, d = h2.shape
    tm = TOKEN_TILE
    grid_spec = pltpu.PrefetchScalarGridSpec(
        num_scalar_prefetch=8,
        grid=(n // tm,),
        in_specs=[pl.BlockSpec((8, tm), lambda i, *_: (0, i)),
                  pl.BlockSpec((tm, d), lambda i, *_: (i, 0))],
        out_specs=pl.BlockSpec(memory_space=pl.ANY),
        scratch_shapes=[pltpu.VMEM((2, SLAB_ROWS, d), F32), pltpu.VMEM((DISPATCH_BLOCK, d), F32),
                        pltpu.SemaphoreType.DMA((6,))],
    )
    return pl.pallas_call(
        _dispatch_kernel,
        grid_spec=grid_spec,
        out_shape=jax.ShapeDtypeStruct((buf_rows, d), F32),
        compiler_params=pltpu.CompilerParams(dimension_semantics=("arbitrary",),
                                             vmem_limit_bytes=VMEM_LIMIT),
        name="dispatch",
    )(*tables, *tails, n_used, route, h2)


def _ffn_kernel(be_ref, nb_ref, x_ref, wg_ref, wu_ref, wd_ref, y_ref, wg_s, wu_s, wd_s):
    i = pl.program_id(0)
    prev = be_ref[jnp.maximum(i - 1, 0)]

    @pl.when((i == 0) | (be_ref[i] != prev))
    def _():
        wg_s[...] = wg_ref[...].astype(BF16)
        wu_s[...] = wu_ref[...].astype(BF16)
        wd_s[...] = wd_ref[...].astype(BF16)

    @pl.when(i < nb_ref[0])
    def _():
        xb = x_ref[...].astype(BF16)
        g = _dot(xb, wg_s[...])
        u = _dot(xb, wu_s[...])
        a = g * jax.nn.sigmoid(g) * u
        y_ref[...] = _dot(a.astype(BF16), wd_s[...])

    @pl.when(i >= nb_ref[0])
    def _():
        y_ref[...] = jnp.zeros_like(y_ref)


def _ffn_call(block_e, n_used, buf, w_gate, w_up, w_down, layer):
    rows, d = buf.shape
    bm = DISPATCH_BLOCK
    de = w_gate.shape[-1]
    grid_spec = pltpu.PrefetchScalarGridSpec(
        num_scalar_prefetch=2,
        grid=(rows // bm,),
        in_specs=[pl.BlockSpec((bm, d), lambda i, be, nb: (jnp.minimum(i, nb[0] - 1), 0)),
                  pl.BlockSpec((None, None, d, de), lambda i, be, nb: (layer, be[i], 0, 0)),
                  pl.BlockSpec((None, None, d, de), lambda i, be, nb: (layer, be[i], 0, 0)),
                  pl.BlockSpec((None, None, de, d), lambda i, be, nb: (layer, be[i], 0, 0))],
        out_specs=pl.BlockSpec((bm, d), lambda i, be, nb: (i, 0)),
        scratch_shapes=[pltpu.VMEM((d, de), BF16), pltpu.VMEM((d, de), BF16), pltpu.VMEM((de, d), BF16)],
    )
    return pl.pallas_call(
        _ffn_kernel,
        grid_spec=grid_spec,
        out_shape=jax.ShapeDtypeStruct((rows, d), F32),
        compiler_params=pltpu.CompilerParams(dimension_semantics=("arbitrary",),
                                             vmem_limit_bytes=VMEM_LIMIT),
        name="expert_ffn",
    )(block_e, n_used, buf, w_gate, w_up, w_down)


def _combine_kernel(ng_ref, ls_ref, gs_ref, pairs_ref, singles_ref, x1_ref, gate_ref, mod_ref, g_ref, b_ref,
                    y_hbm_ref, o_ref, slab_ref, ffn_ref, sem):
    c = pl.program_id(0)
    last = pl.num_programs(0) - 1
    slot = c % 2
    tm = x1_ref.shape[0]
    rows = slab_ref.shape[1]

    def fetch(tile, into, first_expert, n_experts):
        def expert(e, carry):
            idx = tile * N_EXPERTS + e
            ls = ls_ref[idx]
            gs = gs_ref[idx]
            _start_segment(ng_ref[idx], lambda g, n, k: _granule_copy(
                y_hbm_ref, gs + g, slab_ref.at[into], ls + g, sem.at[2 * into + k], n))
            return carry

        lax.fori_loop(first_expert, first_expert + n_experts, expert, 0)

    def prefetch_next(part):
        @pl.when(c < last)
        def _():
            fetch(c + 1, 1 - slot, part * (N_EXPERTS // 4), N_EXPERTS // 4)

    @pl.when(c == 0)
    def _():
        slab_ref[...] = jnp.zeros_like(slab_ref)
        fetch(0, 0, 0, N_EXPERTS)

    _wait_copies(pairs_ref[c], lambda: _granule_copy(y_hbm_ref, 0, slab_ref.at[slot], 0, sem.at[2 * slot], 2))
    _wait_copies(singles_ref[c], lambda: _granule_copy(y_hbm_ref, 0, slab_ref.at[slot], 0, sem.at[2 * slot + 1]))

    prefetch_next(0)
    cols = gate_ref[...]
    lane = lax.broadcasted_iota(jnp.int32, (tm, rows), 1)
    ys = slab_ref[slot].astype(BF16)
    pick1 = jnp.where(lane == cols[:, 2:3].astype(jnp.int32), 1.0, 0.0).astype(BF16)
    ffn_ref[...] = cols[:, 0:1] * _dot(pick1, ys)
    prefetch_next(1)
    pick2 = jnp.where(lane == cols[:, 3:4].astype(jnp.int32), 1.0, 0.0).astype(BF16)
    ffn_ref[...] += cols[:, 1:2] * _dot(pick2, slab_ref[slot].astype(BF16))
    prefetch_next(2)
    mod = mod_ref[...]
    o_ref[...] = (_standardize(DEEPNORM_ALPHA * x1_ref[...] + (1.0 + mod[5:6]) * ffn_ref[...]) * g_ref[...]
                  + b_ref[...])
    prefetch_next(3)


def _combine_call(tables, x1, gates, mod, g, b, y_buf, seq):
    n, d = x1.shape
    tm = TOKEN_TILE
    per_b = seq // tm
    tok = lambda i, *_: (i, 0)
    full = lambda i, *_: (0, 0)
    grid_spec = pltpu.PrefetchScalarGridSpec(
        num_scalar_prefetch=5,
        grid=(n // tm,),
        in_specs=[pl.BlockSpec((tm, d), tok),
                  pl.BlockSpec((tm, LANES), tok),
                  pl.BlockSpec((None, 6, d), lambda i, *_: (i // per_b, 0, 0)),
                  pl.BlockSpec((1, d), full),
                  pl.BlockSpec((1, d), full),
                  pl.BlockSpec(memory_space=pl.ANY)],
        out_specs=pl.BlockSpec((tm, d), tok),
        scratch_shapes=[pltpu.VMEM((2, SLAB_ROWS, d), F32), pltpu.VMEM((tm, d), F32),
                        pltpu.SemaphoreType.DMA((4,))],
    )
    return pl.pallas_call(
        _combine_kernel,
        grid_spec=grid_spec,
        out_shape=jax.ShapeDtypeStruct((n, d), F32),
        compiler_params=pltpu.CompilerParams(dimension_semantics=("arbitrary",),
                                             vmem_limit_bytes=VMEM_LIMIT),
        name="combine",
    )(*tables, x1, gates, mod, g, b, y_buf)


def _layer_weights(w_in, w_uq, w_ukv, w_o):
    d = w_in.shape[0]
    w1 = w_in[:, :3 * SB_WIDTH].astype(BF16)
    lat = 3 * SB_WIDTH + MLA_Q_RANK + MLA_KV_RANK
    w2 = jnp.concatenate([w_in[:, 3 * SB_WIDTH:lat], jnp.zeros((d, ROPE_LANE0), F32), w_in[:, lat:],
                          jnp.zeros((d, LANES - ROPE_LANE0 - MLA_ROPE_DIM), F32)], axis=1).astype(BF16)
    uq = w_uq.reshape(MLA_Q_RANK, MLA_HEADS, MLA_NOPE_DIM + MLA_ROPE_DIM)
    uq = jnp.pad(uq, ((0, 0), (0, 0), (0, LANES - MLA_NOPE_DIM - MLA_ROPE_DIM)))
    wuq = uq.reshape(MLA_Q_RANK, MLA_HEADS * LANES).astype(BF16)
    ukv = w_ukv.reshape(MLA_KV_RANK, MLA_HEADS, MLA_NOPE_DIM + MLA_V_DIM)
    uk = jnp.pad(ukv[:, :, :MLA_NOPE_DIM], ((0, 0), (0, 0), (0, LANES - MLA_NOPE_DIM)))
    uv = ukv[:, :, MLA_NOPE_DIM:]
    even_head = (jnp.arange(MLA_HEADS) % 2 == 0)[None, :, None]
    uv = jnp.where(even_head, jnp.pad(uv, ((0, 0), (0, 0), (0, LANES - MLA_V_DIM))),
                   jnp.pad(uv, ((0, 0), (0, 0), (LANES - MLA_V_DIM, 0))))
    wukv = jnp.concatenate([uk.reshape(MLA_KV_RANK, MLA_HEADS * LANES),
                            uv.reshape(MLA_KV_RANK, MLA_HEADS * LANES)], axis=1).astype(BF16)
    return w1, w2, wuq, wukv, w_o[:SB_WIDTH].astype(BF16), w_o[SB_WIDTH:].astype(BF16)


def _router_weights(router_w, router_bias):
    d = router_w.shape[0]
    rw = router_w.reshape(d, N_GROUPS, EXPERTS_PER_GROUP).transpose(0, 2, 1).reshape(d, N_EXPERTS)
    hi = rw.astype(BF16)
    lo = (rw - hi.astype(F32)).astype(BF16)
    pad = jnp.zeros((d, LANES - N_EXPERTS), BF16)
    rwcat = jnp.concatenate([hi, pad, lo, pad], axis=1)
    rb = router_bias.reshape(N_GROUPS, EXPERTS_PER_GROUP).T.reshape(N_EXPERTS, 1)
    return rwcat, rb


def _dispatch_plan(counts, n_blocks):
    bm = DISPATCH_BLOCK
    gran = (counts + GRANULE - 1) // GRANULE
    slab_start = jnp.cumsum(gran, axis=1) - gran
    per_expert = jnp.sum(gran, axis=0) * GRANULE
    padded = (per_expert + bm - 1) // bm * bm
    pend = jnp.cumsum(padded)
    buf_start = (pend - padded) // GRANULE + jnp.cumsum(gran, axis=0) - gran
    tables = (gran.reshape(-1), slab_start.reshape(-1), buf_start.reshape(-1),
              jnp.sum(gran // 2, axis=1), jnp.sum(gran % 2, axis=1))
    tails = ((pend - padded + per_expert) // GRANULE, (padded - per_expert) // GRANULE)
    block_row = jnp.arange(n_blocks, dtype=jnp.int32) * bm
    block_e = jnp.minimum(jnp.sum(block_row[:, None] >= pend[None, :], axis=1), N_EXPERTS - 1)
    n_used = (pend[-1] // bm).reshape(1)
    as_i32 = lambda ts: tuple(t.astype(jnp.int32) for t in ts)
    return as_i32(tables), as_i32(tails), block_e.astype(jnp.int32), n_used.astype(jnp.int32)


def kernel(x, c, positions, ada_w, ada_b, w_in, q_norm, kv_norm, w_uq, w_ukv, w_o, ln1_g, ln1_b,
           router_w, router_bias, w_gate, w_up, w_down, ln2_g, ln2_b):
    batch, seq, d = x.shape
    n = batch * seq
    depth = ada_w.shape[0]
    assert seq % TOKEN_TILE == 0 and seq % ATT_TILE == 0

    mod_all = _ada_call(c, ada_w, ada_b).reshape(depth, batch, 6, d)
    cos_t, sin_t = _rope_call(positions)
    rwcat, rb = _router_weights(router_w, router_bias)
    t = ATT_TILE
    tri = (jnp.arange(t)[:, None] >= jnp.arange(t)[None, :]).astype(BF16)
    tri = jnp.concatenate([tri, tri], axis=0)
    tm = TOKEN_TILE
    ut = (jnp.arange(tm)[:, None] < jnp.arange(tm)[None, :]).astype(BF16)
    max_rows = n * TOP_K + (n // tm) * N_EXPERTS * (GRANULE - 1) + N_EXPERTS * (DISPATCH_BLOCK - 1)
    n_blocks = -(-max_rows // DISPATCH_BLOCK)

    x2d = x.reshape(n, d)
    for l in range(depth):
        mod = mod_all[l]
        w1, w2, wuq, wukv, wo_sb, wo_mla = _layer_weights(w_in[l], w_uq[l], w_ukv[l], w_o[l])
        sbq, sbk, sbv, mq, mk, mv = _proj_call(x2d, mod, cos_t, sin_t, w1, w2, q_norm[l].reshape(1, -1),
                                               kv_norm[l].reshape(1, -1), wuq, wukv, seq)
        sb_out = _sb_call(sbq, sbk, sbv, tri, batch, seq)
        mla_out = _mla_call(mq, mk, mv, batch, seq)
        x1, h2, route, gates, counts = _post_call(sb_out, mla_out, x2d, mod, wo_sb, wo_mla,
                                                  ln1_g[l].reshape(1, d), ln1_b[l].reshape(1, d),
                                                  rwcat, rb, ut, seq)
        tables, tails, block_e, n_used = _dispatch_plan(counts[:, :, 0], n_blocks)
        buf = _dispatch_call(tables, tails, n_used, route, h2, n_blocks * DISPATCH_BLOCK)
        y_buf = _ffn_call(block_e, n_used, buf, w_gate, w_up, w_down, l)
        x2d = _combine_call(tables, x1, gates, mod, ln2_g[l].reshape(1, d), ln2_b[l].reshape(1, d), y_buf, seq)
    return x2d.reshape(batch, seq, d)
```

```python
import jax
import jax.numpy as jnp
from jax import lax
from jax.experimental import pallas as pl
from jax.experimental.pallas import tpu as pltpu

F32 = jnp.float32
BF16 = jnp.bfloat16

D_MODEL = 1024
DEPTH = 2
SB_HEADS = 8
SB_HEAD_DIM = 64
SB_WIDTH = SB_HEADS * SB_HEAD_DIM
MLA_HEADS = 8
MLA_NOPE_DIM = 64
MLA_ROPE_DIM = 32
MLA_V_DIM = 64
MLA_Q_RANK = 256
MLA_KV_RANK = 128
MLA_WIDTH = MLA_HEADS * MLA_V_DIM
ROPE_BASE = 10000.0
N_EXPERTS = 32
N_GROUPS = 8
EXPERTS_PER_GROUP = 4
TOP_K = 2
D_EXPERT = 256
DISPATCH_BLOCK = 512
DEEPNORM_ALPHA = (2 * DEPTH) ** 0.25
LN_EPS = 1e-5
RMS_EPS = 1e-6
LOG2_E = 1.4426950408889634
UNDERFLOW_LOG2 = 160.0

LANES = 128
HALF_ROPE = MLA_ROPE_DIM // 2
ROPE_LANE0 = MLA_NOPE_DIM
VMEM_LIMIT = 56 * 1024 * 1024

TOKEN_TILE = 512
ATT_TILE = 256
GRANULE = 16
GRANULE_SHIFT = 4
TILE_GROUP = 4
MXU_DIM = 256
SLAB_ROWS = -(-(TOKEN_TILE * TOP_K + N_EXPERTS * (GRANULE - 1)) // MXU_DIM) * MXU_DIM


def _dot(a, b):
    return jnp.dot(a, b, preferred_element_type=F32)


def _dot_nt(a, b):
    return lax.dot_general(a, b, (((1,), (1,)), ((), ())), preferred_element_type=F32)


def _split_bf16(v):
    hi = v.astype(BF16)
    lo = (v - hi.astype(F32)).astype(BF16)
    return hi, lo


def _standardize(x):
    mu = jnp.mean(x, axis=-1, keepdims=True)
    xc = x - mu
    var = jnp.mean(xc * xc, axis=-1, keepdims=True)
    return xc * lax.rsqrt(var + LN_EPS)


def _ada_kernel(c_ref, w_ref, b_ref, o_ref):
    c = c_ref[...]
    ca = c * jax.nn.sigmoid(c)
    ca_hi, ca_lo = _split_bf16(ca)
    w_hi, w_lo = _split_bf16(w_ref[...])
    o_ref[...] = _dot(ca_hi, w_hi) + _dot(ca_lo, w_hi) + _dot(ca_hi, w_lo) + b_ref[...]


def _ada_call(c, ada_w, ada_b):
    depth, d, n6 = ada_w.shape
    b = c.shape[0]
    tn = 1536
    return pl.pallas_call(
        _ada_kernel,
        grid=(depth, n6 // tn),
        in_specs=[pl.BlockSpec((b, d), lambda l, j: (0, 0)),
                  pl.BlockSpec((None, d, tn), lambda l, j: (l, 0, j)),
                  pl.BlockSpec((None, 1, tn), lambda l, j: (l, 0, j))],
        out_specs=pl.BlockSpec((None, b, tn), lambda l, j: (l, 0, j)),
        out_shape=jax.ShapeDtypeStruct((depth, b, n6), F32),
        compiler_params=pltpu.CompilerParams(vmem_limit_bytes=VMEM_LIMIT),
        name="ada",
    )(c, ada_w, ada_b.reshape(depth, 1, n6))


def _rope_kernel(pos_ref, invf_ref, cos_ref, sin_ref):
    ang = pos_ref[...].astype(F32) * invf_ref[...]
    lane = lax.broadcasted_iota(jnp.int32, ang.shape, 1)
    rot = (lane >= ROPE_LANE0) & (lane < ROPE_LANE0 + MLA_ROPE_DIM)
    cos_ref[...] = jnp.where(rot, jnp.cos(ang), 1.0)
    sin_ref[...] = jnp.where(rot, jnp.sin(ang), 0.0)


def _rope_call(positions):
    n = positions.size
    tm = TOKEN_TILE
    inv_freq = ROPE_BASE ** (-jnp.arange(0, MLA_ROPE_DIM, 2, dtype=F32) / MLA_ROPE_DIM)
    invf = jnp.zeros((1, LANES), F32)
    invf = invf.at[0, ROPE_LANE0:ROPE_LANE0 + HALF_ROPE].set(inv_freq)
    invf = invf.at[0, ROPE_LANE0 + HALF_ROPE:ROPE_LANE0 + MLA_ROPE_DIM].set(inv_freq)
    return pl.pallas_call(
        _rope_kernel,
        grid=(n // tm,),
        in_specs=[pl.BlockSpec((tm, 1), lambda i: (i, 0)),
                  pl.BlockSpec((1, LANES), lambda i: (0, 0))],
        out_specs=[pl.BlockSpec((tm, LANES), lambda i: (i, 0)),
                   pl.BlockSpec((tm, LANES), lambda i: (i, 0))],
        out_shape=[jax.ShapeDtypeStruct((n, LANES), F32)] * 2,
        name="rope_tables",
    )(positions.reshape(n, 1), invf)


def _proj_kernel(x_ref, mod_ref, cos_ref, sin_ref, w1_ref, w2_ref, qn_ref, kvn_ref, wuq_ref,
                 wukv_ref, sbq_ref, sbk_ref, sbv_ref, mq_ref, mk_ref, mv_ref):
    mod = mod_ref[...]
    h = _standardize(x_ref[...]) * (1.0 + mod[1:2]) + mod[0:1]
    hb = h.astype(BF16)

    p1 = _dot(hb, w1_ref[...])
    sbq_ref[...] = (p1[:, :SB_WIDTH] * (SB_HEAD_DIM ** -0.5 * LOG2_E)).astype(BF16)
    sbk_ref[...] = p1[:, SB_WIDTH:2 * SB_WIDTH].astype(BF16)
    sbv_ref[...] = p1[:, 2 * SB_WIDTH:].astype(BF16)

    p2 = _dot(hb, w2_ref[...])
    q_lat = p2[:, :MLA_Q_RANK]
    kv_lat = p2[:, MLA_Q_RANK:MLA_Q_RANK + MLA_KV_RANK]
    k_rope = p2[:, MLA_Q_RANK + MLA_KV_RANK:]

    qn = q_lat * lax.rsqrt(jnp.mean(q_lat * q_lat, axis=-1, keepdims=True) + RMS_EPS) * qn_ref[...]
    kvn = kv_lat * lax.rsqrt(jnp.mean(kv_lat * kv_lat, axis=-1, keepdims=True) + RMS_EPS) * kvn_ref[...]
    q = _dot(qn.astype(BF16), wuq_ref[...])
    kv = _dot(kvn.astype(BF16), wukv_ref[...])

    cos = cos_ref[...]
    sin = sin_ref[...]
    lane = lax.broadcasted_iota(jnp.int32, cos.shape, 1)
    second = lane >= ROPE_LANE0 + HALF_ROPE
    sin_up = jnp.where(second, sin, 0.0)
    sin_dn = jnp.where(second, 0.0, -sin)

    def rope(t):
        return (t * cos + pltpu.roll(t, HALF_ROPE, 1) * sin_up
                + pltpu.roll(t, LANES - HALF_ROPE, 1) * sin_dn)

    kr = rope(k_rope)
    mla_scale = (MLA_NOPE_DIM + MLA_ROPE_DIM) ** -0.5 * LOG2_E
    for hd in range(MLA_HEADS):
        sl = slice(hd * LANES, (hd + 1) * LANES)
        mq_ref[:, sl] = (rope(q[:, sl]) * mla_scale).astype(BF16)
        mk_ref[:, sl] = (kv[:, sl] + kr).astype(BF16)
    vlane = lax.broadcasted_iota(jnp.int32, (1, MLA_HEADS * LANES), 1)
    quarter = lax.shift_right_logical(vlane, 6) & 3
    ones = jnp.where((quarter == 1) | (quarter == 2), 1.0, 0.0)
    mv_ref[...] = (kv[:, MLA_HEADS * LANES:] + ones).astype(BF16)


def _proj_call(x2d, mod, cos_t, sin_t, w1, w2, qn, kvn, wuq, wukv, seq):
    n, d = x2d.shape
    tm = TOKEN_TILE
    per_b = seq // tm
    tok = lambda i: (i, 0)
    full = lambda i: (0, 0)
    widths = (SB_WIDTH, SB_WIDTH, SB_WIDTH, MLA_HEADS * LANES, MLA_HEADS * LANES, MLA_HEADS * LANES)
    return pl.pallas_call(
        _proj_kernel,
        grid=(n // tm,),
        in_specs=[pl.BlockSpec((tm, d), tok),
                  pl.BlockSpec((None, 6, d), lambda i: (i // per_b, 0, 0)),
                  pl.BlockSpec((tm, LANES), tok),
                  pl.BlockSpec((tm, LANES), tok),
                  pl.BlockSpec(w1.shape, full),
                  pl.BlockSpec(w2.shape, full),
                  pl.BlockSpec(qn.shape, full),
                  pl.BlockSpec(kvn.shape, full),
                  pl.BlockSpec(wuq.shape, full),
                  pl.BlockSpec(wukv.shape, full)],
        out_specs=[pl.BlockSpec((tm, w), tok) for w in widths],
        out_shape=[jax.ShapeDtypeStruct((n, w), BF16) for w in widths],
        compiler_params=pltpu.CompilerParams(vmem_limit_bytes=VMEM_LIMIT),
        name="proj",
    )(x2d, mod, cos_t, sin_t, w1, w2, qn, kvn, wuq, wukv)


def _causal_tiles(i, step, state):
    first = i % TILE_GROUP + 1

    def first_step(count):
        tiles = tuple(i - d for d in range(count))
        return lambda s: step(tiles, s, (True,) + (False,) * (count - 1))

    state = lax.switch(first - 1, [first_step(c) for c in range(1, TILE_GROUP + 1)], state)
    top = i - first

    def group(p, s):
        j = top - TILE_GROUP * p
        return step(tuple(j - d for d in range(TILE_GROUP)), s, (False,) * TILE_GROUP)

    return lax.fori_loop(0, (i + 1 - first) // TILE_GROUP, group, state)


def _causal_tiles_until(i, step, state, live):
    state = lax.cond(i >= 1,
                     lambda s: step((i, i - 1), s, (True, False)),
                     lambda s: step((i,), s, (True,)), state)

    def more(carry):
        return (carry[0] >= 1) & live(carry[1])

    def pair(carry):
        j, s = carry
        return j - 2, step((j, j - 1), s, (False, False))

    j, state = lax.while_loop(more, pair, (i - 2, state))
    return lax.cond((j == 0) & live(state), lambda s: step((0,), s, (False,)), lambda s: s, state)


def _sb_kernel(q_ref, k_ref, v_ref, tri_ref, o_ref):
    t = ATT_TILE
    tri2 = tri_ref[...]
    lane = lax.broadcasted_iota(jnp.int32, (1, LANES), 1)
    row = lax.broadcasted_iota(jnp.int32, (t, t), 0)
    col = lax.broadcasted_iota(jnp.int32, (t, t), 1)
    strict = col < row
    mine = [lane < SB_HEAD_DIM, lane >= SB_HEAD_DIM]

    def step(qh, js, state, diagonal):
        chains = [(j, h, d) for j, d in zip(js, diagonal) for h in range(2)]
        ks = [k_ref[pl.ds(j * t, t), :] for j in js]
        zs = [_dot_nt(qh[h], ks[n]) for n in range(len(js)) for h in range(2)]
        nlks, pieces = [], []
        for (_, _, diag), z in zip(chains, zs):
            nlk = jnp.maximum(z, 0.0) + jnp.log(1.0 + jnp.exp2(-jnp.abs(z))) * LOG2_E
            if diag:
                nlk = jnp.where(strict, nlk, 0.0)
            nlks.append(nlk)
            pieces.append(jnp.concatenate(_split_bf16(nlk), axis=1))
        laters = [_dot(p, tri2) for p in pieces]
        ncarry = [state[h][0] for h in range(2)]
        ws = []
        for (j, h, diag), z, nlk, later in zip(chains, zs, nlks, laters):
            w = jnp.exp2((z - ncarry[h]) - later)
            if diag:
                w = jnp.where(strict, w, 0.0)
            ws.append(w.astype(BF16))
            ncarry[h] = ncarry[h] + jnp.sum(nlk, axis=1, keepdims=True)
        acc = [state[h][1] for h in range(2)]
        for (j, h, _), w in zip(chains, ws):
            acc[h] = acc[h] + _dot(w, v_ref[pl.ds(j * t, t), :])
        return tuple((ncarry[h], acc[h]) for h in range(2))

    def live(state):
        return jnp.minimum(jnp.min(state[0][0]), jnp.min(state[1][0])) < UNDERFLOW_LOG2

    def q_tile(i, carry):
        rows = pl.ds(pl.multiple_of(i * t, t), t)
        q = q_ref[rows, :]
        qh = [jnp.where(m, q, jnp.zeros_like(q)) for m in mine]
        init = tuple((jnp.zeros((t, 1), F32), jnp.zeros((t, LANES), F32)) for _ in range(2))
        state = _causal_tiles_until(i, lambda js, s, d: step(qh, js, s, d), init, live)
        o_ref[rows, :] = jnp.where(mine[0], state[0][1], state[1][1]).astype(o_ref.dtype)
        return carry

    lax.fori_loop(0, q_ref.shape[0] // t, q_tile, 0)


def _sb_call(q, k, v, tri, batch, seq):
    t = ATT_TILE
    pairs = SB_WIDTH // LANES
    seq_spec = pl.BlockSpec((seq, LANES), lambda b, p: (b, p))
    return pl.pallas_call(
        _sb_kernel,
        grid=(batch, pairs),
        in_specs=[seq_spec, seq_spec, seq_spec,
                  pl.BlockSpec((2 * t, t), lambda b, p: (0, 0))],
        out_specs=seq_spec,
        out_shape=jax.ShapeDtypeStruct(q.shape, BF16),
        compiler_params=pltpu.CompilerParams(vmem_limit_bytes=VMEM_LIMIT),
        name="sb_attention",
    )(q, k, v, tri)


def _mla_kernel(q_ref, k_ref, v_ref, o_ref):
    t = ATT_TILE
    lane = lax.broadcasted_iota(jnp.int32, (1, LANES), 1)
    row = lax.broadcasted_iota(jnp.int32, (t, t), 0)
    col = lax.broadcasted_iota(jnp.int32, (t, t), 1)
    causal = col <= row

    def step(qh, js, state, diagonal):
        ss = []
        for hd in range(2):
            for j, diag in zip(js, diagonal):
                s = _dot_nt(qh[hd], k_ref[pl.ds(j * t, t), hd * LANES:(hd + 1) * LANES])
                ss.append(jnp.where(causal, s, -jnp.inf) if diag else s)
        n = len(js)
        ps, new_state = [], []
        for hd in range(2):
            m, acc = state[hd]
            mine_s = ss[hd * n:(hd + 1) * n]
            m_new = m
            for s in mine_s:
                m_new = jnp.maximum(m_new, jnp.max(s, axis=1, keepdims=True))
            for s in mine_s:
                ps.append(jnp.exp2(s - m_new).astype(BF16))
            new_state.append([m_new, jnp.exp2(m - m_new) * acc])
        for hd in range(2):
            for idx, j in enumerate(js):
                v = v_ref[pl.ds(j * t, t), hd * LANES:(hd + 1) * LANES]
                new_state[hd][1] = new_state[hd][1] + _dot(ps[hd * n + idx], v)
        return tuple(tuple(st) for st in new_state)

    def q_tile(i, carry):
        rows = pl.ds(pl.multiple_of(i * t, t), t)
        qh = [q_ref[rows, hd * LANES:(hd + 1) * LANES] for hd in range(2)]
        init = tuple((jnp.full((t, 1), -jnp.inf, F32), jnp.zeros((t, LANES), F32)) for _ in range(2))
        state = _causal_tiles(i, lambda js, s, d: step(qh, js, s, d), init)
        acc0, acc1 = state[0][1], state[1][1]
        out = jnp.where(lane < MLA_V_DIM, acc0 / pltpu.roll(acc0, MLA_V_DIM, 1), acc1 / pltpu.roll(acc1, MLA_V_DIM, 1))
        o_ref[rows, :] = out.astype(o_ref.dtype)
        return carry

    lax.fori_loop(0, q_ref.shape[0] // t, q_tile, 0)


def _mla_call(q, k, v, batch, seq):
    pairs = MLA_WIDTH // LANES
    return pl.pallas_call(
        _mla_kernel,
        grid=(batch, pairs),
        in_specs=[pl.BlockSpec((seq, 2 * LANES), lambda b, p: (b, p)),
                  pl.BlockSpec((seq, 2 * LANES), lambda b, p: (b, p)),
                  pl.BlockSpec((seq, 2 * LANES), lambda b, p: (b, p))],
        out_specs=pl.BlockSpec((seq, LANES), lambda b, p: (b, p)),
        out_shape=jax.ShapeDtypeStruct((q.shape[0], MLA_WIDTH), BF16),
        compiler_params=pltpu.CompilerParams(vmem_limit_bytes=VMEM_LIMIT),
        name="mla_attention",
    )(q, k, v)


def _top2_sum(a, b, c, d):
    hi1, lo1 = jnp.maximum(a, b), jnp.minimum(a, b)
    hi2, lo2 = jnp.maximum(c, d), jnp.minimum(c, d)
    return jnp.maximum(hi1, hi2) + jnp.maximum(jnp.minimum(hi1, hi2), jnp.maximum(lo1, lo2))


def _first_argmax4(v):
    m = jnp.maximum(jnp.maximum(v[0], v[1]), jnp.maximum(v[2], v[3]))
    return jnp.where(v[0] == m, 0, jnp.where(v[1] == m, 1, jnp.where(v[2] == m, 2, 3)))


def _pick4(idx, v):
    return jnp.where(idx == 0, v[0], jnp.where(idx == 1, v[1], jnp.where(idx == 2, v[2], v[3])))


def _post_kernel(sb_ref, mla_ref, x_ref, mod_ref, wo_sb_ref, wo_mla_ref, g_ref, b_ref, rw_ref,
                 rb_ref, ut_ref, x1_ref, h2_ref, route_ref, gate_ref, cnt_ref):
    tm = x_ref.shape[0]
    mod = mod_ref[...]
    mix = _dot(sb_ref[...], wo_sb_ref[...]) + _dot(mla_ref[...], wo_mla_ref[...])
    x1 = _standardize(DEEPNORM_ALPHA * x_ref[...] + (1.0 + mod[2:3]) * mix) * g_ref[...] + b_ref[...]
    x1_ref[...] = x1
    h2 = _standardize(x1) * (1.0 + mod[4:5]) + mod[3:4]

    h_hi, h_lo = _split_bf16(h2)
    h2_ref[...] = h_hi
    rw = rw_ref[...]
    big = _dot(h_hi, rw)
    logits = big[:, :LANES] + big[:, LANES:] + _dot(h_lo, rw[:, :LANES])
    lt = logits.T[:N_EXPERTS]
    scores = jax.nn.sigmoid(lt)
    biased = scores + rb_ref[...]
    sc = [scores[N_GROUPS * p:N_GROUPS * (p + 1)] for p in range(EXPERTS_PER_GROUP)]
    bi = [biased[N_GROUPS * p:N_GROUPS * (p + 1)] for p in range(EXPERTS_PER_GROUP)]

    group_score = _top2_sum(*bi)
    gidx = lax.broadcasted_iota(jnp.int32, group_score.shape, 0)
    best = jnp.max(group_score, axis=0, keepdims=True)
    g_sel = jnp.min(jnp.where(group_score == best, gidx, N_GROUPS), axis=0, keepdims=True)
    in_sel = gidx == g_sel
    vb = [jnp.sum(jnp.where(in_sel, b, 0.0), axis=0, keepdims=True) for b in bi]
    vs = [jnp.sum(jnp.where(in_sel, s, 0.0), axis=0, keepdims=True) for s in sc]
    l1 = _first_argmax4(vb)
    vb2 = [jnp.where(l1 == p, -jnp.inf, vb[p]) for p in range(EXPERTS_PER_GROUP)]
    l2 = _first_argmax4(vb2)
    s1 = _pick4(l1, vs)
    s2 = _pick4(l2, vs)
    tot = s1 + s2
    e1 = g_sel * EXPERTS_PER_GROUP + l1
    e2 = g_sel * EXPERTS_PER_GROUP + l2

    eidx = lax.broadcasted_iota(jnp.int32, (N_EXPERTS, tm), 0)
    hit1 = eidx == e1
    hit2 = eidx == e2
    onehot = jnp.where(hit1 | hit2, 1.0, 0.0)
    before = _dot(onehot.astype(BF16), ut_ref[...])
    count = jnp.sum(onehot, axis=1, keepdims=True).astype(jnp.int32)
    granules = lax.shift_right_logical(count + (GRANULE - 1), GRANULE_SHIFT)
    lower = (lax.broadcasted_iota(jnp.int32, (N_EXPERTS, N_EXPERTS), 1)
             < lax.broadcasted_iota(jnp.int32, (N_EXPERTS, N_EXPERTS), 0))
    gran_f = jnp.broadcast_to(granules.astype(F32), (N_EXPERTS, LANES)).astype(BF16)
    start = _dot(jnp.where(lower, 1.0, 0.0).astype(BF16), gran_f)[:, 0:1] * GRANULE
    row_of = before + start
    p1 = jnp.sum(jnp.where(hit1, row_of, 0.0), axis=0, keepdims=True)
    p2 = jnp.sum(jnp.where(hit2, row_of, 0.0), axis=0, keepdims=True)
    cnt_ref[...] = jnp.broadcast_to(count, cnt_ref.shape)

    r8 = lax.broadcasted_iota(jnp.int32, (8, tm), 0)
    route_ref[...] = jnp.where(r8 == 0, p1.astype(jnp.int32), jnp.where(r8 == 1, p2.astype(jnp.int32), 0))
    r128 = lax.broadcasted_iota(jnp.int32, (LANES, tm), 0)
    cols = jnp.where(r128 == 0, s1 / tot, jnp.where(r128 == 1, s2 / tot,
                                                     jnp.where(r128 == 2, p1, jnp.where(r128 == 3, p2, 0.0))))
    gate_ref[...] = cols.T


def _post_call(sb_out, mla_out, x2d, mod, wo_sb, wo_mla, g, b, rw, rb, ut, seq):
    n, d = x2d.shape
    tm = TOKEN_TILE
    per_b = seq // tm
    tok = lambda i: (i, 0)
    full = lambda i: (0, 0)
    return pl.pallas_call(
        _post_kernel,
        grid=(n // tm,),
        in_specs=[pl.BlockSpec((tm, SB_WIDTH), tok),
                  pl.BlockSpec((tm, MLA_WIDTH), tok),
                  pl.BlockSpec((tm, d), tok),
                  pl.BlockSpec((None, 6, d), lambda i: (i // per_b, 0, 0)),
                  pl.BlockSpec(wo_sb.shape, full),
                  pl.BlockSpec(wo_mla.shape, full),
                  pl.BlockSpec((1, d), full),
                  pl.BlockSpec((1, d), full),
                  pl.BlockSpec(rw.shape, full),
                  pl.BlockSpec(rb.shape, full),
                  pl.BlockSpec(ut.shape, full)],
        out_specs=[pl.BlockSpec((tm, d), tok),
                   pl.BlockSpec((tm, d), tok),
                   pl.BlockSpec((8, tm), lambda i: (0, i)),
                   pl.BlockSpec((tm, LANES), tok),
                   pl.BlockSpec((None, N_EXPERTS, LANES), lambda i: (i, 0, 0))],
        out_shape=[jax.ShapeDtypeStruct((n, d), F32),
                   jax.ShapeDtypeStruct((n, d), BF16),
                   jax.ShapeDtypeStruct((8, n), jnp.int32),
                   jax.ShapeDtypeStruct((n, LANES), F32),
                   jax.ShapeDtypeStruct((n // tm, N_EXPERTS, LANES), jnp.int32)],
        compiler_params=pltpu.CompilerParams(vmem_limit_bytes=VMEM_LIMIT),
        name="post_attention",
    )(sb_out, mla_out, x2d, mod, wo_sb, wo_mla, g, b, rw, rb, ut)


def _granule_copy(src_ref, src_g, dst_ref, dst_g, sem, granules=1):
    rows = granules * GRANULE
    src = src_ref.at[pl.ds(pl.multiple_of(src_g * GRANULE, GRANULE), rows), :]
    dst = dst_ref.at[pl.ds(pl.multiple_of(dst_g * GRANULE, GRANULE), rows), :]
    return pltpu.make_async_copy(src, dst, sem)


def _start_segment(count, copy):
    def pair(p, carry):
        copy(2 * p, 2, 0).start()
        return carry

    lax.fori_loop(0, lax.shift_right_logical(count, 1), pair, 0)

    @pl.when((count & 1) == 1)
    def _():
        copy(count - 1, 1, 1).start()


def _wait_copies(count, copy):
    def wait(g, carry):
        copy().wait()
        return carry

    lax.fori_loop(0, count, wait, 0)


def _block_copy(zero_ref, buf_ref, block, sem):
    dst = buf_ref.at[pl.ds(pl.multiple_of(block * DISPATCH_BLOCK, DISPATCH_BLOCK), DISPATCH_BLOCK), :]
    return pltpu.make_async_copy(zero_ref, dst, sem)


def _dispatch_kernel(ng_ref, ls_ref, gs_ref, pairs_ref, singles_ref, tail_start_ref, tail_n_ref, n_used_ref,
                     route_ref, h_ref, buf_ref, slab_ref, zero_ref, sem):
    c = pl.program_id(0)
    last = pl.num_programs(0) - 1
    slot = c % 2
    tm = h_ref.shape[0]
    rows = slab_ref.shape[1]

    def wait_slab(tile, s):
        _wait_copies(pairs_ref[tile], lambda: _granule_copy(slab_ref.at[s], 0, buf_ref, 0, sem.at[2 * s], 2))
        _wait_copies(singles_ref[tile], lambda: _granule_copy(slab_ref.at[s], 0, buf_ref, 0, sem.at[2 * s + 1]))

    @pl.when(c == 0)
    def _():
        zero_ref[...] = jnp.zeros_like(zero_ref)

        def expert_tail(e, total):
            def granule(g, inner):
                _granule_copy(zero_ref, 0, buf_ref, tail_start_ref[e] + g, sem.at[4]).start()
                return inner

            lax.fori_loop(0, tail_n_ref[e], granule, 0)
            return total + tail_n_ref[e]

        n_tail = lax.fori_loop(0, N_EXPERTS, expert_tail, 0)
        n_blocks = buf_ref.shape[0] // DISPATCH_BLOCK

        def unused_block(b, carry):
            _block_copy(zero_ref, buf_ref, b, sem.at[5]).start()
            return carry

        lax.fori_loop(n_used_ref[0], n_blocks, unused_block, 0)
        _wait_copies(n_tail, lambda: _granule_copy(zero_ref, 0, buf_ref, 0, sem.at[4]))
        _wait_copies(n_blocks - n_used_ref[0], lambda: _block_copy(zero_ref, buf_ref, 0, sem.at[5]))

    @pl.when(c >= 2)
    def _():
        wait_slab(c - 2, slot)

    row = lax.broadcasted_iota(jnp.int32, (rows, tm), 0)
    route = route_ref[...]
    onehot = (row == route[0:1]) | (row == route[1:2])
    slab_ref[slot] = _dot(jnp.where(onehot, 1.0, 0.0).astype(BF16), h_ref[...]).astype(BF16)

    def expert(e, carry):
        idx = c * N_EXPERTS + e
        ls = ls_ref[idx]
        gs = gs_ref[idx]

        _start_segment(ng_ref[idx], lambda g, n, k: _granule_copy(
            slab_ref.at[slot], ls + g, buf_ref, gs + g, sem.at[2 * slot + k], n))
        return carry

    lax.fori_loop(0, N_EXPERTS, expert, 0)

    @pl.when(c == last)
    def _():
        wait_slab(c, slot)

        @pl.when(c >= 1)
        def _():
            wait_slab(c - 1, 1 - slot)


def _dispatch_call(tables, tails, n_used, route, h2, buf_rows):
    n, d = h2.shape
    tm = TOKEN_TILE
    grid_spec = pltpu.PrefetchScalarGridSpec(
        num_scalar_prefetch=8,
        grid=(n // tm,),
        in_specs=[pl.BlockSpec((8, tm), lambda i, *_: (0, i)),
                  pl.BlockSpec((tm, d), lambda i, *_: (i, 0))],
        out_specs=pl.BlockSpec(memory_space=pl.ANY),
        scratch_shapes=[pltpu.VMEM((2, SLAB_ROWS, d), BF16), pltpu.VMEM((DISPATCH_BLOCK, d), BF16),
                        pltpu.SemaphoreType.DMA((6,))],
    )
    return pl.pallas_call(
        _dispatch_kernel,
        grid_spec=grid_spec,
        out_shape=jax.ShapeDtypeStruct((buf_rows, d), BF16),
        compiler_params=pltpu.CompilerParams(dimension_semantics=("arbitrary",),
                                             vmem_limit_bytes=VMEM_LIMIT),
        name="dispatch",
    )(*tables, *tails, n_used, route, h2)


def _ffn_kernel(be_ref, nb_ref, x_ref, wg_ref, wu_ref, wd_ref, y_ref, wg_s, wu_s, wd_s):
    i = pl.program_id(0)
    prev = be_ref[jnp.maximum(i - 1, 0)]

    @pl.when((i == 0) | (be_ref[i] != prev))
    def _():
        wg_s[...] = wg_ref[...].astype(BF16)
        wu_s[...] = wu_ref[...].astype(BF16)
        wd_s[...] = wd_ref[...].astype(BF16)

    @pl.when(i < nb_ref[0])
    def _():
        xb = x_ref[...]
        g = _dot(xb, wg_s[...])
        u = _dot(xb, wu_s[...])
        a = g * jax.nn.sigmoid(g) * u
        y_ref[...] = _dot(a.astype(BF16), wd_s[...]).astype(y_ref.dtype)

    @pl.when(i >= nb_ref[0])
    def _():
        y_ref[...] = jnp.zeros_like(y_ref)


def _ffn_call(block_e, n_used, buf, w_gate, w_up, w_down, layer):
    rows, d = buf.shape
    bm = DISPATCH_BLOCK
    de = w_gate.shape[-1]
    grid_spec = pltpu.PrefetchScalarGridSpec(
        num_scalar_prefetch=2,
        grid=(rows // bm,),
        in_specs=[pl.BlockSpec((bm, d), lambda i, be, nb: (jnp.minimum(i, nb[0] - 1), 0)),
                  pl.BlockSpec((None, None, d, de), lambda i, be, nb: (layer, be[i], 0, 0)),
                  pl.BlockSpec((None, None, d, de), lambda i, be, nb: (layer, be[i], 0, 0)),
                  pl.BlockSpec((None, None, de, d), lambda i, be, nb: (layer, be[i], 0, 0))],
        out_specs=pl.BlockSpec((bm, d), lambda i, be, nb: (i, 0)),
        scratch_shapes=[pltpu.VMEM((d, de), BF16), pltpu.VMEM((d, de), BF16), pltpu.VMEM((de, d), BF16)],
    )
    return pl.pallas_call(
        _ffn_kernel,
        grid_spec=grid_spec,
        out_shape=jax.ShapeDtypeStruct((rows, d), BF16),
        compiler_params=pltpu.CompilerParams(dimension_semantics=("arbitrary",),
                                             vmem_limit_bytes=VMEM_LIMIT),
        name="expert_ffn",
    )(block_e, n_used, buf, w_gate, w_up, w_down)


def _combine_kernel(ng_ref, ls_ref, gs_ref, pairs_ref, singles_ref, x1_ref, gate_ref, mod_ref, g_ref, b_ref,
                    y_hbm_ref, o_ref, slab_ref, ffn_ref, sem):
    c = pl.program_id(0)
    last = pl.num_programs(0) - 1
    slot = c % 2
    tm = x1_ref.shape[0]
    rows = slab_ref.shape[1]

    def fetch(tile, into, first_expert, n_experts):
        def expert(e, carry):
            idx = tile * N_EXPERTS + e
            ls = ls_ref[idx]
            gs = gs_ref[idx]
            _start_segment(ng_ref[idx], lambda g, n, k: _granule_copy(
                y_hbm_ref, gs + g, slab_ref.at[into], ls + g, sem.at[2 * into + k], n))
            return carry

        lax.fori_loop(first_expert, first_expert + n_experts, expert, 0)

    def prefetch_next(part):
        @pl.when(c < last)
        def _():
            fetch(c + 1, 1 - slot, part * (N_EXPERTS // 4), N_EXPERTS // 4)

    @pl.when(c == 0)
    def _():
        slab_ref[...] = jnp.zeros_like(slab_ref)
        fetch(0, 0, 0, N_EXPERTS)

    _wait_copies(pairs_ref[c], lambda: _granule_copy(y_hbm_ref, 0, slab_ref.at[slot], 0, sem.at[2 * slot], 2))
    _wait_copies(singles_ref[c], lambda: _granule_copy(y_hbm_ref, 0, slab_ref.at[slot], 0, sem.at[2 * slot + 1]))

    prefetch_next(0)
    cols = gate_ref[...]
    lane = lax.broadcasted_iota(jnp.int32, (tm, rows), 1)
    ys = slab_ref[slot]
    pick1 = jnp.where(lane == cols[:, 2:3].astype(jnp.int32), 1.0, 0.0).astype(BF16)
    ffn_ref[...] = cols[:, 0:1] * _dot(pick1, ys)
    prefetch_next(1)
    pick2 = jnp.where(lane == cols[:, 3:4].astype(jnp.int32), 1.0, 0.0).astype(BF16)
    ffn_ref[...] += cols[:, 1:2] * _dot(pick2, slab_ref[slot])
    prefetch_next(2)
    mod = mod_ref[...]
    o_ref[...] = (_standardize(DEEPNORM_ALPHA * x1_ref[...] + (1.0 + mod[5:6]) * ffn_ref[...]) * g_ref[...]
                  + b_ref[...])
    prefetch_next(3)


def _combine_call(tables, x1, gates, mod, g, b, y_buf, seq):
    n, d = x1.shape
    tm = TOKEN_TILE
    per_b = seq // tm
    tok = lambda i, *_: (i, 0)
    full = lambda i, *_: (0, 0)
    grid_spec = pltpu.PrefetchScalarGridSpec(
        num_scalar_prefetch=5,
        grid=(n // tm,),
        in_specs=[pl.BlockSpec((tm, d), tok),
                  pl.BlockSpec((tm, LANES), tok),
                  pl.BlockSpec((None, 6, d), lambda i, *_: (i // per_b, 0, 0)),
                  pl.BlockSpec((1, d), full),
                  pl.BlockSpec((1, d), full),
                  pl.BlockSpec(memory_space=pl.ANY)],
        out_specs=pl.BlockSpec((tm, d), tok),
        scratch_shapes=[pltpu.VMEM((2, SLAB_ROWS, d), BF16), pltpu.VMEM((tm, d), F32),
                        pltpu.SemaphoreType.DMA((4,))],
    )
    return pl.pallas_call(
        _combine_kernel,
        grid_spec=grid_spec,
        out_shape=jax.ShapeDtypeStruct((n, d), F32),
        compiler_params=pltpu.CompilerParams(dimension_semantics=("arbitrary",),
                                             vmem_limit_bytes=VMEM_LIMIT),
        name="combine",
    )(*tables, x1, gates, mod, g, b, y_buf)


def _layer_weights(w_in, w_uq, w_ukv, w_o):
    d = w_in.shape[0]
    w1 = w_in[:, :3 * SB_WIDTH].astype(BF16)
    lat = 3 * SB_WIDTH + MLA_Q_RANK + MLA_KV_RANK
    w2 = jnp.concatenate([w_in[:, 3 * SB_WIDTH:lat], jnp.zeros((d, ROPE_LANE0), F32), w_in[:, lat:],
                          jnp.zeros((d, LANES - ROPE_LANE0 - MLA_ROPE_DIM), F32)], axis=1).astype(BF16)
    uq = w_uq.reshape(MLA_Q_RANK, MLA_HEADS, MLA_NOPE_DIM + MLA_ROPE_DIM)
    uq = jnp.pad(uq, ((0, 0), (0, 0), (0, LANES - MLA_NOPE_DIM - MLA_ROPE_DIM)))
    wuq = uq.reshape(MLA_Q_RANK, MLA_HEADS * LANES).astype(BF16)
    ukv = w_ukv.reshape(MLA_KV_RANK, MLA_HEADS, MLA_NOPE_DIM + MLA_V_DIM)
    uk = jnp.pad(ukv[:, :, :MLA_NOPE_DIM], ((0, 0), (0, 0), (0, LANES - MLA_NOPE_DIM)))
    uv = ukv[:, :, MLA_NOPE_DIM:]
    even_head = (jnp.arange(MLA_HEADS) % 2 == 0)[None, :, None]
    uv = jnp.where(even_head, jnp.pad(uv, ((0, 0), (0, 0), (0, LANES - MLA_V_DIM))),
                   jnp.pad(uv, ((0, 0), (0, 0), (LANES - MLA_V_DIM, 0))))
    wukv = jnp.concatenate([uk.reshape(MLA_KV_RANK, MLA_HEADS * LANES),
                            uv.reshape(MLA_KV_RANK, MLA_HEADS * LANES)], axis=1).astype(BF16)
    return w1, w2, wuq, wukv, w_o[:SB_WIDTH].astype(BF16), w_o[SB_WIDTH:].astype(BF16)


def _router_weights(router_w, router_bias):
    d = router_w.shape[0]
    rw = router_w.reshape(d, N_GROUPS, EXPERTS_PER_GROUP).transpose(0, 2, 1).reshape(d, N_EXPERTS)
    hi = rw.astype(BF16)
    lo = (rw - hi.astype(F32)).astype(BF16)
    pad = jnp.zeros((d, LANES - N_EXPERTS), BF16)
    rwcat = jnp.concatenate([hi, pad, lo, pad], axis=1)
    rb = router_bias.reshape(N_GROUPS, EXPERTS_PER_GROUP).T.reshape(N_EXPERTS, 1)
    return rwcat, rb


def _dispatch_plan(counts, n_blocks):
    bm = DISPATCH_BLOCK
    gran = (counts + GRANULE - 1) // GRANULE
    slab_start = jnp.cumsum(gran, axis=1) - gran
    per_expert = jnp.sum(gran, axis=0) * GRANULE
    padded = (per_expert + bm - 1) // bm * bm
    pend = jnp.cumsum(padded)
    buf_start = (pend - padded) // GRANULE + jnp.cumsum(gran, axis=0) - gran
    tables = (gran.reshape(-1), slab_start.reshape(-1), buf_start.reshape(-1),
              jnp.sum(gran // 2, axis=1), jnp.sum(gran % 2, axis=1))
    tails = ((pend - padded + per_expert) // GRANULE, (padded - per_expert) // GRANULE)
    block_row = jnp.arange(n_blocks, dtype=jnp.int32) * bm
    block_e = jnp.minimum(jnp.sum(block_row[:, None] >= pend[None, :], axis=1), N_EXPERTS - 1)
    n_used = (pend[-1] // bm).reshape(1)
    as_i32 = lambda ts: tuple(t.astype(jnp.int32) for t in ts)
    return as_i32(tables), as_i32(tails), block_e.astype(jnp.int32), n_used.astype(jnp.int32)


def kernel(x, c, positions, ada_w, ada_b, w_in, q_norm, kv_norm, w_uq, w_ukv, w_o, ln1_g, ln1_b,
           router_w, router_bias, w_gate, w_up, w_down, ln2_g, ln2_b):
    batch, seq, d = x.shape
    n = batch * seq
    depth = ada_w.shape[0]
    assert seq % TOKEN_TILE == 0 and seq % ATT_TILE == 0

    mod_all = _ada_call(c, ada_w, ada_b).reshape(depth, batch, 6, d)
    cos_t, sin_t = _rope_call(positions)
    rwcat, rb = _router_weights(router_w, router_bias)
    t = ATT_TILE
    tri = (jnp.arange(t)[:, None] >= jnp.arange(t)[None, :]).astype(BF16)
    tri = jnp.concatenate([tri, tri], axis=0)
    tm = TOKEN_TILE
    ut = (jnp.arange(tm)[:, None] < jnp.arange(tm)[None, :]).astype(BF16)
    max_rows = n * TOP_K + (n // tm) * N_EXPERTS * (GRANULE - 1) + N_EXPERTS * (DISPATCH_BLOCK - 1)
    n_blocks = -(-max_rows // DISPATCH_BLOCK)

    x2d = x.reshape(n, d)
    for l in range(depth):
        mod = mod_all[l]
        w1, w2, wuq, wukv, wo_sb, wo_mla = _layer_weights(w_in[l], w_uq[l], w_ukv[l], w_o[l])
        sbq, sbk, sbv, mq, mk, mv = _proj_call(x2d, mod, cos_t, sin_t, w1, w2, q_norm[l].reshape(1, -1),
                                               kv_norm[l].reshape(1, -1), wuq, wukv, seq)
        sb_out = _sb_call(sbq, sbk, sbv, tri, batch, seq)
        mla_out = _mla_call(mq, mk, mv, batch, seq)
        x1, h2, route, gates, counts = _post_call(sb_out, mla_out, x2d, mod, wo_sb, wo_mla,
                                                  ln1_g[l].reshape(1, d), ln1_b[l].reshape(1, d),
                                                  rwcat, rb, ut, seq)
        tables, tails, block_e, n_used = _dispatch_plan(counts[:, :, 0], n_blocks)
        buf = _dispatch_call(tables, tails, n_used, route, h2, n_blocks * DISPATCH_BLOCK)
        y_buf = _ffn_call(block_e, n_used, buf, w_gate, w_up, w_down, l)
        x2d = _combine_call(tables, x1, gates, mod, ln2_g[l].reshape(1, d), ln2_b[l].reshape(1, d), y_buf, seq)
    return x2d.reshape(batch, seq, d)
```

```python
import jax
import jax.numpy as jnp
from jax import lax
from jax.experimental import pallas as pl
from jax.experimental.pallas import tpu as pltpu

F32 = jnp.float32
BF16 = jnp.bfloat16

D_MODEL = 1024
DEPTH = 2
SB_HEADS = 8
SB_HEAD_DIM = 64
SB_WIDTH = SB_HEADS * SB_HEAD_DIM
MLA_HEADS = 8
MLA_NOPE_DIM = 64
MLA_ROPE_DIM = 32
MLA_V_DIM = 64
MLA_Q_RANK = 256
MLA_KV_RANK = 128
MLA_WIDTH = MLA_HEADS * MLA_V_DIM
ROPE_BASE = 10000.0
N_EXPERTS = 32
N_GROUPS = 8
EXPERTS_PER_GROUP = 4
TOP_K = 2
D_EXPERT = 256
DISPATCH_BLOCK = 512
DEEPNORM_ALPHA = (2 * DEPTH) ** 0.25
LN_EPS = 1e-5
RMS_EPS = 1e-6
LOG2_E = 1.4426950408889634
UNDERFLOW_LOG2 = 160.0

LANES = 128
HALF_ROPE = MLA_ROPE_DIM // 2
ROPE_LANE0 = MLA_NOPE_DIM
VMEM_LIMIT = 56 * 1024 * 1024

TOKEN_TILE = 512
ATT_TILE = 256
GRANULE = 16
GRANULE_SHIFT = 4
MLA_STEP_TILES = 9
MXU_DIM = 256
SLAB_ROWS = -(-(TOKEN_TILE * TOP_K + N_EXPERTS * (GRANULE - 1)) // MXU_DIM) * MXU_DIM


def _dot(a, b):
    return jnp.dot(a, b, preferred_element_type=F32)


def _dot_nt(a, b):
    return lax.dot_general(a, b, (((1,), (1,)), ((), ())), preferred_element_type=F32)


def _split_bf16(v):
    hi = v.astype(BF16)
    lo = (v - hi.astype(F32)).astype(BF16)
    return hi, lo


def _standardize(x):
    mu = jnp.mean(x, axis=-1, keepdims=True)
    xc = x - mu
    var = jnp.mean(xc * xc, axis=-1, keepdims=True)
    return xc * lax.rsqrt(var + LN_EPS)


def _ada_kernel(c_ref, w_ref, b_ref, o_ref):
    c = c_ref[...]
    ca = c * jax.nn.sigmoid(c)
    ca_hi, ca_lo = _split_bf16(ca)
    w_hi, w_lo = _split_bf16(w_ref[...])
    o_ref[...] = _dot(ca_hi, w_hi) + _dot(ca_lo, w_hi) + _dot(ca_hi, w_lo) + b_ref[...]


def _ada_call(c, ada_w, ada_b):
    depth, d, n6 = ada_w.shape
    b = c.shape[0]
    tn = 1536
    return pl.pallas_call(
        _ada_kernel,
        grid=(depth, n6 // tn),
        in_specs=[pl.BlockSpec((b, d), lambda l, j: (0, 0)),
                  pl.BlockSpec((None, d, tn), lambda l, j: (l, 0, j)),
                  pl.BlockSpec((None, 1, tn), lambda l, j: (l, 0, j))],
        out_specs=pl.BlockSpec((None, b, tn), lambda l, j: (l, 0, j)),
        out_shape=jax.ShapeDtypeStruct((depth, b, n6), F32),
        compiler_params=pltpu.CompilerParams(vmem_limit_bytes=VMEM_LIMIT),
        name="ada",
    )(c, ada_w, ada_b.reshape(depth, 1, n6))


def _rope_kernel(pos_ref, invf_ref, cos_ref, sin_ref):
    ang = pos_ref[...].astype(F32) * invf_ref[...]
    lane = lax.broadcasted_iota(jnp.int32, ang.shape, 1)
    rot = (lane >= ROPE_LANE0) & (lane < ROPE_LANE0 + MLA_ROPE_DIM)
    cos_ref[...] = jnp.where(rot, jnp.cos(ang), 1.0)
    sin_ref[...] = jnp.where(rot, jnp.sin(ang), 0.0)


def _rope_call(positions):
    n = positions.size
    tm = TOKEN_TILE
    inv_freq = ROPE_BASE ** (-jnp.arange(0, MLA_ROPE_DIM, 2, dtype=F32) / MLA_ROPE_DIM)
    invf = jnp.zeros((1, LANES), F32)
    invf = invf.at[0, ROPE_LANE0:ROPE_LANE0 + HALF_ROPE].set(inv_freq)
    invf = invf.at[0, ROPE_LANE0 + HALF_ROPE:ROPE_LANE0 + MLA_ROPE_DIM].set(inv_freq)
    return pl.pallas_call(
        _rope_kernel,
        grid=(n // tm,),
        in_specs=[pl.BlockSpec((tm, 1), lambda i: (i, 0)),
                  pl.BlockSpec((1, LANES), lambda i: (0, 0))],
        out_specs=[pl.BlockSpec((tm, LANES), lambda i: (i, 0)),
                   pl.BlockSpec((tm, LANES), lambda i: (i, 0))],
        out_shape=[jax.ShapeDtypeStruct((n, LANES), F32)] * 2,
        name="rope_tables",
    )(positions.reshape(n, 1), invf)


def _proj_kernel(x_ref, mod_ref, cos_ref, sin_ref, w1_ref, w2_ref, qn_ref, kvn_ref, wuq_ref,
                 wukv_ref, sbq_ref, sbk_ref, sbv_ref, mq_ref, mk_ref, mv_ref):
    mod = mod_ref[...]
    h = _standardize(x_ref[...]) * (1.0 + mod[1:2]) + mod[0:1]
    hb = h.astype(BF16)

    p1 = _dot(hb, w1_ref[...])
    sbq_ref[...] = (p1[:, :SB_WIDTH] * (SB_HEAD_DIM ** -0.5 * LOG2_E)).astype(BF16)
    sbk_ref[...] = p1[:, SB_WIDTH:2 * SB_WIDTH].astype(BF16)
    sbv_ref[...] = p1[:, 2 * SB_WIDTH:].astype(BF16)

    p2 = _dot(hb, w2_ref[...])
    q_lat = p2[:, :MLA_Q_RANK]
    kv_lat = p2[:, MLA_Q_RANK:MLA_Q_RANK + MLA_KV_RANK]
    k_rope = p2[:, MLA_Q_RANK + MLA_KV_RANK:]

    qn = q_lat * lax.rsqrt(jnp.mean(q_lat * q_lat, axis=-1, keepdims=True) + RMS_EPS) * qn_ref[...]
    kvn = kv_lat * lax.rsqrt(jnp.mean(kv_lat * kv_lat, axis=-1, keepdims=True) + RMS_EPS) * kvn_ref[...]
    q = _dot(qn.astype(BF16), wuq_ref[...])
    kv = _dot(kvn.astype(BF16), wukv_ref[...])

    cos = cos_ref[...]
    sin = sin_ref[...]
    lane = lax.broadcasted_iota(jnp.int32, cos.shape, 1)
    second = lane >= ROPE_LANE0 + HALF_ROPE
    sin_up = jnp.where(second, sin, 0.0)
    sin_dn = jnp.where(second, 0.0, -sin)

    def rope(t):
        return (t * cos + pltpu.roll(t, HALF_ROPE, 1) * sin_up
                + pltpu.roll(t, LANES - HALF_ROPE, 1) * sin_dn)

    kr = rope(k_rope)
    mla_scale = (MLA_NOPE_DIM + MLA_ROPE_DIM) ** -0.5 * LOG2_E
    for hd in range(MLA_HEADS):
        sl = slice(hd * LANES, (hd + 1) * LANES)
        mq_ref[:, sl] = (rope(q[:, sl]) * mla_scale).astype(BF16)
        mk_ref[:, sl] = (kv[:, sl] + kr).astype(BF16)
    vlane = lax.broadcasted_iota(jnp.int32, (1, MLA_HEADS * LANES), 1)
    quarter = lax.shift_right_logical(vlane, 6) & 3
    ones = jnp.where((quarter == 1) | (quarter == 2), 1.0, 0.0)
    mv_ref[...] = (kv[:, MLA_HEADS * LANES:] + ones).astype(BF16)


def _proj_call(x2d, mod, cos_t, sin_t, w1, w2, qn, kvn, wuq, wukv, seq):
    n, d = x2d.shape
    tm = TOKEN_TILE
    per_b = seq // tm
    tok = lambda i: (i, 0)
    full = lambda i: (0, 0)
    widths = (SB_WIDTH, SB_WIDTH, SB_WIDTH, MLA_HEADS * LANES, MLA_HEADS * LANES, MLA_HEADS * LANES)
    return pl.pallas_call(
        _proj_kernel,
        grid=(n // tm,),
        in_specs=[pl.BlockSpec((tm, d), tok),
                  pl.BlockSpec((None, 6, d), lambda i: (i // per_b, 0, 0)),
                  pl.BlockSpec((tm, LANES), tok),
                  pl.BlockSpec((tm, LANES), tok),
                  pl.BlockSpec(w1.shape, full),
                  pl.BlockSpec(w2.shape, full),
                  pl.BlockSpec(qn.shape, full),
                  pl.BlockSpec(kvn.shape, full),
                  pl.BlockSpec(wuq.shape, full),
                  pl.BlockSpec(wukv.shape, full)],
        out_specs=[pl.BlockSpec((tm, w), tok) for w in widths],
        out_shape=[jax.ShapeDtypeStruct((n, w), BF16) for w in widths],
        compiler_params=pltpu.CompilerParams(vmem_limit_bytes=VMEM_LIMIT),
        name="proj",
    )(x2d, mod, cos_t, sin_t, w1, w2, qn, kvn, wuq, wukv)


def _causal_tiles_until(i, step, state, live):
    state = lax.cond(i >= 1,
                     lambda s: step((i, i - 1), s, (True, False)),
                     lambda s: step((i,), s, (True,)), state)

    def more(carry):
        return (carry[0] >= 1) & live(carry[1])

    def pair(carry):
        j, s = carry
        return j - 2, step((j, j - 1), s, (False, False))

    j, state = lax.while_loop(more, pair, (i - 2, state))
    return lax.cond((j == 0) & live(state), lambda s: step((0,), s, (False,)), lambda s: s, state)


def _sb_kernel(q_ref, k_ref, v_ref, tri_ref, o_ref):
    t = ATT_TILE
    tri2 = tri_ref[...]
    lane = lax.broadcasted_iota(jnp.int32, (1, LANES), 1)
    row = lax.broadcasted_iota(jnp.int32, (t, t), 0)
    col = lax.broadcasted_iota(jnp.int32, (t, t), 1)
    strict = col < row
    mine = [lane < SB_HEAD_DIM, lane >= SB_HEAD_DIM]

    def step(qh, js, state, diagonal):
        chains = [(j, h, d) for j, d in zip(js, diagonal) for h in range(2)]
        ks = [k_ref[pl.ds(j * t, t), :] for j in js]
        zs = [_dot_nt(qh[h], ks[n]) for n in range(len(js)) for h in range(2)]
        nlks, pieces = [], []
        for (_, _, diag), z in zip(chains, zs):
            nlk = jnp.maximum(z, 0.0) + jnp.log(1.0 + jnp.exp2(-jnp.abs(z))) * LOG2_E
            if diag:
                nlk = jnp.where(strict, nlk, 0.0)
            nlks.append(nlk)
            pieces.append(jnp.concatenate(_split_bf16(nlk), axis=1))
        laters = [_dot(p, tri2) for p in pieces]
        ncarry = [state[h][0] for h in range(2)]
        ws = []
        for (j, h, diag), z, nlk, later in zip(chains, zs, nlks, laters):
            w = jnp.exp2((z - ncarry[h]) - later)
            if diag:
                w = jnp.where(strict, w, 0.0)
            ws.append(w.astype(BF16))
            ncarry[h] = ncarry[h] + jnp.sum(nlk, axis=1, keepdims=True)
        acc = [state[h][1] for h in range(2)]
        for (j, h, _), w in zip(chains, ws):
            acc[h] = acc[h] + _dot(w, v_ref[pl.ds(j * t, t), :])
        return tuple((ncarry[h], acc[h]) for h in range(2))

    def live(state):
        return jnp.minimum(jnp.min(state[0][0]), jnp.min(state[1][0])) < UNDERFLOW_LOG2

    def q_tile(i, carry):
        rows = pl.ds(pl.multiple_of(i * t, t), t)
        q = q_ref[rows, :]
        qh = [jnp.where(m, q, jnp.zeros_like(q)) for m in mine]
        init = tuple((jnp.zeros((t, 1), F32), jnp.zeros((t, LANES), F32)) for _ in range(2))
        state = _causal_tiles_until(i, lambda js, s, d: step(qh, js, s, d), init, live)
        o_ref[rows, :] = jnp.where(mine[0], state[0][1], state[1][1]).astype(o_ref.dtype)
        return carry

    lax.fori_loop(0, q_ref.shape[0] // t, q_tile, 0)


def _sb_call(q, k, v, tri, batch, seq):
    t = ATT_TILE
    pairs = SB_WIDTH // LANES
    seq_spec = pl.BlockSpec((seq, LANES), lambda b, p: (b, p))
    return pl.pallas_call(
        _sb_kernel,
        grid=(batch, pairs),
        in_specs=[seq_spec, seq_spec, seq_spec,
                  pl.BlockSpec((2 * t, t), lambda b, p: (0, 0))],
        out_specs=seq_spec,
        out_shape=jax.ShapeDtypeStruct(q.shape, BF16),
        compiler_params=pltpu.CompilerParams(vmem_limit_bytes=VMEM_LIMIT),
        name="sb_attention",
    )(q, k, v, tri)


def _mla_kernel(q_ref, k_ref, v_ref, o_ref):
    t = ATT_TILE
    lane = lax.broadcasted_iota(jnp.int32, (1, LANES), 1)
    row = lax.broadcasted_iota(jnp.int32, (t, t), 0)
    col = lax.broadcasted_iota(jnp.int32, (t, t), 1)
    causal = col <= row

    def head_lanes(hd):
        return slice(hd * LANES, (hd + 1) * LANES)

    def tile_rows(i):
        return slice(i * t, (i + 1) * t)

    def step(items, states):
        chains = [(qi, j, diag, hd) for qi, j, diag in items for hd in range(2)]
        ss = []
        for qi, j, diag, hd in chains:
            s = _dot_nt(q_ref[tile_rows(qi), head_lanes(hd)], k_ref[tile_rows(j), head_lanes(hd)])
            ss.append(jnp.where(causal, s, -jnp.inf) if diag else s)
        ps = [None] * len(chains)
        for key in sorted({(qi, hd) for qi, _, _, hd in chains}):
            mine_n = [n for n, (qi, _, _, hd) in enumerate(chains) if (qi, hd) == key]
            m, acc = states[key]
            m_new = m
            for n in mine_n:
                m_new = jnp.maximum(m_new, jnp.max(ss[n], axis=1, keepdims=True))
            for n in mine_n:
                ps[n] = jnp.exp2(ss[n] - m_new).astype(BF16)
            states[key] = (m_new, jnp.exp2(m - m_new) * acc)
        for n, (qi, j, _, hd) in enumerate(chains):
            m, acc = states[(qi, hd)]
            states[(qi, hd)] = (m, acc + _dot(ps[n], v_ref[tile_rows(j), head_lanes(hd)]))

    nq = q_ref.shape[0] // t
    groups = [(lo, nq - 1 - lo) for lo in range(nq // 2)] + ([(nq // 2,)] if nq % 2 else [])
    for group in groups:
        items = [(qi, qi - d, d == 0) for qi in reversed(group) for d in range(qi + 1)]
        states = {(qi, hd): (jnp.full((t, 1), -jnp.inf, F32), jnp.zeros((t, LANES), F32))
                  for qi in group for hd in range(2)}
        for first in range(0, len(items), MLA_STEP_TILES):
            step(items[first:first + MLA_STEP_TILES], states)
        for qi in group:
            acc0, acc1 = states[(qi, 0)][1], states[(qi, 1)][1]
            out = jnp.where(lane < MLA_V_DIM, acc0 / pltpu.roll(acc0, MLA_V_DIM, 1),
                            acc1 / pltpu.roll(acc1, MLA_V_DIM, 1))
            o_ref[tile_rows(qi), :] = out.astype(o_ref.dtype)


def _mla_call(q, k, v, batch, seq):
    pairs = MLA_WIDTH // LANES
    return pl.pallas_call(
        _mla_kernel,
        grid=(batch, pairs),
        in_specs=[pl.BlockSpec((seq, 2 * LANES), lambda b, p: (b, p)),
                  pl.BlockSpec((seq, 2 * LANES), lambda b, p: (b, p)),
                  pl.BlockSpec((seq, 2 * LANES), lambda b, p: (b, p))],
        out_specs=pl.BlockSpec((seq, LANES), lambda b, p: (b, p)),
        out_shape=jax.ShapeDtypeStruct((q.shape[0], MLA_WIDTH), BF16),
        compiler_params=pltpu.CompilerParams(vmem_limit_bytes=VMEM_LIMIT),
        name="mla_attention",
    )(q, k, v)


def _top2_sum(a, b, c, d):
    hi1, lo1 = jnp.maximum(a, b), jnp.minimum(a, b)
    hi2, lo2 = jnp.maximum(c, d), jnp.minimum(c, d)
    return jnp.maximum(hi1, hi2) + jnp.maximum(jnp.minimum(hi1, hi2), jnp.maximum(lo1, lo2))


def _first_argmax4(v):
    m = jnp.maximum(jnp.maximum(v[0], v[1]), jnp.maximum(v[2], v[3]))
    return jnp.where(v[0] == m, 0, jnp.where(v[1] == m, 1, jnp.where(v[2] == m, 2, 3)))


def _pick4(idx, v):
    return jnp.where(idx == 0, v[0], jnp.where(idx == 1, v[1], jnp.where(idx == 2, v[2], v[3])))


def _post_kernel(sb_ref, mla_ref, x_ref, mod_ref, wo_sb_ref, wo_mla_ref, g_ref, b_ref, rw_ref,
                 rb_ref, ut_ref, x1_ref, h2_ref, route_ref, gate_ref, cnt_ref):
    tm = x_ref.shape[0]
    mod = mod_ref[...]
    mix = _dot(sb_ref[...], wo_sb_ref[...]) + _dot(mla_ref[...], wo_mla_ref[...])
    x1 = _standardize(DEEPNORM_ALPHA * x_ref[...] + (1.0 + mod[2:3]) * mix) * g_ref[...] + b_ref[...]
    x1_ref[...] = x1
    h2 = _standardize(x1) * (1.0 + mod[4:5]) + mod[3:4]

    h_hi, h_lo = _split_bf16(h2)
    h2_ref[...] = h_hi
    rw = rw_ref[...]
    big = _dot(h_hi, rw)
    logits = big[:, :LANES] + big[:, LANES:] + _dot(h_lo, rw[:, :LANES])
    lt = logits.T[:N_EXPERTS]
    scores = jax.nn.sigmoid(lt)
    biased = scores + rb_ref[...]
    sc = [scores[N_GROUPS * p:N_GROUPS * (p + 1)] for p in range(EXPERTS_PER_GROUP)]
    bi = [biased[N_GROUPS * p:N_GROUPS * (p + 1)] for p in range(EXPERTS_PER_GROUP)]

    group_score = _top2_sum(*bi)
    gidx = lax.broadcasted_iota(jnp.int32, group_score.shape, 0)
    best = jnp.max(group_score, axis=0, keepdims=True)
    g_sel = jnp.min(jnp.where(group_score == best, gidx, N_GROUPS), axis=0, keepdims=True)
    in_sel = gidx == g_sel
    vb = [jnp.sum(jnp.where(in_sel, b, 0.0), axis=0, keepdims=True) for b in bi]
    vs = [jnp.sum(jnp.where(in_sel, s, 0.0), axis=0, keepdims=True) for s in sc]
    l1 = _first_argmax4(vb)
    vb2 = [jnp.where(l1 == p, -jnp.inf, vb[p]) for p in range(EXPERTS_PER_GROUP)]
    l2 = _first_argmax4(vb2)
    s1 = _pick4(l1, vs)
    s2 = _pick4(l2, vs)
    tot = s1 + s2
    e1 = g_sel * EXPERTS_PER_GROUP + l1
    e2 = g_sel * EXPERTS_PER_GROUP + l2

    eidx = lax.broadcasted_iota(jnp.int32, (N_EXPERTS, tm), 0)
    hit1 = eidx == e1
    hit2 = eidx == e2
    onehot = jnp.where(hit1 | hit2, 1.0, 0.0)
    before = _dot(onehot.astype(BF16), ut_ref[...])
    count = jnp.sum(onehot, axis=1, keepdims=True).astype(jnp.int32)
    granules = lax.shift_right_logical(count + (GRANULE - 1), GRANULE_SHIFT)
    lower = (lax.broadcasted_iota(jnp.int32, (N_EXPERTS, N_EXPERTS), 1)
             < lax.broadcasted_iota(jnp.int32, (N_EXPERTS, N_EXPERTS), 0))
    gran_f = jnp.broadcast_to(granules.astype(F32), (N_EXPERTS, LANES)).astype(BF16)
    start = _dot(jnp.where(lower, 1.0, 0.0).astype(BF16), gran_f)[:, 0:1] * GRANULE
    row_of = before + start
    p1 = jnp.sum(jnp.where(hit1, row_of, 0.0), axis=0, keepdims=True)
    p2 = jnp.sum(jnp.where(hit2, row_of, 0.0), axis=0, keepdims=True)
    cnt_ref[...] = jnp.broadcast_to(count, cnt_ref.shape)

    r8 = lax.broadcasted_iota(jnp.int32, (8, tm), 0)
    route_ref[...] = jnp.where(r8 == 0, p1.astype(jnp.int32), jnp.where(r8 == 1, p2.astype(jnp.int32), 0))
    r128 = lax.broadcasted_iota(jnp.int32, (LANES, tm), 0)
    cols = jnp.where(r128 == 0, s1 / tot, jnp.where(r128 == 1, s2 / tot,
                                                     jnp.where(r128 == 2, p1, jnp.where(r128 == 3, p2, 0.0))))
    gate_ref[...] = cols.T


def _post_call(sb_out, mla_out, x2d, mod, wo_sb, wo_mla, g, b, rw, rb, ut, seq):
    n, d = x2d.shape
    tm = TOKEN_TILE
    per_b = seq // tm
    tok = lambda i: (i, 0)
    full = lambda i: (0, 0)
    return pl.pallas_call(
        _post_kernel,
        grid=(n // tm,),
        in_specs=[pl.BlockSpec((tm, SB_WIDTH), tok),
                  pl.BlockSpec((tm, MLA_WIDTH), tok),
                  pl.BlockSpec((tm, d), tok),
                  pl.BlockSpec((None, 6, d), lambda i: (i // per_b, 0, 0)),
                  pl.BlockSpec(wo_sb.shape, full),
                  pl.BlockSpec(wo_mla.shape, full),
                  pl.BlockSpec((1, d), full),
                  pl.BlockSpec((1, d), full),
                  pl.BlockSpec(rw.shape, full),
                  pl.BlockSpec(rb.shape, full),
                  pl.BlockSpec(ut.shape, full)],
        out_specs=[pl.BlockSpec((tm, d), tok),
                   pl.BlockSpec((tm, d), tok),
                   pl.BlockSpec((8, tm), lambda i: (0, i)),
                   pl.BlockSpec((tm, LANES), tok),
                   pl.BlockSpec((None, N_EXPERTS, LANES), lambda i: (i, 0, 0))],
        out_shape=[jax.ShapeDtypeStruct((n, d), F32),
                   jax.ShapeDtypeStruct((n, d), BF16),
                   jax.ShapeDtypeStruct((8, n), jnp.int32),
                   jax.ShapeDtypeStruct((n, LANES), F32),
                   jax.ShapeDtypeStruct((n // tm, N_EXPERTS, LANES), jnp.int32)],
        compiler_params=pltpu.CompilerParams(vmem_limit_bytes=VMEM_LIMIT),
        name="post_attention",
    )(sb_out, mla_out, x2d, mod, wo_sb, wo_mla, g, b, rw, rb, ut)


def _granule_copy(src_ref, src_g, dst_ref, dst_g, sem, granules=1):
    rows = granules * GRANULE
    src = src_ref.at[pl.ds(pl.multiple_of(src_g * GRANULE, GRANULE), rows), :]
    dst = dst_ref.at[pl.ds(pl.multiple_of(dst_g * GRANULE, GRANULE), rows), :]
    return pltpu.make_async_copy(src, dst, sem)


def _start_segment(count, copy):
    def pair(p, carry):
        copy(2 * p, 2, 0).start()
        return carry

    lax.fori_loop(0, lax.shift_right_logical(count, 1), pair, 0)

    @pl.when((count & 1) == 1)
    def _():
        copy(count - 1, 1, 1).start()


def _wait_copies(count, copy):
    def wait(g, carry):
        copy().wait()
        return carry

    lax.fori_loop(0, count, wait, 0)


def _block_copy(zero_ref, buf_ref, block, sem):
    dst = buf_ref.at[pl.ds(pl.multiple_of(block * DISPATCH_BLOCK, DISPATCH_BLOCK), DISPATCH_BLOCK), :]
    return pltpu.make_async_copy(zero_ref, dst, sem)


def _dispatch_kernel(ng_ref, ls_ref, gs_ref, pairs_ref, singles_ref, tail_start_ref, tail_n_ref, n_used_ref,
                     route_ref, h_ref, buf_ref, slab_ref, zero_ref, sem):
    c = pl.program_id(0)
    last = pl.num_programs(0) - 1
    slot = c % 2
    tm = h_ref.shape[0]
    rows = slab_ref.shape[1]

    def wait_slab(tile, s):
        _wait_copies(pairs_ref[tile], lambda: _granule_copy(slab_ref.at[s], 0, buf_ref, 0, sem.at[2 * s], 2))
        _wait_copies(singles_ref[tile], lambda: _granule_copy(slab_ref.at[s], 0, buf_ref, 0, sem.at[2 * s + 1]))

    @pl.when(c == 0)
    def _():
        zero_ref[...] = jnp.zeros_like(zero_ref)

        def expert_tail(e, total):
            def granule(g, inner):
                _granule_copy(zero_ref, 0, buf_ref, tail_start_ref[e] + g, sem.at[4]).start()
                return inner

            lax.fori_loop(0, tail_n_ref[e], granule, 0)
            return total + tail_n_ref[e]

        n_tail = lax.fori_loop(0, N_EXPERTS, expert_tail, 0)
        n_blocks = buf_ref.shape[0] // DISPATCH_BLOCK

        def unused_block(b, carry):
            _block_copy(zero_ref, buf_ref, b, sem.at[5]).start()
            return carry

        lax.fori_loop(n_used_ref[0], n_blocks, unused_block, 0)
        _wait_copies(n_tail, lambda: _granule_copy(zero_ref, 0, buf_ref, 0, sem.at[4]))
        _wait_copies(n_blocks - n_used_ref[0], lambda: _block_copy(zero_ref, buf_ref, 0, sem.at[5]))

    @pl.when(c >= 2)
    def _():
        wait_slab(c - 2, slot)

    row = lax.broadcasted_iota(jnp.int32, (rows, tm), 0)
    route = route_ref[...]
    onehot = (row == route[0:1]) | (row == route[1:2])
    slab_ref[slot] = _dot(jnp.where(onehot, 1.0, 0.0).astype(BF16), h_ref[...]).astype(BF16)

    def expert(e, carry):
        idx = c * N_EXPERTS + e
        ls = ls_ref[idx]
        gs = gs_ref[idx]

        _start_segment(ng_ref[idx], lambda g, n, k: _granule_copy(
            slab_ref.at[slot], ls + g, buf_ref, gs + g, sem.at[2 * slot + k], n))
        return carry

    lax.fori_loop(0, N_EXPERTS, expert, 0)

    @pl.when(c == last)
    def _():
        wait_slab(c, slot)

        @pl.when(c >= 1)
        def _():
            wait_slab(c - 1, 1 - slot)


def _dispatch_call(tables, tails, n_used, route, h2, buf_rows):
    n, d = h2.shape
    tm = TOKEN_TILE
    grid_spec = pltpu.PrefetchScalarGridSpec(
        num_scalar_prefetch=8,
        grid=(n // tm,),
        in_specs=[pl.BlockSpec((8, tm), lambda i, *_: (0, i)),
                  pl.BlockSpec((tm, d), lambda i, *_: (i, 0))],
        out_specs=pl.BlockSpec(memory_space=pl.ANY),
        scratch_shapes=[pltpu.VMEM((2, SLAB_ROWS, d), BF16), pltpu.VMEM((DISPATCH_BLOCK, d), BF16),
                        pltpu.SemaphoreType.DMA((6,))],
    )
    return pl.pallas_call(
        _dispatch_kernel,
        grid_spec=grid_spec,
        out_shape=jax.ShapeDtypeStruct((buf_rows, d), BF16),
        compiler_params=pltpu.CompilerParams(dimension_semantics=("arbitrary",),
                                             vmem_limit_bytes=VMEM_LIMIT),
        name="dispatch",
    )(*tables, *tails, n_used, route, h2)


def _ffn_kernel(be_ref, nb_ref, x_ref, wg_ref, wu_ref, wd_ref, y_ref, wg_s, wu_s, wd_s):
    i = pl.program_id(0)
    prev = be_ref[jnp.maximum(i - 1, 0)]

    @pl.when((i == 0) | (be_ref[i] != prev))
    def _():
        wg_s[...] = wg_ref[...].astype(BF16)
        wu_s[...] = wu_ref[...].astype(BF16)
        wd_s[...] = wd_ref[...].astype(BF16)

    @pl.when(i < nb_ref[0])
    def _():
        xb = x_ref[...]
        g = _dot(xb, wg_s[...])
        u = _dot(xb, wu_s[...])
        a = g * jax.nn.sigmoid(g) * u
        y_ref[...] = _dot(a.astype(BF16), wd_s[...]).astype(y_ref.dtype)

    @pl.when(i >= nb_ref[0])
    def _():
        y_ref[...] = jnp.zeros_like(y_ref)


def _ffn_call(block_e, n_used, buf, w_gate, w_up, w_down, layer):
    rows, d = buf.shape
    bm = DISPATCH_BLOCK
    de = w_gate.shape[-1]
    grid_spec = pltpu.PrefetchScalarGridSpec(
        num_scalar_prefetch=2,
        grid=(rows // bm,),
        in_specs=[pl.BlockSpec((bm, d), lambda i, be, nb: (jnp.minimum(i, nb[0] - 1), 0)),
                  pl.BlockSpec((None, None, d, de), lambda i, be, nb: (layer, be[i], 0, 0)),
                  pl.BlockSpec((None, None, d, de), lambda i, be, nb: (layer, be[i], 0, 0)),
                  pl.BlockSpec((None, None, de, d), lambda i, be, nb: (layer, be[i], 0, 0))],
        out_specs=pl.BlockSpec((bm, d), lambda i, be, nb: (i, 0)),
        scratch_shapes=[pltpu.VMEM((d, de), BF16), pltpu.VMEM((d, de), BF16), pltpu.VMEM((de, d), BF16)],
    )
    return pl.pallas_call(
        _ffn_kernel,
        grid_spec=grid_spec,
        out_shape=jax.ShapeDtypeStruct((rows, d), BF16),
        compiler_params=pltpu.CompilerParams(dimension_semantics=("arbitrary",),
                                             vmem_limit_bytes=VMEM_LIMIT),
        name="expert_ffn",
    )(block_e, n_used, buf, w_gate, w_up, w_down)


def _combine_kernel(ng_ref, ls_ref, gs_ref, pairs_ref, singles_ref, x1_ref, gate_ref, mod_ref, g_ref, b_ref,
                    y_hbm_ref, o_ref, slab_ref, ffn_ref, sem):
    c = pl.program_id(0)
    last = pl.num_programs(0) - 1
    slot = c % 2
    tm = x1_ref.shape[0]
    rows = slab_ref.shape[1]

    def fetch(tile, into, first_expert, n_experts):
        def expert(e, carry):
            idx = tile * N_EXPERTS + e
            ls = ls_ref[idx]
            gs = gs_ref[idx]
            _start_segment(ng_ref[idx], lambda g, n, k: _granule_copy(
                y_hbm_ref, gs + g, slab_ref.at[into], ls + g, sem.at[2 * into + k], n))
            return carry

        lax.fori_loop(first_expert, first_expert + n_experts, expert, 0)

    def prefetch_next(part):
        @pl.when(c < last)
        def _():
            fetch(c + 1, 1 - slot, part * (N_EXPERTS // 4), N_EXPERTS // 4)

    @pl.when(c == 0)
    def _():
        slab_ref[...] = jnp.zeros_like(slab_ref)
        fetch(0, 0, 0, N_EXPERTS)

    _wait_copies(pairs_ref[c], lambda: _granule_copy(y_hbm_ref, 0, slab_ref.at[slot], 0, sem.at[2 * slot], 2))
    _wait_copies(singles_ref[c], lambda: _granule_copy(y_hbm_ref, 0, slab_ref.at[slot], 0, sem.at[2 * slot + 1]))

    prefetch_next(0)
    cols = gate_ref[...]
    lane = lax.broadcasted_iota(jnp.int32, (tm, rows), 1)
    ys = slab_ref[slot]
    pick1 = jnp.where(lane == cols[:, 2:3].astype(jnp.int32), 1.0, 0.0).astype(BF16)
    ffn_ref[...] = cols[:, 0:1] * _dot(pick1, ys)
    prefetch_next(1)
    pick2 = jnp.where(lane == cols[:, 3:4].astype(jnp.int32), 1.0, 0.0).astype(BF16)
    ffn_ref[...] += cols[:, 1:2] * _dot(pick2, slab_ref[slot])
    prefetch_next(2)
    mod = mod_ref[...]
    o_ref[...] = (_standardize(DEEPNORM_ALPHA * x1_ref[...] + (1.0 + mod[5:6]) * ffn_ref[...]) * g_ref[...]
                  + b_ref[...])
    prefetch_next(3)


def _combine_call(tables, x1, gates, mod, g, b, y_buf, seq):
    n, d = x1.shape
    tm = TOKEN_TILE
    per_b = seq // tm
    tok = lambda i, *_: (i, 0)
    full = lambda i, *_: (0, 0)
    grid_spec = pltpu.PrefetchScalarGridSpec(
        num_scalar_prefetch=5,
        grid=(n // tm,),
        in_specs=[pl.BlockSpec((tm, d), tok),
                  pl.BlockSpec((tm, LANES), tok),
                  pl.BlockSpec((None, 6, d), lambda i, *_: (i // per_b, 0, 0)),
                  pl.BlockSpec((1, d), full),
                  pl.BlockSpec((1, d), full),
                  pl.BlockSpec(memory_space=pl.ANY)],
        out_specs=pl.BlockSpec((tm, d), tok),
        scratch_shapes=[pltpu.VMEM((2, SLAB_ROWS, d), BF16), pltpu.VMEM((tm, d), F32),
                        pltpu.SemaphoreType.DMA((4,))],
    )
    return pl.pallas_call(
        _combine_kernel,
        grid_spec=grid_spec,
        out_shape=jax.ShapeDtypeStruct((n, d), F32),
        compiler_params=pltpu.CompilerParams(dimension_semantics=("arbitrary",),
                                             vmem_limit_bytes=VMEM_LIMIT),
        name="combine",
    )(*tables, x1, gates, mod, g, b, y_buf)


def _layer_weights(w_in, w_uq, w_ukv, w_o):
    d = w_in.shape[0]
    w1 = w_in[:, :3 * SB_WIDTH].astype(BF16)
    lat = 3 * SB_WIDTH + MLA_Q_RANK + MLA_KV_RANK
    w2 = jnp.concatenate([w_in[:, 3 * SB_WIDTH:lat], jnp.zeros((d, ROPE_LANE0), F32), w_in[:, lat:],
                          jnp.zeros((d, LANES - ROPE_LANE0 - MLA_ROPE_DIM), F32)], axis=1).astype(BF16)
    uq = w_uq.reshape(MLA_Q_RANK, MLA_HEADS, MLA_NOPE_DIM + MLA_ROPE_DIM)
    uq = jnp.pad(uq, ((0, 0), (0, 0), (0, LANES - MLA_NOPE_DIM - MLA_ROPE_DIM)))
    wuq = uq.reshape(MLA_Q_RANK, MLA_HEADS * LANES).astype(BF16)
    ukv = w_ukv.reshape(MLA_KV_RANK, MLA_HEADS, MLA_NOPE_DIM + MLA_V_DIM)
    uk = jnp.pad(ukv[:, :, :MLA_NOPE_DIM], ((0, 0), (0, 0), (0, LANES - MLA_NOPE_DIM)))
    uv = ukv[:, :, MLA_NOPE_DIM:]
    even_head = (jnp.arange(MLA_HEADS) % 2 == 0)[None, :, None]
    uv = jnp.where(even_head, jnp.pad(uv, ((0, 0), (0, 0), (0, LANES - MLA_V_DIM))),
                   jnp.pad(uv, ((0, 0), (0, 0), (LANES - MLA_V_DIM, 0))))
    wukv = jnp.concatenate([uk.reshape(MLA_KV_RANK, MLA_HEADS * LANES),
                            uv.reshape(MLA_KV_RANK, MLA_HEADS * LANES)], axis=1).astype(BF16)
    return w1, w2, wuq, wukv, w_o[:SB_WIDTH].astype(BF16), w_o[SB_WIDTH:].astype(BF16)


def _router_weights(router_w, router_bias):
    d = router_w.shape[0]
    rw = router_w.reshape(d, N_GROUPS, EXPERTS_PER_GROUP).transpose(0, 2, 1).reshape(d, N_EXPERTS)
    hi = rw.astype(BF16)
    lo = (rw - hi.astype(F32)).astype(BF16)
    pad = jnp.zeros((d, LANES - N_EXPERTS), BF16)
    rwcat = jnp.concatenate([hi, pad, lo, pad], axis=1)
    rb = router_bias.reshape(N_GROUPS, EXPERTS_PER_GROUP).T.reshape(N_EXPERTS, 1)
    return rwcat, rb


def _dispatch_plan(counts, n_blocks):
    bm = DISPATCH_BLOCK
    gran = (counts + GRANULE - 1) // GRANULE
    slab_start = jnp.cumsum(gran, axis=1) - gran
    per_expert = jnp.sum(gran, axis=0) * GRANULE
    padded = (per_expert + bm - 1) // bm * bm
    pend = jnp.cumsum(padded)
    buf_start = (pend - padded) // GRANULE + jnp.cumsum(gran, axis=0) - gran
    tables = (gran.reshape(-1), slab_start.reshape(-1), buf_start.reshape(-1),
              jnp.sum(gran // 2, axis=1), jnp.sum(gran % 2, axis=1))
    tails = ((pend - padded + per_expert) // GRANULE, (padded - per_expert) // GRANULE)
    block_row = jnp.arange(n_blocks, dtype=jnp.int32) * bm
    block_e = jnp.minimum(jnp.sum(block_row[:, None] >= pend[None, :], axis=1), N_EXPERTS - 1)
    n_used = (pend[-1] // bm).reshape(1)
    as_i32 = lambda ts: tuple(t.astype(jnp.int32) for t in ts)
    return as_i32(tables), as_i32(tails), block_e.astype(jnp.int32), n_used.astype(jnp.int32)


def kernel(x, c, positions, ada_w, ada_b, w_in, q_norm, kv_norm, w_uq, w_ukv, w_o, ln1_g, ln1_b,
           router_w, router_bias, w_gate, w_up, w_down, ln2_g, ln2_b):
    batch, seq, d = x.shape
    n = batch * seq
    depth = ada_w.shape[0]
    assert seq % TOKEN_TILE == 0 and seq % ATT_TILE == 0

    mod_all = _ada_call(c, ada_w, ada_b).reshape(depth, batch, 6, d)
    cos_t, sin_t = _rope_call(positions)
    rwcat, rb = _router_weights(router_w, router_bias)
    t = ATT_TILE
    tri = (jnp.arange(t)[:, None] >= jnp.arange(t)[None, :]).astype(BF16)
    tri = jnp.concatenate([tri, tri], axis=0)
    tm = TOKEN_TILE
    ut = (jnp.arange(tm)[:, None] < jnp.arange(tm)[None, :]).astype(BF16)
    max_rows = n * TOP_K + (n // tm) * N_EXPERTS * (GRANULE - 1) + N_EXPERTS * (DISPATCH_BLOCK - 1)
    n_blocks = -(-max_rows // DISPATCH_BLOCK)

    x2d = x.reshape(n, d)
    for l in range(depth):
        mod = mod_all[l]
        w1, w2, wuq, wukv, wo_sb, wo_mla = _layer_weights(w_in[l], w_uq[l], w_ukv[l], w_o[l])
        sbq, sbk, sbv, mq, mk, mv = _proj_call(x2d, mod, cos_t, sin_t, w1, w2, q_norm[l].reshape(1, -1),
                                               kv_norm[l].reshape(1, -1), wuq, wukv, seq)
        sb_out = _sb_call(sbq, sbk, sbv, tri, batch, seq)
        mla_out = _mla_call(mq, mk, mv, batch, seq)
        x1, h2, route, gates, counts = _post_call(sb_out, mla_out, x2d, mod, wo_sb, wo_mla,
                                                  ln1_g[l].reshape(1, d), ln1_b[l].reshape(1, d),
                                                  rwcat, rb, ut, seq)
        tables, tails, block_e, n_used = _dispatch_plan(counts[:, :, 0], n_blocks)
        buf = _dispatch_call(tables, tails, n_used, route, h2, n_blocks * DISPATCH_BLOCK)
        y_buf = _ffn_call(block_e, n_used, buf, w_gate, w_up, w_down, l)
        x2d = _combine_call(tables, x1, gates, mod, ln2_g[l].reshape(1, d), ln2_b[l].reshape(1, d), y_buf, seq)
    return x2d.reshape(batch, seq, d)
```

```python
import jax
import jax.numpy as jnp
from jax import lax
from jax.experimental import pallas as pl
from jax.experimental.pallas import tpu as pltpu

F32 = jnp.float32
BF16 = jnp.bfloat16

D_MODEL = 1024
DEPTH = 2
SB_HEADS = 8
SB_HEAD_DIM = 64
SB_WIDTH = SB_HEADS * SB_HEAD_DIM
MLA_HEADS = 8
MLA_NOPE_DIM = 64
MLA_ROPE_DIM = 32
MLA_V_DIM = 64
MLA_Q_RANK = 256
MLA_KV_RANK = 128
MLA_WIDTH = MLA_HEADS * MLA_V_DIM
ROPE_BASE = 10000.0
N_EXPERTS = 32
N_GROUPS = 8
EXPERTS_PER_GROUP = 4
TOP_K = 2
D_EXPERT = 256
DISPATCH_BLOCK = 512
DEEPNORM_ALPHA = (2 * DEPTH) ** 0.25
LN_EPS = 1e-5
RMS_EPS = 1e-6
LOG2_E = 1.4426950408889634
UNDERFLOW_LOG2 = 160.0

LANES = 128
HALF_ROPE = MLA_ROPE_DIM // 2
ROPE_LANE0 = MLA_NOPE_DIM
VMEM_LIMIT = 56 * 1024 * 1024

TOKEN_TILE = 512
ATT_TILE = 256
GRANULE = 16
GRANULE_SHIFT = 4
MLA_STEP_TILES = 9
MXU_DIM = 256
SLAB_ROWS = -(-(TOKEN_TILE * TOP_K + N_EXPERTS * (GRANULE - 1)) // MXU_DIM) * MXU_DIM


def _dot(a, b):
    return jnp.dot(a, b, preferred_element_type=F32)


def _dot_nt(a, b):
    return lax.dot_general(a, b, (((1,), (1,)), ((), ())), preferred_element_type=F32)


def _split_bf16(v):
    hi = v.astype(BF16)
    lo = (v - hi.astype(F32)).astype(BF16)
    return hi, lo


def _standardize(x):
    mu = jnp.mean(x, axis=-1, keepdims=True)
    xc = x - mu
    var = jnp.mean(xc * xc, axis=-1, keepdims=True)
    return xc * lax.rsqrt(var + LN_EPS)


def _ada_kernel(c_ref, w_ref, b_ref, o_ref):
    c = c_ref[...]
    ca = c * jax.nn.sigmoid(c)
    ca_hi, ca_lo = _split_bf16(ca)
    w_hi, w_lo = _split_bf16(w_ref[...])
    o_ref[...] = _dot(ca_hi, w_hi) + _dot(ca_lo, w_hi) + _dot(ca_hi, w_lo) + b_ref[...]


def _ada_call(c, ada_w, ada_b):
    depth, d, n6 = ada_w.shape
    b = c.shape[0]
    tn = 1536
    return pl.pallas_call(
        _ada_kernel,
        grid=(depth, n6 // tn),
        in_specs=[pl.BlockSpec((b, d), lambda l, j: (0, 0)),
                  pl.BlockSpec((None, d, tn), lambda l, j: (l, 0, j)),
                  pl.BlockSpec((None, 1, tn), lambda l, j: (l, 0, j))],
        out_specs=pl.BlockSpec((None, b, tn), lambda l, j: (l, 0, j)),
        out_shape=jax.ShapeDtypeStruct((depth, b, n6), F32),
        compiler_params=pltpu.CompilerParams(vmem_limit_bytes=VMEM_LIMIT),
        name="ada",
    )(c, ada_w, ada_b.reshape(depth, 1, n6))


def _rope_kernel(pos_ref, invf_ref, cos_ref, sin_ref):
    ang = pos_ref[...].astype(F32) * invf_ref[...]
    lane = lax.broadcasted_iota(jnp.int32, ang.shape, 1)
    rot = (lane >= ROPE_LANE0) & (lane < ROPE_LANE0 + MLA_ROPE_DIM)
    cos_ref[...] = jnp.where(rot, jnp.cos(ang), 1.0)
    sin_ref[...] = jnp.where(rot, jnp.sin(ang), 0.0)


def _rope_call(positions):
    n = positions.size
    tm = TOKEN_TILE
    inv_freq = ROPE_BASE ** (-jnp.arange(0, MLA_ROPE_DIM, 2, dtype=F32) / MLA_ROPE_DIM)
    invf = jnp.zeros((1, LANES), F32)
    invf = invf.at[0, ROPE_LANE0:ROPE_LANE0 + HALF_ROPE].set(inv_freq)
    invf = invf.at[0, ROPE_LANE0 + HALF_ROPE:ROPE_LANE0 + MLA_ROPE_DIM].set(inv_freq)
    return pl.pallas_call(
        _rope_kernel,
        grid=(n // tm,),
        in_specs=[pl.BlockSpec((tm, 1), lambda i: (i, 0)),
                  pl.BlockSpec((1, LANES), lambda i: (0, 0))],
        out_specs=[pl.BlockSpec((tm, LANES), lambda i: (i, 0)),
                   pl.BlockSpec((tm, LANES), lambda i: (i, 0))],
        out_shape=[jax.ShapeDtypeStruct((n, LANES), F32)] * 2,
        name="rope_tables",
    )(positions.reshape(n, 1), invf)


def _proj_kernel(x_ref, mod_ref, cos_ref, sin_ref, w1_ref, w2_ref, qn_ref, kvn_ref, wuq_ref,
                 wukv_ref, sbq_ref, sbk_ref, sbv_ref, mq_ref, mk_ref, mv_ref):
    mod = mod_ref[...]
    h = _standardize(x_ref[...]) * (1.0 + mod[1:2]) + mod[0:1]
    hb = h.astype(BF16)

    p1 = _dot(hb, w1_ref[...])
    sbq_ref[...] = (p1[:, :SB_WIDTH] * (SB_HEAD_DIM ** -0.5 * LOG2_E)).astype(BF16)
    sbk_ref[...] = p1[:, SB_WIDTH:2 * SB_WIDTH].astype(BF16)
    sbv_ref[...] = p1[:, 2 * SB_WIDTH:].astype(BF16)

    p2 = _dot(hb, w2_ref[...])
    q_lat = p2[:, :MLA_Q_RANK]
    kv_lat = p2[:, MLA_Q_RANK:MLA_Q_RANK + MLA_KV_RANK]
    k_rope = p2[:, MLA_Q_RANK + MLA_KV_RANK:]

    qn = q_lat * lax.rsqrt(jnp.mean(q_lat * q_lat, axis=-1, keepdims=True) + RMS_EPS) * qn_ref[...]
    kvn = kv_lat * lax.rsqrt(jnp.mean(kv_lat * kv_lat, axis=-1, keepdims=True) + RMS_EPS) * kvn_ref[...]
    q = _dot(qn.astype(BF16), wuq_ref[...])
    kv = _dot(kvn.astype(BF16), wukv_ref[...])

    cos = cos_ref[...]
    sin = sin_ref[...]
    lane = lax.broadcasted_iota(jnp.int32, cos.shape, 1)
    second = lane >= ROPE_LANE0 + HALF_ROPE
    sin_up = jnp.where(second, sin, 0.0)
    sin_dn = jnp.where(second, 0.0, -sin)

    def rope(t):
        return (t * cos + pltpu.roll(t, HALF_ROPE, 1) * sin_up
                + pltpu.roll(t, LANES - HALF_ROPE, 1) * sin_dn)

    kr = rope(k_rope)
    mla_scale = (MLA_NOPE_DIM + MLA_ROPE_DIM) ** -0.5 * LOG2_E
    for hd in range(MLA_HEADS):
        sl = slice(hd * LANES, (hd + 1) * LANES)
        mq_ref[:, sl] = (rope(q[:, sl]) * mla_scale).astype(BF16)
        mk_ref[:, sl] = (kv[:, sl] + kr).astype(BF16)
    vlane = lax.broadcasted_iota(jnp.int32, (1, MLA_HEADS * LANES), 1)
    quarter = lax.shift_right_logical(vlane, 6) & 3
    ones = jnp.where((quarter == 1) | (quarter == 2), 1.0, 0.0)
    mv_ref[...] = (kv[:, MLA_HEADS * LANES:] + ones).astype(BF16)


def _proj_call(x2d, mod, cos_t, sin_t, w1, w2, qn, kvn, wuq, wukv, seq):
    n, d = x2d.shape
    tm = TOKEN_TILE
    per_b = seq // tm
    tok = lambda i: (i, 0)
    full = lambda i: (0, 0)
    widths = (SB_WIDTH, SB_WIDTH, SB_WIDTH, MLA_HEADS * LANES, MLA_HEADS * LANES, MLA_HEADS * LANES)
    return pl.pallas_call(
        _proj_kernel,
        grid=(n // tm,),
        in_specs=[pl.BlockSpec((tm, d), tok),
                  pl.BlockSpec((None, 6, d), lambda i: (i // per_b, 0, 0)),
                  pl.BlockSpec((tm, LANES), tok),
                  pl.BlockSpec((tm, LANES), tok),
                  pl.BlockSpec(w1.shape, full),
                  pl.BlockSpec(w2.shape, full),
                  pl.BlockSpec(qn.shape, full),
                  pl.BlockSpec(kvn.shape, full),
                  pl.BlockSpec(wuq.shape, full),
                  pl.BlockSpec(wukv.shape, full)],
        out_specs=[pl.BlockSpec((tm, w), tok) for w in widths],
        out_shape=[jax.ShapeDtypeStruct((n, w), BF16) for w in widths],
        compiler_params=pltpu.CompilerParams(vmem_limit_bytes=VMEM_LIMIT),
        name="proj",
    )(x2d, mod, cos_t, sin_t, w1, w2, qn, kvn, wuq, wukv)


def _sb_kernel(q_ref, k_ref, v_ref, tri_ref, o_ref, carry_ref, acc_ref):
    t = ATT_TILE
    nq = q_ref.shape[0] // t
    tri2 = tri_ref[...]
    lane = lax.broadcasted_iota(jnp.int32, (1, LANES), 1)
    row = lax.broadcasted_iota(jnp.int32, (t, t), 0)
    col = lax.broadcasted_iota(jnp.int32, (t, t), 1)
    strict = col < row
    mine = [lane < SB_HEAD_DIM, lane >= SB_HEAD_DIM]

    def head_queries(rows):
        q = q_ref[rows, :]
        return [jnp.where(m, q, jnp.zeros_like(q)) for m in mine]

    def step(items, states, queries):
        chains = [(qk, j, diag, h) for qk, j, diag in items for h in range(2)]
        ks = [k_ref[pl.ds(j * t, t), :] for _, j, _ in items]
        zs = [_dot_nt(queries[qk][h], ks[n]) for n, (qk, _, _) in enumerate(items) for h in range(2)]
        nlks, pieces = [], []
        for (_, _, diag, _), z in zip(chains, zs):
            nlk = jnp.maximum(z, 0.0) + jnp.log(1.0 + jnp.exp2(-jnp.abs(z))) * LOG2_E
            if diag:
                nlk = jnp.where(strict, nlk, 0.0)
            nlks.append(nlk)
            pieces.append(jnp.concatenate(_split_bf16(nlk), axis=1))
        laters = [_dot(p, tri2) for p in pieces]
        ncarry = {key: st[0] for key, st in states.items()}
        ws = []
        for (qk, j, diag, h), z, nlk, later in zip(chains, zs, nlks, laters):
            w = jnp.exp2((z - ncarry[(qk, h)]) - later)
            if diag:
                w = jnp.where(strict, w, 0.0)
            ws.append(w.astype(BF16))
            ncarry[(qk, h)] = ncarry[(qk, h)] + jnp.sum(nlk, axis=1, keepdims=True)
        acc = {key: st[1] for key, st in states.items()}
        for (qk, j, _, h), w in zip(chains, ws):
            acc[(qk, h)] = acc[(qk, h)] + _dot(w, v_ref[pl.ds(j * t, t), :])
        for key in states:
            states[key] = (ncarry[key], acc[key])

    def live(state):
        return jnp.minimum(jnp.min(state[0][0]), jnp.min(state[1][0])) < UNDERFLOW_LOG2

    def merged(state):
        return jnp.where(mine[0], state[0][1], state[1][1]).astype(o_ref.dtype)

    for first in range(0, nq, 2):
        tiles = [qi for qi in (first, first + 1) if qi < nq]
        queries = {qi: head_queries(pl.ds(qi * t, t)) for qi in tiles}
        states = {(qi, h): (jnp.zeros((t, 1), F32), jnp.zeros((t, LANES), F32)) for qi in tiles for h in range(2)}
        items = [(qi, qi - d, d == 0) for qi in tiles for d in range(min(qi, 1) + 1)]
        step(items, states, queries)
        for qi in tiles:
            rows = pl.ds(qi * t, t)
            o_ref[rows, :] = merged((states[(qi, 0)], states[(qi, 1)]))
            for h in range(2):
                carry_ref[h, rows, :] = jnp.broadcast_to(states[(qi, h)][0], (t, LANES))
                acc_ref[h, rows, :] = states[(qi, h)][1]

    def finish(i, carry):
        rows = pl.ds(pl.multiple_of(i * t, t), t)
        ncarry = [carry_ref[h, rows, 0:1] for h in range(2)]

        @pl.when(live(((ncarry[0],), (ncarry[1],))))
        def _():
            queries = {0: head_queries(rows)}

            def dynamic_step(js, state):
                states = {(0, h): state[h] for h in range(2)}
                step([(0, j, False) for j in js], states, queries)
                return tuple(states[(0, h)] for h in range(2))

            def more(c):
                return (c[0] >= 1) & live(c[1])

            def pair(c):
                return c[0] - 2, dynamic_step((c[0], c[0] - 1), c[1])

            state = tuple((ncarry[h], acc_ref[h, rows, :]) for h in range(2))
            j, state = lax.while_loop(more, pair, (i - 2, state))
            state = lax.cond((j == 0) & live(state), lambda s: dynamic_step((0,), s), lambda s: s, state)
            o_ref[rows, :] = merged(state)

        return carry

    lax.fori_loop(2, nq, finish, 0)


def _sb_call(q, k, v, tri, batch, seq):
    t = ATT_TILE
    pairs = SB_WIDTH // LANES
    seq_spec = pl.BlockSpec((seq, LANES), lambda b, p: (b, p))
    return pl.pallas_call(
        _sb_kernel,
        grid=(batch, pairs),
        in_specs=[seq_spec, seq_spec, seq_spec,
                  pl.BlockSpec((2 * t, t), lambda b, p: (0, 0))],
        out_specs=seq_spec,
        out_shape=jax.ShapeDtypeStruct(q.shape, BF16),
        scratch_shapes=[pltpu.VMEM((2, seq, LANES), F32), pltpu.VMEM((2, seq, LANES), F32)],
        compiler_params=pltpu.CompilerParams(vmem_limit_bytes=VMEM_LIMIT),
        name="sb_attention",
    )(q, k, v, tri)


def _mla_kernel(q_ref, k_ref, v_ref, o_ref):
    t = ATT_TILE
    lane = lax.broadcasted_iota(jnp.int32, (1, LANES), 1)
    row = lax.broadcasted_iota(jnp.int32, (t, t), 0)
    col = lax.broadcasted_iota(jnp.int32, (t, t), 1)
    causal = col <= row

    def head_lanes(hd):
        return slice(hd * LANES, (hd + 1) * LANES)

    def tile_rows(i):
        return slice(i * t, (i + 1) * t)

    def step(items, states):
        chains = [(qi, j, diag, hd) for qi, j, diag in items for hd in range(2)]
        ss = []
        for qi, j, diag, hd in chains:
            s = _dot_nt(q_ref[tile_rows(qi), head_lanes(hd)], k_ref[tile_rows(j), head_lanes(hd)])
            ss.append(jnp.where(causal, s, -jnp.inf) if diag else s)
        ps = [None] * len(chains)
        for key in sorted({(qi, hd) for qi, _, _, hd in chains}):
            mine_n = [n for n, (qi, _, _, hd) in enumerate(chains) if (qi, hd) == key]
            m, acc = states[key]
            m_new = m
            for n in mine_n:
                m_new = jnp.maximum(m_new, jnp.max(ss[n], axis=1, keepdims=True))
            for n in mine_n:
                ps[n] = jnp.exp2(ss[n] - m_new).astype(BF16)
            states[key] = (m_new, jnp.exp2(m - m_new) * acc)
        for n, (qi, j, _, hd) in enumerate(chains):
            m, acc = states[(qi, hd)]
            states[(qi, hd)] = (m, acc + _dot(ps[n], v_ref[tile_rows(j), head_lanes(hd)]))

    nq = q_ref.shape[0] // t
    groups = [(lo, nq - 1 - lo) for lo in range(nq // 2)] + ([(nq // 2,)] if nq % 2 else [])
    for group in groups:
        items = [(qi, qi - d, d == 0) for qi in reversed(group) for d in range(qi + 1)]
        states = {(qi, hd): (jnp.full((t, 1), -jnp.inf, F32), jnp.zeros((t, LANES), F32))
                  for qi in group for hd in range(2)}
        for first in range(0, len(items), MLA_STEP_TILES):
            step(items[first:first + MLA_STEP_TILES], states)
        for qi in group:
            acc0, acc1 = states[(qi, 0)][1], states[(qi, 1)][1]
            out = jnp.where(lane < MLA_V_DIM, acc0 / pltpu.roll(acc0, MLA_V_DIM, 1),
                            acc1 / pltpu.roll(acc1, MLA_V_DIM, 1))
            o_ref[tile_rows(qi), :] = out.astype(o_ref.dtype)


def _mla_call(q, k, v, batch, seq):
    pairs = MLA_WIDTH // LANES
    return pl.pallas_call(
        _mla_kernel,
        grid=(batch, pairs),
        in_specs=[pl.BlockSpec((seq, 2 * LANES), lambda b, p: (b, p)),
                  pl.BlockSpec((seq, 2 * LANES), lambda b, p: (b, p)),
                  pl.BlockSpec((seq, 2 * LANES), lambda b, p: (b, p))],
        out_specs=pl.BlockSpec((seq, LANES), lambda b, p: (b, p)),
        out_shape=jax.ShapeDtypeStruct((q.shape[0], MLA_WIDTH), BF16),
        compiler_params=pltpu.CompilerParams(vmem_limit_bytes=VMEM_LIMIT),
        name="mla_attention",
    )(q, k, v)


def _top2_sum(a, b, c, d):
    hi1, lo1 = jnp.maximum(a, b), jnp.minimum(a, b)
    hi2, lo2 = jnp.maximum(c, d), jnp.minimum(c, d)
    return jnp.maximum(hi1, hi2) + jnp.maximum(jnp.minimum(hi1, hi2), jnp.maximum(lo1, lo2))


def _first_argmax4(v):
    m = jnp.maximum(jnp.maximum(v[0], v[1]), jnp.maximum(v[2], v[3]))
    return jnp.where(v[0] == m, 0, jnp.where(v[1] == m, 1, jnp.where(v[2] == m, 2, 3)))


def _pick4(idx, v):
    return jnp.where(idx == 0, v[0], jnp.where(idx == 1, v[1], jnp.where(idx == 2, v[2], v[3])))


def _post_kernel(sb_ref, mla_ref, x_ref, mod_ref, wo_sb_ref, wo_mla_ref, g_ref, b_ref, rw_ref,
                 rb_ref, ut_ref, x1_ref, h2_ref, route_ref, gate_ref, cnt_ref):
    tm = x_ref.shape[0]
    mod = mod_ref[...]
    mix = _dot(sb_ref[...], wo_sb_ref[...]) + _dot(mla_ref[...], wo_mla_ref[...])
    x1 = _standardize(DEEPNORM_ALPHA * x_ref[...] + (1.0 + mod[2:3]) * mix) * g_ref[...] + b_ref[...]
    x1_ref[...] = x1
    h2 = _standardize(x1) * (1.0 + mod[4:5]) + mod[3:4]

    h_hi, h_lo = _split_bf16(h2)
    h2_ref[...] = h_hi
    rw = rw_ref[...]
    big = _dot(h_hi, rw)
    logits = big[:, :LANES] + big[:, LANES:] + _dot(h_lo, rw[:, :LANES])
    lt = logits.T[:N_EXPERTS]
    scores = jax.nn.sigmoid(lt)
    biased = scores + rb_ref[...]
    sc = [scores[N_GROUPS * p:N_GROUPS * (p + 1)] for p in range(EXPERTS_PER_GROUP)]
    bi = [biased[N_GROUPS * p:N_GROUPS * (p + 1)] for p in range(EXPERTS_PER_GROUP)]

    group_score = _top2_sum(*bi)
    gidx = lax.broadcasted_iota(jnp.int32, group_score.shape, 0)
    best = jnp.max(group_score, axis=0, keepdims=True)
    g_sel = jnp.min(jnp.where(group_score == best, gidx, N_GROUPS), axis=0, keepdims=True)
    in_sel = gidx == g_sel
    vb = [jnp.sum(jnp.where(in_sel, b, 0.0), axis=0, keepdims=True) for b in bi]
    vs = [jnp.sum(jnp.where(in_sel, s, 0.0), axis=0, keepdims=True) for s in sc]
    l1 = _first_argmax4(vb)
    vb2 = [jnp.where(l1 == p, -jnp.inf, vb[p]) for p in range(EXPERTS_PER_GROUP)]
    l2 = _first_argmax4(vb2)
    s1 = _pick4(l1, vs)
    s2 = _pick4(l2, vs)
    tot = s1 + s2
    e1 = g_sel * EXPERTS_PER_GROUP + l1
    e2 = g_sel * EXPERTS_PER_GROUP + l2

    eidx = lax.broadcasted_iota(jnp.int32, (N_EXPERTS, tm), 0)
    hit1 = eidx == e1
    hit2 = eidx == e2
    onehot = jnp.where(hit1 | hit2, 1.0, 0.0)
    before = _dot(onehot.astype(BF16), ut_ref[...])
    count = jnp.sum(onehot, axis=1, keepdims=True).astype(jnp.int32)
    granules = lax.shift_right_logical(count + (GRANULE - 1), GRANULE_SHIFT)
    lower = (lax.broadcasted_iota(jnp.int32, (N_EXPERTS, N_EXPERTS), 1)
             < lax.broadcasted_iota(jnp.int32, (N_EXPERTS, N_EXPERTS), 0))
    gran_f = jnp.broadcast_to(granules.astype(F32), (N_EXPERTS, LANES)).astype(BF16)
    start = _dot(jnp.where(lower, 1.0, 0.0).astype(BF16), gran_f)[:, 0:1] * GRANULE
    row_of = before + start
    p1 = jnp.sum(jnp.where(hit1, row_of, 0.0), axis=0, keepdims=True)
    p2 = jnp.sum(jnp.where(hit2, row_of, 0.0), axis=0, keepdims=True)
    cnt_ref[...] = jnp.broadcast_to(count, cnt_ref.shape)

    r8 = lax.broadcasted_iota(jnp.int32, (8, tm), 0)
    route_ref[...] = jnp.where(r8 == 0, p1.astype(jnp.int32), jnp.where(r8 == 1, p2.astype(jnp.int32), 0))
    r128 = lax.broadcasted_iota(jnp.int32, (LANES, tm), 0)
    cols = jnp.where(r128 == 0, s1 / tot, jnp.where(r128 == 1, s2 / tot,
                                                     jnp.where(r128 == 2, p1, jnp.where(r128 == 3, p2, 0.0))))
    gate_ref[...] = cols.T


def _post_call(sb_out, mla_out, x2d, mod, wo_sb, wo_mla, g, b, rw, rb, ut, seq):
    n, d = x2d.shape
    tm = TOKEN_TILE
    per_b = seq // tm
    tok = lambda i: (i, 0)
    full = lambda i: (0, 0)
    return pl.pallas_call(
        _post_kernel,
        grid=(n // tm,),
        in_specs=[pl.BlockSpec((tm, SB_WIDTH), tok),
                  pl.BlockSpec((tm, MLA_WIDTH), tok),
                  pl.BlockSpec((tm, d), tok),
                  pl.BlockSpec((None, 6, d), lambda i: (i // per_b, 0, 0)),
                  pl.BlockSpec(wo_sb.shape, full),
                  pl.BlockSpec(wo_mla.shape, full),
                  pl.BlockSpec((1, d), full),
                  pl.BlockSpec((1, d), full),
                  pl.BlockSpec(rw.shape, full),
                  pl.BlockSpec(rb.shape, full),
                  pl.BlockSpec(ut.shape, full)],
        out_specs=[pl.BlockSpec((tm, d), tok),
                   pl.BlockSpec((tm, d), tok),
                   pl.BlockSpec((8, tm), lambda i: (0, i)),
                   pl.BlockSpec((tm, LANES), tok),
                   pl.BlockSpec((None, N_EXPERTS, LANES), lambda i: (i, 0, 0))],
        out_shape=[jax.ShapeDtypeStruct((n, d), F32),
                   jax.ShapeDtypeStruct((n, d), BF16),
                   jax.ShapeDtypeStruct((8, n), jnp.int32),
                   jax.ShapeDtypeStruct((n, LANES), F32),
                   jax.ShapeDtypeStruct((n // tm, N_EXPERTS, LANES), jnp.int32)],
        compiler_params=pltpu.CompilerParams(vmem_limit_bytes=VMEM_LIMIT),
        name="post_attention",
    )(sb_out, mla_out, x2d, mod, wo_sb, wo_mla, g, b, rw, rb, ut)


def _granule_copy(src_ref, src_g, dst_ref, dst_g, sem, granules=1):
    rows = granules * GRANULE
    src = src_ref.at[pl.ds(pl.multiple_of(src_g * GRANULE, GRANULE), rows), :]
    dst = dst_ref.at[pl.ds(pl.multiple_of(dst_g * GRANULE, GRANULE), rows), :]
    return pltpu.make_async_copy(src, dst, sem)


def _start_segment(count, copy):
    def pair(p, carry):
        copy(2 * p, 2, 0).start()
        return carry

    lax.fori_loop(0, lax.shift_right_logical(count, 1), pair, 0)

    @pl.when((count & 1) == 1)
    def _():
        copy(count - 1, 1, 1).start()


def _wait_copies(count, copy):
    def wait(g, carry):
        copy().wait()
        return carry

    lax.fori_loop(0, count, wait, 0)


def _block_copy(zero_ref, buf_ref, block, sem):
    dst = buf_ref.at[pl.ds(pl.multiple_of(block * DISPATCH_BLOCK, DISPATCH_BLOCK), DISPATCH_BLOCK), :]
    return pltpu.make_async_copy(zero_ref, dst, sem)


def _dispatch_kernel(ng_ref, ls_ref, gs_ref, pairs_ref, singles_ref, tail_start_ref, tail_n_ref, n_used_ref,
                     route_ref, h_ref, buf_ref, slab_ref, zero_ref, sem):
    c = pl.program_id(0)
    last = pl.num_programs(0) - 1
    slot = c % 2
    tm = h_ref.shape[0]
    rows = slab_ref.shape[1]

    def wait_slab(tile, s):
        _wait_copies(pairs_ref[tile], lambda: _granule_copy(slab_ref.at[s], 0, buf_ref, 0, sem.at[2 * s], 2))
        _wait_copies(singles_ref[tile], lambda: _granule_copy(slab_ref.at[s], 0, buf_ref, 0, sem.at[2 * s + 1]))

    @pl.when(c == 0)
    def _():
        zero_ref[...] = jnp.zeros_like(zero_ref)

        def expert_tail(e, total):
            def granule(g, inner):
                _granule_copy(zero_ref, 0, buf_ref, tail_start_ref[e] + g, sem.at[4]).start()
                return inner

            lax.fori_loop(0, tail_n_ref[e], granule, 0)
            return total + tail_n_ref[e]

        n_tail = lax.fori_loop(0, N_EXPERTS, expert_tail, 0)
        n_blocks = buf_ref.shape[0] // DISPATCH_BLOCK

        def unused_block(b, carry):
            _block_copy(zero_ref, buf_ref, b, sem.at[5]).start()
            return carry

        lax.fori_loop(n_used_ref[0], n_blocks, unused_block, 0)
        _wait_copies(n_tail, lambda: _granule_copy(zero_ref, 0, buf_ref, 0, sem.at[4]))
        _wait_copies(n_blocks - n_used_ref[0], lambda: _block_copy(zero_ref, buf_ref, 0, sem.at[5]))

    @pl.when(c >= 2)
    def _():
        wait_slab(c - 2, slot)

    row = lax.broadcasted_iota(jnp.int32, (rows, tm), 0)
    route = route_ref[...]
    onehot = (row == route[0:1]) | (row == route[1:2])
    slab_ref[slot] = _dot(jnp.where(onehot, 1.0, 0.0).astype(BF16), h_ref[...]).astype(BF16)

    def expert(e, carry):
        idx = c * N_EXPERTS + e
        ls = ls_ref[idx]
        gs = gs_ref[idx]

        _start_segment(ng_ref[idx], lambda g, n, k: _granule_copy(
            slab_ref.at[slot], ls + g, buf_ref, gs + g, sem.at[2 * slot + k], n))
        return carry

    lax.fori_loop(0, N_EXPERTS, expert, 0)

    @pl.when(c == last)
    def _():
        wait_slab(c, slot)

        @pl.when(c >= 1)
        def _():
            wait_slab(c - 1, 1 - slot)


def _dispatch_call(tables, tails, n_used, route, h2, buf_rows):
    n, d = h2.shape
    tm = TOKEN_TILE
    grid_spec = pltpu.PrefetchScalarGridSpec(
        num_scalar_prefetch=8,
        grid=(n // tm,),
        in_specs=[pl.BlockSpec((8, tm), lambda i, *_: (0, i)),
                  pl.BlockSpec((tm, d), lambda i, *_: (i, 0))],
        out_specs=pl.BlockSpec(memory_space=pl.ANY),
        scratch_shapes=[pltpu.VMEM((2, SLAB_ROWS, d), BF16), pltpu.VMEM((DISPATCH_BLOCK, d), BF16),
                        pltpu.SemaphoreType.DMA((6,))],
    )
    return pl.pallas_call(
        _dispatch_kernel,
        grid_spec=grid_spec,
        out_shape=jax.ShapeDtypeStruct((buf_rows, d), BF16),
        compiler_params=pltpu.CompilerParams(dimension_semantics=("arbitrary",),
                                             vmem_limit_bytes=VMEM_LIMIT),
        name="dispatch",
    )(*tables, *tails, n_used, route, h2)


def _ffn_kernel(be_ref, nb_ref, x_ref, wg_ref, wu_ref, wd_ref, y_ref, wg_s, wu_s, wd_s):
    i = pl.program_id(0)
    prev = be_ref[jnp.maximum(i - 1, 0)]

    @pl.when((i == 0) | (be_ref[i] != prev))
    def _():
        wg_s[...] = wg_ref[...].astype(BF16)
        wu_s[...] = wu_ref[...].astype(BF16)
        wd_s[...] = wd_ref[...].astype(BF16)

    @pl.when(i < nb_ref[0])
    def _():
        xb = x_ref[...]
        g = _dot(xb, wg_s[...])
        u = _dot(xb, wu_s[...])
        a = g * jax.nn.sigmoid(g) * u
        y_ref[...] = _dot(a.astype(BF16), wd_s[...]).astype(y_ref.dtype)

    @pl.when(i >= nb_ref[0])
    def _():
        y_ref[...] = jnp.zeros_like(y_ref)


def _ffn_call(block_e, n_used, buf, w_gate, w_up, w_down, layer):
    rows, d = buf.shape
    bm = DISPATCH_BLOCK
    de = w_gate.shape[-1]
    grid_spec = pltpu.PrefetchScalarGridSpec(
        num_scalar_prefetch=2,
        grid=(rows // bm,),
        in_specs=[pl.BlockSpec((bm, d), lambda i, be, nb: (jnp.minimum(i, nb[0] - 1), 0)),
                  pl.BlockSpec((None, None, d, de), lambda i, be, nb: (layer, be[i], 0, 0)),
                  pl.BlockSpec((None, None, d, de), lambda i, be, nb: (layer, be[i], 0, 0)),
                  pl.BlockSpec((None, None, de, d), lambda i, be, nb: (layer, be[i], 0, 0))],
        out_specs=pl.BlockSpec((bm, d), lambda i, be, nb: (i, 0)),
        scratch_shapes=[pltpu.VMEM((d, de), BF16), pltpu.VMEM((d, de), BF16), pltpu.VMEM((de, d), BF16)],
    )
    return pl.pallas_call(
        _ffn_kernel,
        grid_spec=grid_spec,
        out_shape=jax.ShapeDtypeStruct((rows, d), BF16),
        compiler_params=pltpu.CompilerParams(dimension_semantics=("arbitrary",),
                                             vmem_limit_bytes=VMEM_LIMIT),
        name="expert_ffn",
    )(block_e, n_used, buf, w_gate, w_up, w_down)


def _combine_kernel(ng_ref, ls_ref, gs_ref, pairs_ref, singles_ref, x1_ref, gate_ref, mod_ref, g_ref, b_ref,
                    y_hbm_ref, o_ref, slab_ref, ffn_ref, sem):
    c = pl.program_id(0)
    last = pl.num_programs(0) - 1
    slot = c % 2
    tm = x1_ref.shape[0]
    rows = slab_ref.shape[1]

    def fetch(tile, into, first_expert, n_experts):
        def expert(e, carry):
            idx = tile * N_EXPERTS + e
            ls = ls_ref[idx]
            gs = gs_ref[idx]
            _start_segment(ng_ref[idx], lambda g, n, k: _granule_copy(
                y_hbm_ref, gs + g, slab_ref.at[into], ls + g, sem.at[2 * into + k], n))
            return carry

        lax.fori_loop(first_expert, first_expert + n_experts, expert, 0)

    def prefetch_next(part):
        @pl.when(c < last)
        def _():
            fetch(c + 1, 1 - slot, part * (N_EXPERTS // 4), N_EXPERTS // 4)

    @pl.when(c == 0)
    def _():
        slab_ref[...] = jnp.zeros_like(slab_ref)
        fetch(0, 0, 0, N_EXPERTS)

    _wait_copies(pairs_ref[c], lambda: _granule_copy(y_hbm_ref, 0, slab_ref.at[slot], 0, sem.at[2 * slot], 2))
    _wait_copies(singles_ref[c], lambda: _granule_copy(y_hbm_ref, 0, slab_ref.at[slot], 0, sem.at[2 * slot + 1]))

    prefetch_next(0)
    cols = gate_ref[...]
    lane = lax.broadcasted_iota(jnp.int32, (tm, rows), 1)
    ys = slab_ref[slot]
    pick1 = jnp.where(lane == cols[:, 2:3].astype(jnp.int32), 1.0, 0.0).astype(BF16)
    ffn_ref[...] = cols[:, 0:1] * _dot(pick1, ys)
    prefetch_next(1)
    pick2 = jnp.where(lane == cols[:, 3:4].astype(jnp.int32), 1.0, 0.0).astype(BF16)
    ffn_ref[...] += cols[:, 1:2] * _dot(pick2, slab_ref[slot])
    prefetch_next(2)
    mod = mod_ref[...]
    o_ref[...] = (_standardize(DEEPNORM_ALPHA * x1_ref[...] + (1.0 + mod[5:6]) * ffn_ref[...]) * g_ref[...]
                  + b_ref[...])
    prefetch_next(3)


def _combine_call(tables, x1, gates, mod, g, b, y_buf, seq):
    n, d = x1.shape
    tm = TOKEN_TILE
    per_b = seq // tm
    tok = lambda i, *_: (i, 0)
    full = lambda i, *_: (0, 0)
    grid_spec = pltpu.PrefetchScalarGridSpec(
        num_scalar_prefetch=5,
        grid=(n // tm,),
        in_specs=[pl.BlockSpec((tm, d), tok),
                  pl.BlockSpec((tm, LANES), tok),
                  pl.BlockSpec((None, 6, d), lambda i, *_: (i // per_b, 0, 0)),
                  pl.BlockSpec((1, d), full),
                  pl.BlockSpec((1, d), full),
                  pl.BlockSpec(memory_space=pl.ANY)],
        out_specs=pl.BlockSpec((tm, d), tok),
        scratch_shapes=[pltpu.VMEM((2, SLAB_ROWS, d), BF16), pltpu.VMEM((tm, d), F32),
                        pltpu.SemaphoreType.DMA((4,))],
    )
    return pl.pallas_call(
        _combine_kernel,
        grid_spec=grid_spec,
        out_shape=jax.ShapeDtypeStruct((n, d), F32),
        compiler_params=pltpu.CompilerParams(dimension_semantics=("arbitrary",),
                                             vmem_limit_bytes=VMEM_LIMIT),
        name="combine",
    )(*tables, x1, gates, mod, g, b, y_buf)


def _layer_weights(w_in, w_uq, w_ukv, w_o):
    d = w_in.shape[0]
    w1 = w_in[:, :3 * SB_WIDTH].astype(BF16)
    lat = 3 * SB_WIDTH + MLA_Q_RANK + MLA_KV_RANK
    w2 = jnp.concatenate([w_in[:, 3 * SB_WIDTH:lat], jnp.zeros((d, ROPE_LANE0), F32), w_in[:, lat:],
                          jnp.zeros((d, LANES - ROPE_LANE0 - MLA_ROPE_DIM), F32)], axis=1).astype(BF16)
    uq = w_uq.reshape(MLA_Q_RANK, MLA_HEADS, MLA_NOPE_DIM + MLA_ROPE_DIM)
    uq = jnp.pad(uq, ((0, 0), (0, 0), (0, LANES - MLA_NOPE_DIM - MLA_ROPE_DIM)))
    wuq = uq.reshape(MLA_Q_RANK, MLA_HEADS * LANES).astype(BF16)
    ukv = w_ukv.reshape(MLA_KV_RANK, MLA_HEADS, MLA_NOPE_DIM + MLA_V_DIM)
    uk = jnp.pad(ukv[:, :, :MLA_NOPE_DIM], ((0, 0), (0, 0), (0, LANES - MLA_NOPE_DIM)))
    uv = ukv[:, :, MLA_NOPE_DIM:]
    even_head = (jnp.arange(MLA_HEADS) % 2 == 0)[None, :, None]
    uv = jnp.where(even_head, jnp.pad(uv, ((0, 0), (0, 0), (0, LANES - MLA_V_DIM))),
                   jnp.pad(uv, ((0, 0), (0, 0), (LANES - MLA_V_DIM, 0))))
    wukv = jnp.concatenate([uk.reshape(MLA_KV_RANK, MLA_HEADS * LANES),
                            uv.reshape(MLA_KV_RANK, MLA_HEADS * LANES)], axis=1).astype(BF16)
    return w1, w2, wuq, wukv, w_o[:SB_WIDTH].astype(BF16), w_o[SB_WIDTH:].astype(BF16)


def _router_weights(router_w, router_bias):
    d = router_w.shape[0]
    rw = router_w.reshape(d, N_GROUPS, EXPERTS_PER_GROUP).transpose(0, 2, 1).reshape(d, N_EXPERTS)
    hi = rw.astype(BF16)
    lo = (rw - hi.astype(F32)).astype(BF16)
    pad = jnp.zeros((d, LANES - N_EXPERTS), BF16)
    rwcat = jnp.concatenate([hi, pad, lo, pad], axis=1)
    rb = router_bias.reshape(N_GROUPS, EXPERTS_PER_GROUP).T.reshape(N_EXPERTS, 1)
    return rwcat, rb


def _dispatch_plan(counts, n_blocks):
    bm = DISPATCH_BLOCK
    gran = (counts + GRANULE - 1) // GRANULE
    slab_start = jnp.cumsum(gran, axis=1) - gran
    per_expert = jnp.sum(gran, axis=0) * GRANULE
    padded = (per_expert + bm - 1) // bm * bm
    pend = jnp.cumsum(padded)
    buf_start = (pend - padded) // GRANULE + jnp.cumsum(gran, axis=0) - gran
    tables = (gran.reshape(-1), slab_start.reshape(-1), buf_start.reshape(-1),
              jnp.sum(gran // 2, axis=1), jnp.sum(gran % 2, axis=1))
    tails = ((pend - padded + per_expert) // GRANULE, (padded - per_expert) // GRANULE)
    block_row = jnp.arange(n_blocks, dtype=jnp.int32) * bm
    block_e = jnp.minimum(jnp.sum(block_row[:, None] >= pend[None, :], axis=1), N_EXPERTS - 1)
    n_used = (pend[-1] // bm).reshape(1)
    as_i32 = lambda ts: tuple(t.astype(jnp.int32) for t in ts)
    return as_i32(tables), as_i32(tails), block_e.astype(jnp.int32), n_used.astype(jnp.int32)


def kernel(x, c, positions, ada_w, ada_b, w_in, q_norm, kv_norm, w_uq, w_ukv, w_o, ln1_g, ln1_b,
           router_w, router_bias, w_gate, w_up, w_down, ln2_g, ln2_b):
    batch, seq, d = x.shape
    n = batch * seq
    depth = ada_w.shape[0]
    assert seq % TOKEN_TILE == 0 and seq % ATT_TILE == 0

    mod_all = _ada_call(c, ada_w, ada_b).reshape(depth, batch, 6, d)
    cos_t, sin_t = _rope_call(positions)
    rwcat, rb = _router_weights(router_w, router_bias)
    t = ATT_TILE
    tri = (jnp.arange(t)[:, None] >= jnp.arange(t)[None, :]).astype(BF16)
    tri = jnp.concatenate([tri, tri], axis=0)
    tm = TOKEN_TILE
    ut = (jnp.arange(tm)[:, None] < jnp.arange(tm)[None, :]).astype(BF16)
    max_rows = n * TOP_K + (n // tm) * N_EXPERTS * (GRANULE - 1) + N_EXPERTS * (DISPATCH_BLOCK - 1)
    n_blocks = -(-max_rows // DISPATCH_BLOCK)

    x2d = x.reshape(n, d)
    for l in range(depth):
        mod = mod_all[l]
        w1, w2, wuq, wukv, wo_sb, wo_mla = _layer_weights(w_in[l], w_uq[l], w_ukv[l], w_o[l])
        sbq, sbk, sbv, mq, mk, mv = _proj_call(x2d, mod, cos_t, sin_t, w1, w2, q_norm[l].reshape(1, -1),
                                               kv_norm[l].reshape(1, -1), wuq, wukv, seq)
        sb_out = _sb_call(sbq, sbk, sbv, tri, batch, seq)
        mla_out = _mla_call(mq, mk, mv, batch, seq)
        x1, h2, route, gates, counts = _post_call(sb_out, mla_out, x2d, mod, wo_sb, wo_mla,
                                                  ln1_g[l].reshape(1, d), ln1_b[l].reshape(1, d),
                                                  rwcat, rb, ut, seq)
        tables, tails, block_e, n_used = _dispatch_plan(counts[:, :, 0], n_blocks)
        buf = _dispatch_call(tables, tails, n_used, route, h2, n_blocks * DISPATCH_BLOCK)
        y_buf = _ffn_call(block_e, n_used, buf, w_gate, w_up, w_down, l)
        x2d = _combine_call(tables, x1, gates, mod, ln2_g[l].reshape(1, d), ln2_b[l].reshape(1, d), y_buf, seq)
    return x2d.reshape(batch, seq, d)
```

```python
import jax
import jax.numpy as jnp
from jax import lax
from jax.experimental import pallas as pl
from jax.experimental.pallas import tpu as pltpu

F32 = jnp.float32
BF16 = jnp.bfloat16

D_MODEL = 1024
DEPTH = 2
SB_HEADS = 8
SB_HEAD_DIM = 64
SB_WIDTH = SB_HEADS * SB_HEAD_DIM
MLA_HEADS = 8
MLA_NOPE_DIM = 64
MLA_ROPE_DIM = 32
MLA_V_DIM = 64
MLA_Q_RANK = 256
MLA_KV_RANK = 128
MLA_WIDTH = MLA_HEADS * MLA_V_DIM
ROPE_BASE = 10000.0
N_EXPERTS = 32
N_GROUPS = 8
EXPERTS_PER_GROUP = 4
TOP_K = 2
D_EXPERT = 256
DISPATCH_BLOCK = 512
DEEPNORM_ALPHA = (2 * DEPTH) ** 0.25
LN_EPS = 1e-5
RMS_EPS = 1e-6
LOG2_E = 1.4426950408889634
UNDERFLOW_LOG2 = 160.0

LANES = 128
HALF_ROPE = MLA_ROPE_DIM // 2
ROPE_LANE0 = MLA_NOPE_DIM
VMEM_LIMIT = 56 * 1024 * 1024

TOKEN_TILE = 512
ATT_TILE = 256
GRANULE = 16
GRANULE_SHIFT = 4
MLA_STEP_TILES = 9
SB_STEP_QUERY_TILES = 4
MXU_DIM = 256
SLAB_ROWS = -(-(TOKEN_TILE * TOP_K + N_EXPERTS * (GRANULE - 1)) // MXU_DIM) * MXU_DIM


def _dot(a, b):
    return jnp.dot(a, b, preferred_element_type=F32)


def _dot_nt(a, b):
    return lax.dot_general(a, b, (((1,), (1,)), ((), ())), preferred_element_type=F32)


def _split_bf16(v):
    hi = v.astype(BF16)
    lo = (v - hi.astype(F32)).astype(BF16)
    return hi, lo


def _standardize(x):
    mu = jnp.mean(x, axis=-1, keepdims=True)
    xc = x - mu
    var = jnp.mean(xc * xc, axis=-1, keepdims=True)
    return xc * lax.rsqrt(var + LN_EPS)


def _ada_kernel(c_ref, w_ref, b_ref, o_ref):
    c = c_ref[...]
    ca = c * jax.nn.sigmoid(c)
    ca_hi, ca_lo = _split_bf16(ca)
    w_hi, w_lo = _split_bf16(w_ref[...])
    o_ref[...] = _dot(ca_hi, w_hi) + _dot(ca_lo, w_hi) + _dot(ca_hi, w_lo) + b_ref[...]


def _ada_call(c, ada_w, ada_b):
    depth, d, n6 = ada_w.shape
    b = c.shape[0]
    tn = 1536
    return pl.pallas_call(
        _ada_kernel,
        grid=(depth, n6 // tn),
        in_specs=[pl.BlockSpec((b, d), lambda l, j: (0, 0)),
                  pl.BlockSpec((None, d, tn), lambda l, j: (l, 0, j)),
                  pl.BlockSpec((None, 1, tn), lambda l, j: (l, 0, j))],
        out_specs=pl.BlockSpec((None, b, tn), lambda l, j: (l, 0, j)),
        out_shape=jax.ShapeDtypeStruct((depth, b, n6), F32),
        compiler_params=pltpu.CompilerParams(vmem_limit_bytes=VMEM_LIMIT),
        name="ada",
    )(c, ada_w, ada_b.reshape(depth, 1, n6))


def _rope_kernel(pos_ref, invf_ref, cos_ref, sin_ref):
    ang = pos_ref[...].astype(F32) * invf_ref[...]
    lane = lax.broadcasted_iota(jnp.int32, ang.shape, 1)
    rot = (lane >= ROPE_LANE0) & (lane < ROPE_LANE0 + MLA_ROPE_DIM)
    cos_ref[...] = jnp.where(rot, jnp.cos(ang), 1.0)
    sin_ref[...] = jnp.where(rot, jnp.sin(ang), 0.0)


def _rope_call(positions):
    n = positions.size
    tm = TOKEN_TILE
    inv_freq = ROPE_BASE ** (-jnp.arange(0, MLA_ROPE_DIM, 2, dtype=F32) / MLA_ROPE_DIM)
    invf = jnp.zeros((1, LANES), F32)
    invf = invf.at[0, ROPE_LANE0:ROPE_LANE0 + HALF_ROPE].set(inv_freq)
    invf = invf.at[0, ROPE_LANE0 + HALF_ROPE:ROPE_LANE0 + MLA_ROPE_DIM].set(inv_freq)
    return pl.pallas_call(
        _rope_kernel,
        grid=(n // tm,),
        in_specs=[pl.BlockSpec((tm, 1), lambda i: (i, 0)),
                  pl.BlockSpec((1, LANES), lambda i: (0, 0))],
        out_specs=[pl.BlockSpec((tm, LANES), lambda i: (i, 0)),
                   pl.BlockSpec((tm, LANES), lambda i: (i, 0))],
        out_shape=[jax.ShapeDtypeStruct((n, LANES), F32)] * 2,
        name="rope_tables",
    )(positions.reshape(n, 1), invf)


def _proj_kernel(x_ref, mod_ref, cos_ref, sin_ref, w1_ref, w2_ref, qn_ref, kvn_ref, wuq_ref,
                 wukv_ref, sbq_ref, sbk_ref, sbv_ref, mq_ref, mk_ref, mv_ref):
    mod = mod_ref[...]
    h = _standardize(x_ref[...]) * (1.0 + mod[1:2]) + mod[0:1]
    hb = h.astype(BF16)

    p1 = _dot(hb, w1_ref[...])
    sbq_ref[...] = (p1[:, :SB_WIDTH] * (SB_HEAD_DIM ** -0.5 * LOG2_E)).astype(BF16)
    sbk_ref[...] = p1[:, SB_WIDTH:2 * SB_WIDTH].astype(BF16)
    sbv_ref[...] = p1[:, 2 * SB_WIDTH:].astype(BF16)

    p2 = _dot(hb, w2_ref[...])
    q_lat = p2[:, :MLA_Q_RANK]
    kv_lat = p2[:, MLA_Q_RANK:MLA_Q_RANK + MLA_KV_RANK]
    k_rope = p2[:, MLA_Q_RANK + MLA_KV_RANK:]

    qn = q_lat * lax.rsqrt(jnp.mean(q_lat * q_lat, axis=-1, keepdims=True) + RMS_EPS) * qn_ref[...]
    kvn = kv_lat * lax.rsqrt(jnp.mean(kv_lat * kv_lat, axis=-1, keepdims=True) + RMS_EPS) * kvn_ref[...]
    q = _dot(qn.astype(BF16), wuq_ref[...])
    kv = _dot(kvn.astype(BF16), wukv_ref[...])

    cos = cos_ref[...]
    sin = sin_ref[...]
    lane = lax.broadcasted_iota(jnp.int32, cos.shape, 1)
    second = lane >= ROPE_LANE0 + HALF_ROPE
    sin_up = jnp.where(second, sin, 0.0)
    sin_dn = jnp.where(second, 0.0, -sin)

    def rope(t):
        return (t * cos + pltpu.roll(t, HALF_ROPE, 1) * sin_up
                + pltpu.roll(t, LANES - HALF_ROPE, 1) * sin_dn)

    kr = rope(k_rope)
    mla_scale = (MLA_NOPE_DIM + MLA_ROPE_DIM) ** -0.5 * LOG2_E
    for hd in range(MLA_HEADS):
        sl = slice(hd * LANES, (hd + 1) * LANES)
        mq_ref[:, sl] = (rope(q[:, sl]) * mla_scale).astype(BF16)
        mk_ref[:, sl] = (kv[:, sl] + kr).astype(BF16)
    vlane = lax.broadcasted_iota(jnp.int32, (1, MLA_HEADS * LANES), 1)
    quarter = lax.shift_right_logical(vlane, 6) & 3
    ones = jnp.where((quarter == 1) | (quarter == 2), 1.0, 0.0)
    mv_ref[...] = (kv[:, MLA_HEADS * LANES:] + ones).astype(BF16)


def _proj_call(x2d, mod, cos_t, sin_t, w1, w2, qn, kvn, wuq, wukv, seq):
    n, d = x2d.shape
    tm = TOKEN_TILE
    per_b = seq // tm
    tok = lambda i: (i, 0)
    full = lambda i: (0, 0)
    widths = (SB_WIDTH, SB_WIDTH, SB_WIDTH, MLA_HEADS * LANES, MLA_HEADS * LANES, MLA_HEADS * LANES)
    return pl.pallas_call(
        _proj_kernel,
        grid=(n // tm,),
        in_specs=[pl.BlockSpec((tm, d), tok),
                  pl.BlockSpec((None, 6, d), lambda i: (i // per_b, 0, 0)),
                  pl.BlockSpec((tm, LANES), tok),
                  pl.BlockSpec((tm, LANES), tok),
                  pl.BlockSpec(w1.shape, full),
                  pl.BlockSpec(w2.shape, full),
                  pl.BlockSpec(qn.shape, full),
                  pl.BlockSpec(kvn.shape, full),
                  pl.BlockSpec(wuq.shape, full),
                  pl.BlockSpec(wukv.shape, full)],
        out_specs=[pl.BlockSpec((tm, w), tok) for w in widths],
        out_shape=[jax.ShapeDtypeStruct((n, w), BF16) for w in widths],
        compiler_params=pltpu.CompilerParams(vmem_limit_bytes=VMEM_LIMIT),
        name="proj",
    )(x2d, mod, cos_t, sin_t, w1, w2, qn, kvn, wuq, wukv)


def _sb_kernel(q_ref, k_ref, v_ref, tri_ref, o_ref, carry_ref, acc_ref):
    t = ATT_TILE
    nq = q_ref.shape[0] // t
    tri = tri_ref[...]
    lane = lax.broadcasted_iota(jnp.int32, (1, LANES), 1)
    row = lax.broadcasted_iota(jnp.int32, (t, t), 0)
    col = lax.broadcasted_iota(jnp.int32, (t, t), 1)
    strict = col < row
    mine = [lane < SB_HEAD_DIM, lane >= SB_HEAD_DIM]

    def head_queries(rows):
        q = q_ref[rows, :]
        return [jnp.where(m, q, jnp.zeros_like(q)) for m in mine]

    def step(items, states, queries):
        chains = [(qk, j, diag, h) for qk, j, diag in items for h in range(2)]
        ks = [k_ref[pl.ds(j * t, t), :] for _, j, _ in items]
        zs = [_dot_nt(queries[qk][h], ks[n]) for n, (qk, _, _) in enumerate(items) for h in range(2)]
        nlks, pieces = [], []
        for (_, _, diag, _), z in zip(chains, zs):
            nlk = jnp.maximum(z, 0.0) + jnp.log(1.0 + jnp.exp2(-jnp.abs(z))) * LOG2_E
            if diag:
                nlk = jnp.where(strict, nlk, 0.0)
            nlks.append(nlk)
            pieces.append(nlk.astype(BF16))
        laters = [_dot(p, tri) for p in pieces]
        ncarry = {key: st[0] for key, st in states.items()}
        ws = []
        for (qk, j, diag, h), z, nlk, later in zip(chains, zs, nlks, laters):
            w = jnp.exp2((z - ncarry[(qk, h)]) - later)
            if diag:
                w = jnp.where(strict, w, 0.0)
            ws.append(w.astype(BF16))
            ncarry[(qk, h)] = ncarry[(qk, h)] + jnp.sum(nlk, axis=1, keepdims=True)
        acc = {key: st[1] for key, st in states.items()}
        for (qk, j, _, h), w in zip(chains, ws):
            acc[(qk, h)] = acc[(qk, h)] + _dot(w, v_ref[pl.ds(j * t, t), :])
        for key in states:
            states[key] = (ncarry[key], acc[key])

    def live(state):
        return jnp.minimum(jnp.min(state[0][0]), jnp.min(state[1][0])) < UNDERFLOW_LOG2

    def merged(state):
        return jnp.where(mine[0], state[0][1], state[1][1]).astype(o_ref.dtype)

    for first in range(0, nq, SB_STEP_QUERY_TILES):
        tiles = list(range(first, min(first + SB_STEP_QUERY_TILES, nq)))
        queries = {qi: head_queries(pl.ds(qi * t, t)) for qi in tiles}
        states = {(qi, h): (jnp.zeros((t, 1), F32), jnp.zeros((t, LANES), F32)) for qi in tiles for h in range(2)}
        items = [(qi, qi - d, d == 0) for qi in tiles for d in range(min(qi, 1) + 1)]
        step(items, states, queries)
        for qi in tiles:
            rows = pl.ds(qi * t, t)
            o_ref[rows, :] = merged((states[(qi, 0)], states[(qi, 1)]))
            for h in range(2):
                carry_ref[h, rows, :] = jnp.broadcast_to(states[(qi, h)][0], (t, LANES))
                acc_ref[h, rows, :] = states[(qi, h)][1]

    def finish(i, carry):
        rows = pl.ds(pl.multiple_of(i * t, t), t)
        ncarry = [carry_ref[h, rows, 0:1] for h in range(2)]

        @pl.when(live(((ncarry[0],), (ncarry[1],))))
        def _():
            queries = {0: head_queries(rows)}

            def dynamic_step(js, state):
                states = {(0, h): state[h] for h in range(2)}
                step([(0, j, False) for j in js], states, queries)
                return tuple(states[(0, h)] for h in range(2))

            def more(c):
                return (c[0] >= 1) & live(c[1])

            def pair(c):
                return c[0] - 2, dynamic_step((c[0], c[0] - 1), c[1])

            state = tuple((ncarry[h], acc_ref[h, rows, :]) for h in range(2))
            j, state = lax.while_loop(more, pair, (i - 2, state))
            state = lax.cond((j == 0) & live(state), lambda s: dynamic_step((0,), s), lambda s: s, state)
            o_ref[rows, :] = merged(state)

        return carry

    lax.fori_loop(2, nq, finish, 0)


def _sb_call(q, k, v, tri, batch, seq):
    t = ATT_TILE
    pairs = SB_WIDTH // LANES
    seq_spec = pl.BlockSpec((seq, LANES), lambda b, p: (b, p))
    return pl.pallas_call(
        _sb_kernel,
        grid=(batch, pairs),
        in_specs=[seq_spec, seq_spec, seq_spec,
                  pl.BlockSpec((t, t), lambda b, p: (0, 0))],
        out_specs=seq_spec,
        out_shape=jax.ShapeDtypeStruct(q.shape, BF16),
        scratch_shapes=[pltpu.VMEM((2, seq, LANES), F32), pltpu.VMEM((2, seq, LANES), F32)],
        compiler_params=pltpu.CompilerParams(vmem_limit_bytes=VMEM_LIMIT),
        name="sb_attention",
    )(q, k, v, tri)


def _mla_kernel(q_ref, k_ref, v_ref, o_ref):
    t = ATT_TILE
    lane = lax.broadcasted_iota(jnp.int32, (1, LANES), 1)
    row = lax.broadcasted_iota(jnp.int32, (t, t), 0)
    col = lax.broadcasted_iota(jnp.int32, (t, t), 1)
    causal = col <= row

    def head_lanes(hd):
        return slice(hd * LANES, (hd + 1) * LANES)

    def tile_rows(i):
        return slice(i * t, (i + 1) * t)

    def step(items, states):
        chains = [(qi, j, diag, hd) for qi, j, diag in items for hd in range(2)]
        ss = []
        for qi, j, diag, hd in chains:
            s = _dot_nt(q_ref[tile_rows(qi), head_lanes(hd)], k_ref[tile_rows(j), head_lanes(hd)])
            ss.append(jnp.where(causal, s, -jnp.inf) if diag else s)
        ps = [None] * len(chains)
        for key in sorted({(qi, hd) for qi, _, _, hd in chains}):
            mine_n = [n for n, (qi, _, _, hd) in enumerate(chains) if (qi, hd) == key]
            m, acc = states[key]
            m_new = m
            for n in mine_n:
                m_new = jnp.maximum(m_new, jnp.max(ss[n], axis=1, keepdims=True))
            for n in mine_n:
                ps[n] = jnp.exp2(ss[n] - m_new).astype(BF16)
            states[key] = (m_new, jnp.exp2(m - m_new) * acc)
        for n, (qi, j, _, hd) in enumerate(chains):
            m, acc = states[(qi, hd)]
            states[(qi, hd)] = (m, acc + _dot(ps[n], v_ref[tile_rows(j), head_lanes(hd)]))

    nq = q_ref.shape[0] // t
    groups = [(lo, nq - 1 - lo) for lo in range(nq // 2)] + ([(nq // 2,)] if nq % 2 else [])
    for group in groups:
        items = [(qi, qi - d, d == 0) for qi in reversed(group) for d in range(qi + 1)]
        states = {(qi, hd): (jnp.full((t, 1), -jnp.inf, F32), jnp.zeros((t, LANES), F32))
                  for qi in group for hd in range(2)}
        for first in range(0, len(items), MLA_STEP_TILES):
            step(items[first:first + MLA_STEP_TILES], states)
        for qi in group:
            acc0, acc1 = states[(qi, 0)][1], states[(qi, 1)][1]
            out = jnp.where(lane < MLA_V_DIM, acc0 / pltpu.roll(acc0, MLA_V_DIM, 1),
                            acc1 / pltpu.roll(acc1, MLA_V_DIM, 1))
            o_ref[tile_rows(qi), :] = out.astype(o_ref.dtype)


def _mla_call(q, k, v, batch, seq):
    pairs = MLA_WIDTH // LANES
    return pl.pallas_call(
        _mla_kernel,
        grid=(batch, pairs),
        in_specs=[pl.BlockSpec((seq, 2 * LANES), lambda b, p: (b, p)),
                  pl.BlockSpec((seq, 2 * LANES), lambda b, p: (b, p)),
                  pl.BlockSpec((seq, 2 * LANES), lambda b, p: (b, p))],
        out_specs=pl.BlockSpec((seq, LANES), lambda b, p: (b, p)),
        out_shape=jax.ShapeDtypeStruct((q.shape[0], MLA_WIDTH), BF16),
        compiler_params=pltpu.CompilerParams(vmem_limit_bytes=VMEM_LIMIT),
        name="mla_attention",
    )(q, k, v)


def _top2_sum(a, b, c, d):
    hi1, lo1 = jnp.maximum(a, b), jnp.minimum(a, b)
    hi2, lo2 = jnp.maximum(c, d), jnp.minimum(c, d)
    return jnp.maximum(hi1, hi2) + jnp.maximum(jnp.minimum(hi1, hi2), jnp.maximum(lo1, lo2))


def _first_argmax4(v):
    m = jnp.maximum(jnp.maximum(v[0], v[1]), jnp.maximum(v[2], v[3]))
    return jnp.where(v[0] == m, 0, jnp.where(v[1] == m, 1, jnp.where(v[2] == m, 2, 3)))


def _pick4(idx, v):
    return jnp.where(idx == 0, v[0], jnp.where(idx == 1, v[1], jnp.where(idx == 2, v[2], v[3])))


def _post_kernel(sb_ref, mla_ref, x_ref, mod_ref, wo_sb_ref, wo_mla_ref, g_ref, b_ref, rw_ref,
                 rb_ref, ut_ref, x1_ref, h2_ref, route_ref, gate_ref, cnt_ref):
    tm = x_ref.shape[0]
    mod = mod_ref[...]
    mix = _dot(sb_ref[...], wo_sb_ref[...]) + _dot(mla_ref[...], wo_mla_ref[...])
    x1 = _standardize(DEEPNORM_ALPHA * x_ref[...] + (1.0 + mod[2:3]) * mix) * g_ref[...] + b_ref[...]
    x1_ref[...] = x1
    h2 = _standardize(x1) * (1.0 + mod[4:5]) + mod[3:4]

    h_hi, h_lo = _split_bf16(h2)
    h2_ref[...] = h_hi
    rw = rw_ref[...]
    big = _dot(h_hi, rw)
    logits = big[:, :LANES] + big[:, LANES:] + _dot(h_lo, rw[:, :LANES])
    lt = logits.T[:N_EXPERTS]
    scores = jax.nn.sigmoid(lt)
    biased = scores + rb_ref[...]
    sc = [scores[N_GROUPS * p:N_GROUPS * (p + 1)] for p in range(EXPERTS_PER_GROUP)]
    bi = [biased[N_GROUPS * p:N_GROUPS * (p + 1)] for p in range(EXPERTS_PER_GROUP)]

    group_score = _top2_sum(*bi)
    gidx = lax.broadcasted_iota(jnp.int32, group_score.shape, 0)
    best = jnp.max(group_score, axis=0, keepdims=True)
    g_sel = jnp.min(jnp.where(group_score == best, gidx, N_GROUPS), axis=0, keepdims=True)
    in_sel = gidx == g_sel
    vb = [jnp.sum(jnp.where(in_sel, b, 0.0), axis=0, keepdims=True) for b in bi]
    vs = [jnp.sum(jnp.where(in_sel, s, 0.0), axis=0, keepdims=True) for s in sc]
    l1 = _first_argmax4(vb)
    vb2 = [jnp.where(l1 == p, -jnp.inf, vb[p]) for p in range(EXPERTS_PER_GROUP)]
    l2 = _first_argmax4(vb2)
    s1 = _pick4(l1, vs)
    s2 = _pick4(l2, vs)
    tot = s1 + s2
    e1 = g_sel * EXPERTS_PER_GROUP + l1
    e2 = g_sel * EXPERTS_PER_GROUP + l2

    eidx = lax.broadcasted_iota(jnp.int32, (N_EXPERTS, tm), 0)
    hit1 = eidx == e1
    hit2 = eidx == e2
    onehot = jnp.where(hit1 | hit2, 1.0, 0.0)
    before = _dot(onehot.astype(BF16), ut_ref[...])
    count = jnp.sum(onehot, axis=1, keepdims=True).astype(jnp.int32)
    granules = lax.shift_right_logical(count + (GRANULE - 1), GRANULE_SHIFT)
    lower = (lax.broadcasted_iota(jnp.int32, (N_EXPERTS, N_EXPERTS), 1)
             < lax.broadcasted_iota(jnp.int32, (N_EXPERTS, N_EXPERTS), 0))
    gran_f = jnp.broadcast_to(granules.astype(F32), (N_EXPERTS, LANES)).astype(BF16)
    start = _dot(jnp.where(lower, 1.0, 0.0).astype(BF16), gran_f)[:, 0:1] * GRANULE
    row_of = before + start
    p1 = jnp.sum(jnp.where(hit1, row_of, 0.0), axis=0, keepdims=True)
    p2 = jnp.sum(jnp.where(hit2, row_of, 0.0), axis=0, keepdims=True)
    cnt_ref[...] = jnp.broadcast_to(count, cnt_ref.shape)

    r8 = lax.broadcasted_iota(jnp.int32, (8, tm), 0)
    route_ref[...] = jnp.where(r8 == 0, p1.astype(jnp.int32), jnp.where(r8 == 1, p2.astype(jnp.int32), 0))
    r128 = lax.broadcasted_iota(jnp.int32, (LANES, tm), 0)
    cols = jnp.where(r128 == 0, s1 / tot, jnp.where(r128 == 1, s2 / tot,
                                                     jnp.where(r128 == 2, p1, jnp.where(r128 == 3, p2, 0.0))))
    gate_ref[...] = cols.T


def _post_call(sb_out, mla_out, x2d, mod, wo_sb, wo_mla, g, b, rw, rb, ut, seq):
    n, d = x2d.shape
    tm = TOKEN_TILE
    per_b = seq // tm
    tok = lambda i: (i, 0)
    full = lambda i: (0, 0)
    return pl.pallas_call(
        _post_kernel,
        grid=(n // tm,),
        in_specs=[pl.BlockSpec((tm, SB_WIDTH), tok),
                  pl.BlockSpec((tm, MLA_WIDTH), tok),
                  pl.BlockSpec((tm, d), tok),
                  pl.BlockSpec((None, 6, d), lambda i: (i // per_b, 0, 0)),
                  pl.BlockSpec(wo_sb.shape, full),
                  pl.BlockSpec(wo_mla.shape, full),
                  pl.BlockSpec((1, d), full),
                  pl.BlockSpec((1, d), full),
                  pl.BlockSpec(rw.shape, full),
                  pl.BlockSpec(rb.shape, full),
                  pl.BlockSpec(ut.shape, full)],
        out_specs=[pl.BlockSpec((tm, d), tok),
                   pl.BlockSpec((tm, d), tok),
                   pl.BlockSpec((8, tm), lambda i: (0, i)),
                   pl.BlockSpec((tm, LANES), tok),
                   pl.BlockSpec((None, N_EXPERTS, LANES), lambda i: (i, 0, 0))],
        out_shape=[jax.ShapeDtypeStruct((n, d), F32),
                   jax.ShapeDtypeStruct((n, d), BF16),
                   jax.ShapeDtypeStruct((8, n), jnp.int32),
                   jax.ShapeDtypeStruct((n, LANES), F32),
                   jax.ShapeDtypeStruct((n // tm, N_EXPERTS, LANES), jnp.int32)],
        compiler_params=pltpu.CompilerParams(vmem_limit_bytes=VMEM_LIMIT),
        name="post_attention",
    )(sb_out, mla_out, x2d, mod, wo_sb, wo_mla, g, b, rw, rb, ut)


def _granule_copy(src_ref, src_g, dst_ref, dst_g, sem, granules=1):
    rows = granules * GRANULE
    src = src_ref.at[pl.ds(pl.multiple_of(src_g * GRANULE, GRANULE), rows), :]
    dst = dst_ref.at[pl.ds(pl.multiple_of(dst_g * GRANULE, GRANULE), rows), :]
    return pltpu.make_async_copy(src, dst, sem)


def _start_segment(count, copy):
    def pair(p, carry):
        copy(2 * p, 2, 0).start()
        return carry

    lax.fori_loop(0, lax.shift_right_logical(count, 1), pair, 0)

    @pl.when((count & 1) == 1)
    def _():
        copy(count - 1, 1, 1).start()


def _wait_copies(count, copy):
    def wait(g, carry):
        copy().wait()
        return carry

    lax.fori_loop(0, count, wait, 0)


def _block_copy(zero_ref, buf_ref, block, sem):
    dst = buf_ref.at[pl.ds(pl.multiple_of(block * DISPATCH_BLOCK, DISPATCH_BLOCK), DISPATCH_BLOCK), :]
    return pltpu.make_async_copy(zero_ref, dst, sem)


def _dispatch_kernel(ng_ref, ls_ref, gs_ref, pairs_ref, singles_ref, tail_start_ref, tail_n_ref, n_used_ref,
                     route_ref, h_ref, buf_ref, slab_ref, zero_ref, sem):
    c = pl.program_id(0)
    last = pl.num_programs(0) - 1
    slot = c % 2
    tm = h_ref.shape[0]
    rows = slab_ref.shape[1]

    def wait_slab(tile, s):
        _wait_copies(pairs_ref[tile], lambda: _granule_copy(slab_ref.at[s], 0, buf_ref, 0, sem.at[2 * s], 2))
        _wait_copies(singles_ref[tile], lambda: _granule_copy(slab_ref.at[s], 0, buf_ref, 0, sem.at[2 * s + 1]))

    @pl.when(c == 0)
    def _():
        zero_ref[...] = jnp.zeros_like(zero_ref)

        def expert_tail(e, total):
            def granule(g, inner):
                _granule_copy(zero_ref, 0, buf_ref, tail_start_ref[e] + g, sem.at[4]).start()
                return inner

            lax.fori_loop(0, tail_n_ref[e], granule, 0)
            return total + tail_n_ref[e]

        n_tail = lax.fori_loop(0, N_EXPERTS, expert_tail, 0)
        n_blocks = buf_ref.shape[0] // DISPATCH_BLOCK

        def unused_block(b, carry):
            _block_copy(zero_ref, buf_ref, b, sem.at[5]).start()
            return carry

        lax.fori_loop(n_used_ref[0], n_blocks, unused_block, 0)
        _wait_copies(n_tail, lambda: _granule_copy(zero_ref, 0, buf_ref, 0, sem.at[4]))
        _wait_copies(n_blocks - n_used_ref[0], lambda: _block_copy(zero_ref, buf_ref, 0, sem.at[5]))

    @pl.when(c >= 2)
    def _():
        wait_slab(c - 2, slot)

    row = lax.broadcasted_iota(jnp.int32, (rows, tm), 0)
    route = route_ref[...]
    onehot = (row == route[0:1]) | (row == route[1:2])
    slab_ref[slot] = _dot(jnp.where(onehot, 1.0, 0.0).astype(BF16), h_ref[...]).astype(BF16)

    def expert(e, carry):
        idx = c * N_EXPERTS + e
        ls = ls_ref[idx]
        gs = gs_ref[idx]

        _start_segment(ng_ref[idx], lambda g, n, k: _granule_copy(
            slab_ref.at[slot], ls + g, buf_ref, gs + g, sem.at[2 * slot + k], n))
        return carry

    lax.fori_loop(0, N_EXPERTS, expert, 0)

    @pl.when(c == last)
    def _():
        wait_slab(c, slot)

        @pl.when(c >= 1)
        def _():
            wait_slab(c - 1, 1 - slot)


def _dispatch_call(tables, tails, n_used, route, h2, buf_rows):
    n, d = h2.shape
    tm = TOKEN_TILE
    grid_spec = pltpu.PrefetchScalarGridSpec(
        num_scalar_prefetch=8,
        grid=(n // tm,),
        in_specs=[pl.BlockSpec((8, tm), lambda i, *_: (0, i)),
                  pl.BlockSpec((tm, d), lambda i, *_: (i, 0))],
        out_specs=pl.BlockSpec(memory_space=pl.ANY),
        scratch_shapes=[pltpu.VMEM((2, SLAB_ROWS, d), BF16), pltpu.VMEM((DISPATCH_BLOCK, d), BF16),
                        pltpu.SemaphoreType.DMA((6,))],
    )
    return pl.pallas_call(
        _dispatch_kernel,
        grid_spec=grid_spec,
        out_shape=jax.ShapeDtypeStruct((buf_rows, d), BF16),
        compiler_params=pltpu.CompilerParams(dimension_semantics=("arbitrary",),
                                             vmem_limit_bytes=VMEM_LIMIT),
        name="dispatch",
    )(*tables, *tails, n_used, route, h2)


def _ffn_kernel(be_ref, nb_ref, x_ref, wg_ref, wu_ref, wd_ref, y_ref, wg_s, wu_s, wd_s):
    i = pl.program_id(0)
    prev = be_ref[jnp.maximum(i - 1, 0)]

    @pl.when((i == 0) | (be_ref[i] != prev))
    def _():
        wg_s[...] = wg_ref[...].astype(BF16)
        wu_s[...] = wu_ref[...].astype(BF16)
        wd_s[...] = wd_ref[...].astype(BF16)

    @pl.when(i < nb_ref[0])
    def _():
        xb = x_ref[...]
        g = _dot(xb, wg_s[...])
        u = _dot(xb, wu_s[...])
        a = g * jax.nn.sigmoid(g) * u
        y_ref[...] = _dot(a.astype(BF16), wd_s[...]).astype(y_ref.dtype)

    @pl.when(i >= nb_ref[0])
    def _():
        y_ref[...] = jnp.zeros_like(y_ref)


def _ffn_call(block_e, n_used, buf, w_gate, w_up, w_down, layer):
    rows, d = buf.shape
    bm = DISPATCH_BLOCK
    de = w_gate.shape[-1]
    grid_spec = pltpu.PrefetchScalarGridSpec(
        num_scalar_prefetch=2,
        grid=(rows // bm,),
        in_specs=[pl.BlockSpec((bm, d), lambda i, be, nb: (jnp.minimum(i, nb[0] - 1), 0)),
                  pl.BlockSpec((None, None, d, de), lambda i, be, nb: (layer, be[i], 0, 0)),
                  pl.BlockSpec((None, None, d, de), lambda i, be, nb: (layer, be[i], 0, 0)),
                  pl.BlockSpec((None, None, de, d), lambda i, be, nb: (layer, be[i], 0, 0))],
        out_specs=pl.BlockSpec((bm, d), lambda i, be, nb: (i, 0)),
        scratch_shapes=[pltpu.VMEM((d, de), BF16), pltpu.VMEM((d, de), BF16), pltpu.VMEM((de, d), BF16)],
    )
    return pl.pallas_call(
        _ffn_kernel,
        grid_spec=grid_spec,
        out_shape=jax.ShapeDtypeStruct((rows, d), BF16),
        compiler_params=pltpu.CompilerParams(dimension_semantics=("arbitrary",),
                                             vmem_limit_bytes=VMEM_LIMIT),
        name="expert_ffn",
    )(block_e, n_used, buf, w_gate, w_up, w_down)


def _combine_kernel(ng_ref, ls_ref, gs_ref, pairs_ref, singles_ref, x1_ref, gate_ref, mod_ref, g_ref, b_ref,
                    y_hbm_ref, o_ref, slab_ref, ffn_ref, sem):
    c = pl.program_id(0)
    last = pl.num_programs(0) - 1
    slot = c % 2
    tm = x1_ref.shape[0]
    rows = slab_ref.shape[1]

    def fetch(tile, into, first_expert, n_experts):
        def expert(e, carry):
            idx = tile * N_EXPERTS + e
            ls = ls_ref[idx]
            gs = gs_ref[idx]
            _start_segment(ng_ref[idx], lambda g, n, k: _granule_copy(
                y_hbm_ref, gs + g, slab_ref.at[into], ls + g, sem.at[2 * into + k], n))
            return carry

        lax.fori_loop(first_expert, first_expert + n_experts, expert, 0)

    def prefetch_next(part):
        @pl.when(c < last)
        def _():
            fetch(c + 1, 1 - slot, part * (N_EXPERTS // 4), N_EXPERTS // 4)

    @pl.when(c == 0)
    def _():
        slab_ref[...] = jnp.zeros_like(slab_ref)
        fetch(0, 0, 0, N_EXPERTS)

    _wait_copies(pairs_ref[c], lambda: _granule_copy(y_hbm_ref, 0, slab_ref.at[slot], 0, sem.at[2 * slot], 2))
    _wait_copies(singles_ref[c], lambda: _granule_copy(y_hbm_ref, 0, slab_ref.at[slot], 0, sem.at[2 * slot + 1]))

    prefetch_next(0)
    cols = gate_ref[...]
    lane = lax.broadcasted_iota(jnp.int32, (tm, rows), 1)
    ys = slab_ref[slot]
    pick1 = jnp.where(lane == cols[:, 2:3].astype(jnp.int32), 1.0, 0.0).astype(BF16)
    ffn_ref[...] = cols[:, 0:1] * _dot(pick1, ys)
    prefetch_next(1)
    pick2 = jnp.where(lane == cols[:, 3:4].astype(jnp.int32), 1.0, 0.0).astype(BF16)
    ffn_ref[...] += cols[:, 1:2] * _dot(pick2, slab_ref[slot])
    prefetch_next(2)
    mod = mod_ref[...]
    o_ref[...] = (_standardize(DEEPNORM_ALPHA * x1_ref[...] + (1.0 + mod[5:6]) * ffn_ref[...]) * g_ref[...]
                  + b_ref[...])
    prefetch_next(3)


def _combine_call(tables, x1, gates, mod, g, b, y_buf, seq):
    n, d = x1.shape
    tm = TOKEN_TILE
    per_b = seq // tm
    tok = lambda i, *_: (i, 0)
    full = lambda i, *_: (0, 0)
    grid_spec = pltpu.PrefetchScalarGridSpec(
        num_scalar_prefetch=5,
        grid=(n // tm,),
        in_specs=[pl.BlockSpec((tm, d), tok),
                  pl.BlockSpec((tm, LANES), tok),
                  pl.BlockSpec((None, 6, d), lambda i, *_: (i // per_b, 0, 0)),
                  pl.BlockSpec((1, d), full),
                  pl.BlockSpec((1, d), full),
                  pl.BlockSpec(memory_space=pl.ANY)],
        out_specs=pl.BlockSpec((tm, d), tok),
        scratch_shapes=[pltpu.VMEM((2, SLAB_ROWS, d), BF16), pltpu.VMEM((tm, d), F32),
                        pltpu.SemaphoreType.DMA((4,))],
    )
    return pl.pallas_call(
        _combine_kernel,
        grid_spec=grid_spec,
        out_shape=jax.ShapeDtypeStruct((n, d), F32),
        compiler_params=pltpu.CompilerParams(dimension_semantics=("arbitrary",),
                                             vmem_limit_bytes=VMEM_LIMIT),
        name="combine",
    )(*tables, x1, gates, mod, g, b, y_buf)


def _layer_weights(w_in, w_uq, w_ukv, w_o):
    d = w_in.shape[0]
    w1 = w_in[:, :3 * SB_WIDTH].astype(BF16)
    lat = 3 * SB_WIDTH + MLA_Q_RANK + MLA_KV_RANK
    w2 = jnp.concatenate([w_in[:, 3 * SB_WIDTH:lat], jnp.zeros((d, ROPE_LANE0), F32), w_in[:, lat:],
                          jnp.zeros((d, LANES - ROPE_LANE0 - MLA_ROPE_DIM), F32)], axis=1).astype(BF16)
    uq = w_uq.reshape(MLA_Q_RANK, MLA_HEADS, MLA_NOPE_DIM + MLA_ROPE_DIM)
    uq = jnp.pad(uq, ((0, 0), (0, 0), (0, LANES - MLA_NOPE_DIM - MLA_ROPE_DIM)))
    wuq = uq.reshape(MLA_Q_RANK, MLA_HEADS * LANES).astype(BF16)
    ukv = w_ukv.reshape(MLA_KV_RANK, MLA_HEADS, MLA_NOPE_DIM + MLA_V_DIM)
    uk = jnp.pad(ukv[:, :, :MLA_NOPE_DIM], ((0, 0), (0, 0), (0, LANES - MLA_NOPE_DIM)))
    uv = ukv[:, :, MLA_NOPE_DIM:]
    even_head = (jnp.arange(MLA_HEADS) % 2 == 0)[None, :, None]
    uv = jnp.where(even_head, jnp.pad(uv, ((0, 0), (0, 0), (0, LANES - MLA_V_DIM))),
                   jnp.pad(uv, ((0, 0), (0, 0), (LANES - MLA_V_DIM, 0))))
    wukv = jnp.concatenate([uk.reshape(MLA_KV_RANK, MLA_HEADS * LANES),
                            uv.reshape(MLA_KV_RANK, MLA_HEADS * LANES)], axis=1).astype(BF16)
    return w1, w2, wuq, wukv, w_o[:SB_WIDTH].astype(BF16), w_o[SB_WIDTH:].astype(BF16)


def _router_weights(router_w, router_bias):
    d = router_w.shape[0]
    rw = router_w.reshape(d, N_GROUPS, EXPERTS_PER_GROUP).transpose(0, 2, 1).reshape(d, N_EXPERTS)
    hi = rw.astype(BF16)
    lo = (rw - hi.astype(F32)).astype(BF16)
    pad = jnp.zeros((d, LANES - N_EXPERTS), BF16)
    rwcat = jnp.concatenate([hi, pad, lo, pad], axis=1)
    rb = router_bias.reshape(N_GROUPS, EXPERTS_PER_GROUP).T.reshape(N_EXPERTS, 1)
    return rwcat, rb


def _dispatch_plan(counts, n_blocks):
    bm = DISPATCH_BLOCK
    gran = (counts + GRANULE - 1) // GRANULE
    slab_start = jnp.cumsum(gran, axis=1) - gran
    per_expert = jnp.sum(gran, axis=0) * GRANULE
    padded = (per_expert + bm - 1) // bm * bm
    pend = jnp.cumsum(padded)
    buf_start = (pend - padded) // GRANULE + jnp.cumsum(gran, axis=0) - gran
    tables = (gran.reshape(-1), slab_start.reshape(-1), buf_start.reshape(-1),
              jnp.sum(gran // 2, axis=1), jnp.sum(gran % 2, axis=1))
    tails = ((pend - padded + per_expert) // GRANULE, (padded - per_expert) // GRANULE)
    block_row = jnp.arange(n_blocks, dtype=jnp.int32) * bm
    block_e = jnp.minimum(jnp.sum(block_row[:, None] >= pend[None, :], axis=1), N_EXPERTS - 1)
    n_used = (pend[-1] // bm).reshape(1)
    as_i32 = lambda ts: tuple(t.astype(jnp.int32) for t in ts)
    return as_i32(tables), as_i32(tails), block_e.astype(jnp.int32), n_used.astype(jnp.int32)


def kernel(x, c, positions, ada_w, ada_b, w_in, q_norm, kv_norm, w_uq, w_ukv, w_o, ln1_g, ln1_b,
           router_w, router_bias, w_gate, w_up, w_down, ln2_g, ln2_b):
    batch, seq, d = x.shape
    n = batch * seq
    depth = ada_w.shape[0]
    assert seq % TOKEN_TILE == 0 and seq % ATT_TILE == 0

    mod_all = _ada_call(c, ada_w, ada_b).reshape(depth, batch, 6, d)
    cos_t, sin_t = _rope_call(positions)
    rwcat, rb = _router_weights(router_w, router_bias)
    t = ATT_TILE
    tri = (jnp.arange(t)[:, None] >= jnp.arange(t)[None, :]).astype(BF16)
    tm = TOKEN_TILE
    ut = (jnp.arange(tm)[:, None] < jnp.arange(tm)[None, :]).astype(BF16)
    max_rows = n * TOP_K + (n // tm) * N_EXPERTS * (GRANULE - 1) + N_EXPERTS * (DISPATCH_BLOCK - 1)
    n_blocks = -(-max_rows // DISPATCH_BLOCK)

    x2d = x.reshape(n, d)
    for l in range(depth):
        mod = mod_all[l]
        w1, w2, wuq, wukv, wo_sb, wo_mla = _layer_weights(w_in[l], w_uq[l], w_ukv[l], w_o[l])
        sbq, sbk, sbv, mq, mk, mv = _proj_call(x2d, mod, cos_t, sin_t, w1, w2, q_norm[l].reshape(1, -1),
                                               kv_norm[l].reshape(1, -1), wuq, wukv, seq)
        sb_out = _sb_call(sbq, sbk, sbv, tri, batch, seq)
        mla_out = _mla_call(mq, mk, mv, batch, seq)
        x1, h2, route, gates, counts = _post_call(sb_out, mla_out, x2d, mod, wo_sb, wo_mla,
                                                  ln1_g[l].reshape(1, d), ln1_b[l].reshape(1, d),
                                                  rwcat, rb, ut, seq)
        tables, tails, block_e, n_used = _dispatch_plan(counts[:, :, 0], n_blocks)
        buf = _dispatch_call(tables, tails, n_used, route, h2, n_blocks * DISPATCH_BLOCK)
        y_buf = _ffn_call(block_e, n_used, buf, w_gate, w_up, w_down, l)
        x2d = _combine_call(tables, x1, gates, mod, ln2_g[l].reshape(1, d), ln2_b[l].reshape(1, d), y_buf, seq)
    return x2d.reshape(batch, seq, d)
```

```python
import jax
import jax.numpy as jnp
from jax import lax
from jax.experimental import pallas as pl
from jax.experimental.pallas import tpu as pltpu

F32 = jnp.float32
BF16 = jnp.bfloat16

D_MODEL = 1024
DEPTH = 2
SB_HEADS = 8
SB_HEAD_DIM = 64
SB_WIDTH = SB_HEADS * SB_HEAD_DIM
MLA_HEADS = 8
MLA_NOPE_DIM = 64
MLA_ROPE_DIM = 32
MLA_V_DIM = 64
MLA_Q_RANK = 256
MLA_KV_RANK = 128
MLA_WIDTH = MLA_HEADS * MLA_V_DIM
ROPE_BASE = 10000.0
N_EXPERTS = 32
N_GROUPS = 8
EXPERTS_PER_GROUP = 4
TOP_K = 2
D_EXPERT = 256
DISPATCH_BLOCK = 512
DEEPNORM_ALPHA = (2 * DEPTH) ** 0.25
LN_EPS = 1e-5
RMS_EPS = 1e-6
LOG2_E = 1.4426950408889634
UNDERFLOW_LOG2 = 160.0

LANES = 128
HALF_ROPE = MLA_ROPE_DIM // 2
ROPE_LANE0 = MLA_NOPE_DIM
VMEM_LIMIT = 56 * 1024 * 1024

TOKEN_TILE = 512
ATT_TILE = 256
GRANULE = 16
GRANULE_SHIFT = 4
MLA_STEP_TILES = 9
SB_STEP_QUERY_TILES = 4
MXU_DIM = 256
SLAB_ROWS = -(-(TOKEN_TILE * TOP_K + N_EXPERTS * (GRANULE - 1)) // MXU_DIM) * MXU_DIM
SLAB_MAIN_ROWS = -(-(TOKEN_TILE * TOP_K + N_EXPERTS * GRANULE // 2) // MXU_DIM) * MXU_DIM


def _dot(a, b):
    return jnp.dot(a, b, preferred_element_type=F32)


def _dot_nt(a, b):
    return lax.dot_general(a, b, (((1,), (1,)), ((), ())), preferred_element_type=F32)


def _split_bf16(v):
    hi = v.astype(BF16)
    lo = (v - hi.astype(F32)).astype(BF16)
    return hi, lo


def _standardize(x):
    mu = jnp.mean(x, axis=-1, keepdims=True)
    xc = x - mu
    var = jnp.mean(xc * xc, axis=-1, keepdims=True)
    return xc * lax.rsqrt(var + LN_EPS)


def _ada_kernel(c_ref, w_ref, b_ref, o_ref):
    c = c_ref[...]
    ca = c * jax.nn.sigmoid(c)
    ca_hi, ca_lo = _split_bf16(ca)
    w_hi, w_lo = _split_bf16(w_ref[...])
    o_ref[...] = _dot(ca_hi, w_hi) + _dot(ca_lo, w_hi) + _dot(ca_hi, w_lo) + b_ref[...]


def _ada_call(c, ada_w, ada_b):
    depth, d, n6 = ada_w.shape
    b = c.shape[0]
    tn = 1536
    return pl.pallas_call(
        _ada_kernel,
        grid=(depth, n6 // tn),
        in_specs=[pl.BlockSpec((b, d), lambda l, j: (0, 0)),
                  pl.BlockSpec((None, d, tn), lambda l, j: (l, 0, j)),
                  pl.BlockSpec((None, 1, tn), lambda l, j: (l, 0, j))],
        out_specs=pl.BlockSpec((None, b, tn), lambda l, j: (l, 0, j)),
        out_shape=jax.ShapeDtypeStruct((depth, b, n6), F32),
        compiler_params=pltpu.CompilerParams(vmem_limit_bytes=VMEM_LIMIT),
        name="ada",
    )(c, ada_w, ada_b.reshape(depth, 1, n6))


def _rope_kernel(pos_ref, invf_ref, cos_ref, sin_ref):
    ang = pos_ref[...].astype(F32) * invf_ref[...]
    lane = lax.broadcasted_iota(jnp.int32, ang.shape, 1)
    rot = (lane >= ROPE_LANE0) & (lane < ROPE_LANE0 + MLA_ROPE_DIM)
    cos_ref[...] = jnp.where(rot, jnp.cos(ang), 1.0)
    sin_ref[...] = jnp.where(rot, jnp.sin(ang), 0.0)


def _rope_call(positions):
    n = positions.size
    tm = TOKEN_TILE
    inv_freq = ROPE_BASE ** (-jnp.arange(0, MLA_ROPE_DIM, 2, dtype=F32) / MLA_ROPE_DIM)
    invf = jnp.zeros((1, LANES), F32)
    invf = invf.at[0, ROPE_LANE0:ROPE_LANE0 + HALF_ROPE].set(inv_freq)
    invf = invf.at[0, ROPE_LANE0 + HALF_ROPE:ROPE_LANE0 + MLA_ROPE_DIM].set(inv_freq)
    return pl.pallas_call(
        _rope_kernel,
        grid=(n // tm,),
        in_specs=[pl.BlockSpec((tm, 1), lambda i: (i, 0)),
                  pl.BlockSpec((1, LANES), lambda i: (0, 0))],
        out_specs=[pl.BlockSpec((tm, LANES), lambda i: (i, 0)),
                   pl.BlockSpec((tm, LANES), lambda i: (i, 0))],
        out_shape=[jax.ShapeDtypeStruct((n, LANES), F32)] * 2,
        name="rope_tables",
    )(positions.reshape(n, 1), invf)


def _proj_kernel(x_ref, mod_ref, cos_ref, sin_ref, w1_ref, w2_ref, qn_ref, kvn_ref, wuq_ref,
                 wukv_ref, sbq_ref, sbk_ref, sbv_ref, mq_ref, mk_ref, mv_ref):
    mod = mod_ref[...]
    h = _standardize(x_ref[...]) * (1.0 + mod[1:2]) + mod[0:1]
    hb = h.astype(BF16)

    p1 = _dot(hb, w1_ref[...])
    sbq_ref[...] = (p1[:, :SB_WIDTH] * (SB_HEAD_DIM ** -0.5 * LOG2_E)).astype(BF16)
    sbk_ref[...] = p1[:, SB_WIDTH:2 * SB_WIDTH].astype(BF16)
    sbv_ref[...] = p1[:, 2 * SB_WIDTH:].astype(BF16)

    p2 = _dot(hb, w2_ref[...])
    q_lat = p2[:, :MLA_Q_RANK]
    kv_lat = p2[:, MLA_Q_RANK:MLA_Q_RANK + MLA_KV_RANK]
    k_rope = p2[:, MLA_Q_RANK + MLA_KV_RANK:]

    qn = q_lat * lax.rsqrt(jnp.mean(q_lat * q_lat, axis=-1, keepdims=True) + RMS_EPS) * qn_ref[...]
    kvn = kv_lat * lax.rsqrt(jnp.mean(kv_lat * kv_lat, axis=-1, keepdims=True) + RMS_EPS) * kvn_ref[...]
    q = _dot(qn.astype(BF16), wuq_ref[...])
    kv = _dot(kvn.astype(BF16), wukv_ref[...])

    cos = cos_ref[...]
    sin = sin_ref[...]
    lane = lax.broadcasted_iota(jnp.int32, cos.shape, 1)
    second = lane >= ROPE_LANE0 + HALF_ROPE
    sin_up = jnp.where(second, sin, 0.0)
    sin_dn = jnp.where(second, 0.0, -sin)

    def rope(t):
        return (t * cos + pltpu.roll(t, HALF_ROPE, 1) * sin_up
                + pltpu.roll(t, LANES - HALF_ROPE, 1) * sin_dn)

    kr = rope(k_rope)
    mla_scale = (MLA_NOPE_DIM + MLA_ROPE_DIM) ** -0.5 * LOG2_E
    for hd in range(MLA_HEADS):
        sl = slice(hd * LANES, (hd + 1) * LANES)
        mq_ref[:, sl] = (rope(q[:, sl]) * mla_scale).astype(BF16)
        mk_ref[:, sl] = (kv[:, sl] + kr).astype(BF16)
    vlane = lax.broadcasted_iota(jnp.int32, (1, MLA_HEADS * LANES), 1)
    quarter = lax.shift_right_logical(vlane, 6) & 3
    ones = jnp.where((quarter == 1) | (quarter == 2), 1.0, 0.0)
    mv_ref[...] = (kv[:, MLA_HEADS * LANES:] + ones).astype(BF16)


def _proj_call(x2d, mod, cos_t, sin_t, w1, w2, qn, kvn, wuq, wukv, seq):
    n, d = x2d.shape
    tm = TOKEN_TILE
    per_b = seq // tm
    tok = lambda i: (i, 0)
    full = lambda i: (0, 0)
    widths = (SB_WIDTH, SB_WIDTH, SB_WIDTH, MLA_HEADS * LANES, MLA_HEADS * LANES, MLA_HEADS * LANES)
    return pl.pallas_call(
        _proj_kernel,
        grid=(n // tm,),
        in_specs=[pl.BlockSpec((tm, d), tok),
                  pl.BlockSpec((None, 6, d), lambda i: (i // per_b, 0, 0)),
                  pl.BlockSpec((tm, LANES), tok),
                  pl.BlockSpec((tm, LANES), tok),
                  pl.BlockSpec(w1.shape, full),
                  pl.BlockSpec(w2.shape, full),
                  pl.BlockSpec(qn.shape, full),
                  pl.BlockSpec(kvn.shape, full),
                  pl.BlockSpec(wuq.shape, full),
                  pl.BlockSpec(wukv.shape, full)],
        out_specs=[pl.BlockSpec((tm, w), tok) for w in widths],
        out_shape=[jax.ShapeDtypeStruct((n, w), BF16) for w in widths],
        compiler_params=pltpu.CompilerParams(vmem_limit_bytes=VMEM_LIMIT),
        name="proj",
    )(x2d, mod, cos_t, sin_t, w1, w2, qn, kvn, wuq, wukv)


def _sb_kernel(q_ref, k_ref, v_ref, tri_ref, o_ref, carry_ref, acc_ref):
    t = ATT_TILE
    nq = q_ref.shape[0] // t
    tri = tri_ref[...]
    lane = lax.broadcasted_iota(jnp.int32, (1, LANES), 1)
    row = lax.broadcasted_iota(jnp.int32, (t, t), 0)
    col = lax.broadcasted_iota(jnp.int32, (t, t), 1)
    strict = col < row
    mine = [lane < SB_HEAD_DIM, lane >= SB_HEAD_DIM]

    def head_queries(rows):
        q = q_ref[rows, :]
        return [jnp.where(m, q, jnp.zeros_like(q)) for m in mine]

    def step(items, states, queries):
        chains = [(qk, j, diag, h) for qk, j, diag in items for h in range(2)]
        ks = [k_ref[pl.ds(j * t, t), :] for _, j, _ in items]
        zs = [_dot_nt(queries[qk][h], ks[n]) for n, (qk, _, _) in enumerate(items) for h in range(2)]
        nlks, pieces = [], []
        for (_, _, diag, _), z in zip(chains, zs):
            nlk = jnp.maximum(z, 0.0) + jnp.log(1.0 + jnp.exp2(-jnp.abs(z))) * LOG2_E
            if diag:
                nlk = jnp.where(strict, nlk, 0.0)
            nlks.append(nlk)
            pieces.append(nlk.astype(BF16))
        laters = [_dot(p, tri) for p in pieces]
        ncarry = {key: st[0] for key, st in states.items()}
        ws = []
        for (qk, j, diag, h), z, nlk, later in zip(chains, zs, nlks, laters):
            w = jnp.exp2((z - ncarry[(qk, h)]) - later)
            if diag:
                w = jnp.where(strict, w, 0.0)
            ws.append(w.astype(BF16))
            ncarry[(qk, h)] = ncarry[(qk, h)] + jnp.sum(nlk, axis=1, keepdims=True)
        acc = {key: st[1] for key, st in states.items()}
        for (qk, j, _, h), w in zip(chains, ws):
            acc[(qk, h)] = acc[(qk, h)] + _dot(w, v_ref[pl.ds(j * t, t), :])
        for key in states:
            states[key] = (ncarry[key], acc[key])

    def live(state):
        return jnp.minimum(jnp.min(state[0][0]), jnp.min(state[1][0])) < UNDERFLOW_LOG2

    def merged(state):
        return jnp.where(mine[0], state[0][1], state[1][1]).astype(o_ref.dtype)

    for first in range(0, nq, SB_STEP_QUERY_TILES):
        tiles = list(range(first, min(first + SB_STEP_QUERY_TILES, nq)))
        queries = {qi: head_queries(pl.ds(qi * t, t)) for qi in tiles}
        states = {(qi, h): (jnp.zeros((t, 1), F32), jnp.zeros((t, LANES), F32)) for qi in tiles for h in range(2)}
        items = [(qi, qi - d, d == 0) for qi in tiles for d in range(min(qi, 1) + 1)]
        step(items, states, queries)
        for qi in tiles:
            rows = pl.ds(qi * t, t)
            o_ref[rows, :] = merged((states[(qi, 0)], states[(qi, 1)]))
            for h in range(2):
                carry_ref[h, rows, :] = jnp.broadcast_to(states[(qi, h)][0], (t, LANES))
                acc_ref[h, rows, :] = states[(qi, h)][1]

    def finish(i, carry):
        rows = pl.ds(pl.multiple_of(i * t, t), t)
        ncarry = [carry_ref[h, rows, 0:1] for h in range(2)]

        @pl.when(live(((ncarry[0],), (ncarry[1],))))
        def _():
            queries = {0: head_queries(rows)}

            def dynamic_step(js, state):
                states = {(0, h): state[h] for h in range(2)}
                step([(0, j, False) for j in js], states, queries)
                return tuple(states[(0, h)] for h in range(2))

            def more(c):
                return (c[0] >= 1) & live(c[1])

            def pair(c):
                return c[0] - 2, dynamic_step((c[0], c[0] - 1), c[1])

            state = tuple((ncarry[h], acc_ref[h, rows, :]) for h in range(2))
            j, state = lax.while_loop(more, pair, (i - 2, state))
            state = lax.cond((j == 0) & live(state), lambda s: dynamic_step((0,), s), lambda s: s, state)
            o_ref[rows, :] = merged(state)

        return carry

    lax.fori_loop(2, nq, finish, 0)


def _sb_call(q, k, v, tri, batch, seq):
    t = ATT_TILE
    pairs = SB_WIDTH // LANES
    seq_spec = pl.BlockSpec((seq, LANES), lambda b, p: (b, p))
    return pl.pallas_call(
        _sb_kernel,
        grid=(batch, pairs),
        in_specs=[seq_spec, seq_spec, seq_spec,
                  pl.BlockSpec((t, t), lambda b, p: (0, 0))],
        out_specs=seq_spec,
        out_shape=jax.ShapeDtypeStruct(q.shape, BF16),
        scratch_shapes=[pltpu.VMEM((2, seq, LANES), F32), pltpu.VMEM((2, seq, LANES), F32)],
        compiler_params=pltpu.CompilerParams(vmem_limit_bytes=VMEM_LIMIT),
        name="sb_attention",
    )(q, k, v, tri)


def _mla_kernel(q_ref, k_ref, v_ref, o_ref):
    t = ATT_TILE
    lane = lax.broadcasted_iota(jnp.int32, (1, LANES), 1)
    row = lax.broadcasted_iota(jnp.int32, (t, t), 0)
    col = lax.broadcasted_iota(jnp.int32, (t, t), 1)
    causal = col <= row

    def head_lanes(hd):
        return slice(hd * LANES, (hd + 1) * LANES)

    def tile_rows(i):
        return slice(i * t, (i + 1) * t)

    def step(items, states):
        chains = [(qi, j, diag, hd) for qi, j, diag in items for hd in range(2)]
        ss = []
        for qi, j, diag, hd in chains:
            s = _dot_nt(q_ref[tile_rows(qi), head_lanes(hd)], k_ref[tile_rows(j), head_lanes(hd)])
            ss.append(jnp.where(causal, s, -jnp.inf) if diag else s)
        ps = [None] * len(chains)
        for key in sorted({(qi, hd) for qi, _, _, hd in chains}):
            mine_n = [n for n, (qi, _, _, hd) in enumerate(chains) if (qi, hd) == key]
            m, acc = states[key]
            m_new = m
            for n in mine_n:
                m_new = jnp.maximum(m_new, jnp.max(ss[n], axis=1, keepdims=True))
            for n in mine_n:
                ps[n] = jnp.exp2(ss[n] - m_new).astype(BF16)
            states[key] = (m_new, jnp.exp2(m - m_new) * acc)
        for n, (qi, j, _, hd) in enumerate(chains):
            m, acc = states[(qi, hd)]
            states[(qi, hd)] = (m, acc + _dot(ps[n], v_ref[tile_rows(j), head_lanes(hd)]))

    nq = q_ref.shape[0] // t
    groups = [(lo, nq - 1 - lo) for lo in range(nq // 2)] + ([(nq // 2,)] if nq % 2 else [])
    for group in groups:
        items = [(qi, qi - d, d == 0) for qi in reversed(group) for d in range(qi + 1)]
        states = {(qi, hd): (jnp.full((t, 1), -jnp.inf, F32), jnp.zeros((t, LANES), F32))
                  for qi in group for hd in range(2)}
        for first in range(0, len(items), MLA_STEP_TILES):
            step(items[first:first + MLA_STEP_TILES], states)
        for qi in group:
            acc0, acc1 = states[(qi, 0)][1], states[(qi, 1)][1]
            out = jnp.where(lane < MLA_V_DIM, acc0 / pltpu.roll(acc0, MLA_V_DIM, 1),
                            acc1 / pltpu.roll(acc1, MLA_V_DIM, 1))
            o_ref[tile_rows(qi), :] = out.astype(o_ref.dtype)


def _mla_call(q, k, v, batch, seq):
    pairs = MLA_WIDTH // LANES
    return pl.pallas_call(
        _mla_kernel,
        grid=(batch, pairs),
        in_specs=[pl.BlockSpec((seq, 2 * LANES), lambda b, p: (b, p)),
                  pl.BlockSpec((seq, 2 * LANES), lambda b, p: (b, p)),
                  pl.BlockSpec((seq, 2 * LANES), lambda b, p: (b, p))],
        out_specs=pl.BlockSpec((seq, LANES), lambda b, p: (b, p)),
        out_shape=jax.ShapeDtypeStruct((q.shape[0], MLA_WIDTH), BF16),
        compiler_params=pltpu.CompilerParams(vmem_limit_bytes=VMEM_LIMIT),
        name="mla_attention",
    )(q, k, v)


def _top2_sum(a, b, c, d):
    hi1, lo1 = jnp.maximum(a, b), jnp.minimum(a, b)
    hi2, lo2 = jnp.maximum(c, d), jnp.minimum(c, d)
    return jnp.maximum(hi1, hi2) + jnp.maximum(jnp.minimum(hi1, hi2), jnp.maximum(lo1, lo2))


def _first_argmax4(v):
    m = jnp.maximum(jnp.maximum(v[0], v[1]), jnp.maximum(v[2], v[3]))
    return jnp.where(v[0] == m, 0, jnp.where(v[1] == m, 1, jnp.where(v[2] == m, 2, 3)))


def _pick4(idx, v):
    return jnp.where(idx == 0, v[0], jnp.where(idx == 1, v[1], jnp.where(idx == 2, v[2], v[3])))


def _post_kernel(sb_ref, mla_ref, x_ref, mod_ref, wo_sb_ref, wo_mla_ref, g_ref, b_ref, rw_ref,
                 rb_ref, ut_ref, x1_ref, h2_ref, route_ref, gate_ref, cnt_ref):
    tm = x_ref.shape[0]
    mod = mod_ref[...]
    mix = _dot(sb_ref[...], wo_sb_ref[...]) + _dot(mla_ref[...], wo_mla_ref[...])
    x1 = _standardize(DEEPNORM_ALPHA * x_ref[...] + (1.0 + mod[2:3]) * mix) * g_ref[...] + b_ref[...]
    x1_ref[...] = x1
    h2 = _standardize(x1) * (1.0 + mod[4:5]) + mod[3:4]

    h_hi, h_lo = _split_bf16(h2)
    h2_ref[...] = h_hi
    rw = rw_ref[...]
    big = _dot(h_hi, rw)
    logits = big[:, :LANES] + big[:, LANES:] + _dot(h_lo, rw[:, :LANES])
    lt = logits.T[:N_EXPERTS]
    scores = jax.nn.sigmoid(lt)
    biased = scores + rb_ref[...]
    sc = [scores[N_GROUPS * p:N_GROUPS * (p + 1)] for p in range(EXPERTS_PER_GROUP)]
    bi = [biased[N_GROUPS * p:N_GROUPS * (p + 1)] for p in range(EXPERTS_PER_GROUP)]

    group_score = _top2_sum(*bi)
    gidx = lax.broadcasted_iota(jnp.int32, group_score.shape, 0)
    best = jnp.max(group_score, axis=0, keepdims=True)
    g_sel = jnp.min(jnp.where(group_score == best, gidx, N_GROUPS), axis=0, keepdims=True)
    in_sel = gidx == g_sel
    vb = [jnp.sum(jnp.where(in_sel, b, 0.0), axis=0, keepdims=True) for b in bi]
    vs = [jnp.sum(jnp.where(in_sel, s, 0.0), axis=0, keepdims=True) for s in sc]
    l1 = _first_argmax4(vb)
    vb2 = [jnp.where(l1 == p, -jnp.inf, vb[p]) for p in range(EXPERTS_PER_GROUP)]
    l2 = _first_argmax4(vb2)
    s1 = _pick4(l1, vs)
    s2 = _pick4(l2, vs)
    tot = s1 + s2
    e1 = g_sel * EXPERTS_PER_GROUP + l1
    e2 = g_sel * EXPERTS_PER_GROUP + l2

    eidx = lax.broadcasted_iota(jnp.int32, (N_EXPERTS, tm), 0)
    hit1 = eidx == e1
    hit2 = eidx == e2
    onehot = jnp.where(hit1 | hit2, 1.0, 0.0)
    before = _dot(onehot.astype(BF16), ut_ref[...])
    count = jnp.sum(onehot, axis=1, keepdims=True).astype(jnp.int32)
    granules = lax.shift_right_logical(count + (GRANULE - 1), GRANULE_SHIFT)
    lower = (lax.broadcasted_iota(jnp.int32, (N_EXPERTS, N_EXPERTS), 1)
             < lax.broadcasted_iota(jnp.int32, (N_EXPERTS, N_EXPERTS), 0))
    gran_f = jnp.broadcast_to(granules.astype(F32), (N_EXPERTS, LANES)).astype(BF16)
    start = _dot(jnp.where(lower, 1.0, 0.0).astype(BF16), gran_f)[:, 0:1] * GRANULE
    row_of = before + start
    p1 = jnp.sum(jnp.where(hit1, row_of, 0.0), axis=0, keepdims=True)
    p2 = jnp.sum(jnp.where(hit2, row_of, 0.0), axis=0, keepdims=True)
    cnt_ref[...] = jnp.broadcast_to(count, cnt_ref.shape)

    r8 = lax.broadcasted_iota(jnp.int32, (8, tm), 0)
    route_ref[...] = jnp.where(r8 == 0, p1.astype(jnp.int32), jnp.where(r8 == 1, p2.astype(jnp.int32), 0))
    r128 = lax.broadcasted_iota(jnp.int32, (LANES, tm), 0)
    cols = jnp.where(r128 == 0, s1 / tot, jnp.where(r128 == 1, s2 / tot,
                                                     jnp.where(r128 == 2, p1, jnp.where(r128 == 3, p2, 0.0))))
    gate_ref[...] = cols.T


def _post_call(sb_out, mla_out, x2d, mod, wo_sb, wo_mla, g, b, rw, rb, ut, seq):
    n, d = x2d.shape
    tm = TOKEN_TILE
    per_b = seq // tm
    tok = lambda i: (i, 0)
    full = lambda i: (0, 0)
    return pl.pallas_call(
        _post_kernel,
        grid=(n // tm,),
        in_specs=[pl.BlockSpec((tm, SB_WIDTH), tok),
                  pl.BlockSpec((tm, MLA_WIDTH), tok),
                  pl.BlockSpec((tm, d), tok),
                  pl.BlockSpec((None, 6, d), lambda i: (i // per_b, 0, 0)),
                  pl.BlockSpec(wo_sb.shape, full),
                  pl.BlockSpec(wo_mla.shape, full),
                  pl.BlockSpec((1, d), full),
                  pl.BlockSpec((1, d), full),
                  pl.BlockSpec(rw.shape, full),
                  pl.BlockSpec(rb.shape, full),
                  pl.BlockSpec(ut.shape, full)],
        out_specs=[pl.BlockSpec((tm, d), tok),
                   pl.BlockSpec((tm, d), tok),
                   pl.BlockSpec((8, tm), lambda i: (0, i)),
                   pl.BlockSpec((tm, LANES), tok),
                   pl.BlockSpec((None, N_EXPERTS, LANES), lambda i: (i, 0, 0))],
        out_shape=[jax.ShapeDtypeStruct((n, d), F32),
                   jax.ShapeDtypeStruct((n, d), BF16),
                   jax.ShapeDtypeStruct((8, n), jnp.int32),
                   jax.ShapeDtypeStruct((n, LANES), F32),
                   jax.ShapeDtypeStruct((n // tm, N_EXPERTS, LANES), jnp.int32)],
        compiler_params=pltpu.CompilerParams(vmem_limit_bytes=VMEM_LIMIT),
        name="post_attention",
    )(sb_out, mla_out, x2d, mod, wo_sb, wo_mla, g, b, rw, rb, ut)


def _granule_copy(src_ref, src_g, dst_ref, dst_g, sem, granules=1):
    rows = granules * GRANULE
    src = src_ref.at[pl.ds(pl.multiple_of(src_g * GRANULE, GRANULE), rows), :]
    dst = dst_ref.at[pl.ds(pl.multiple_of(dst_g * GRANULE, GRANULE), rows), :]
    return pltpu.make_async_copy(src, dst, sem)


def _start_segment(count, copy):
    def pair(p, carry):
        copy(2 * p, 2, 0).start()
        return carry

    lax.fori_loop(0, lax.shift_right_logical(count, 1), pair, 0)

    @pl.when((count & 1) == 1)
    def _():
        copy(count - 1, 1, 1).start()


def _wait_copies(count, copy):
    def wait(g, carry):
        copy().wait()
        return carry

    lax.fori_loop(0, count, wait, 0)


def _block_copy(zero_ref, buf_ref, block, sem):
    dst = buf_ref.at[pl.ds(pl.multiple_of(block * DISPATCH_BLOCK, DISPATCH_BLOCK), DISPATCH_BLOCK), :]
    return pltpu.make_async_copy(zero_ref, dst, sem)


def _dispatch_kernel(ng_ref, ls_ref, gs_ref, pairs_ref, singles_ref, tail_start_ref, tail_n_ref, n_used_ref,
                     route_ref, h_ref, buf_ref, slab_ref, zero_ref, sem):
    c = pl.program_id(0)
    last = pl.num_programs(0) - 1
    slot = c % 2
    tm = h_ref.shape[0]
    rows = slab_ref.shape[1]

    def wait_slab(tile, s):
        _wait_copies(pairs_ref[tile], lambda: _granule_copy(slab_ref.at[s], 0, buf_ref, 0, sem.at[2 * s], 2))
        _wait_copies(singles_ref[tile], lambda: _granule_copy(slab_ref.at[s], 0, buf_ref, 0, sem.at[2 * s + 1]))

    @pl.when(c == 0)
    def _():
        zero_ref[...] = jnp.zeros_like(zero_ref)

        def expert_tail(e, total):
            def granule(g, inner):
                _granule_copy(zero_ref, 0, buf_ref, tail_start_ref[e] + g, sem.at[4]).start()
                return inner

            lax.fori_loop(0, tail_n_ref[e], granule, 0)
            return total + tail_n_ref[e]

        n_tail = lax.fori_loop(0, N_EXPERTS, expert_tail, 0)
        n_blocks = buf_ref.shape[0] // DISPATCH_BLOCK

        def unused_block(b, carry):
            _block_copy(zero_ref, buf_ref, b, sem.at[5]).start()
            return carry

        lax.fori_loop(n_used_ref[0], n_blocks, unused_block, 0)
        _wait_copies(n_tail, lambda: _granule_copy(zero_ref, 0, buf_ref, 0, sem.at[4]))
        _wait_copies(n_blocks - n_used_ref[0], lambda: _block_copy(zero_ref, buf_ref, 0, sem.at[5]))

    @pl.when(c >= 2)
    def _():
        wait_slab(c - 2, slot)

    row = lax.broadcasted_iota(jnp.int32, (rows, tm), 0)
    route = route_ref[...]
    onehot = (row == route[0:1]) | (row == route[1:2])
    slab_ref[slot] = _dot(jnp.where(onehot, 1.0, 0.0).astype(BF16), h_ref[...]).astype(BF16)

    def expert(e, carry):
        idx = c * N_EXPERTS + e
        ls = ls_ref[idx]
        gs = gs_ref[idx]

        _start_segment(ng_ref[idx], lambda g, n, k: _granule_copy(
            slab_ref.at[slot], ls + g, buf_ref, gs + g, sem.at[2 * slot + k], n))
        return carry

    lax.fori_loop(0, N_EXPERTS, expert, 0)

    @pl.when(c == last)
    def _():
        wait_slab(c, slot)

        @pl.when(c >= 1)
        def _():
            wait_slab(c - 1, 1 - slot)


def _dispatch_call(tables, tails, n_used, route, h2, buf_rows):
    n, d = h2.shape
    tm = TOKEN_TILE
    grid_spec = pltpu.PrefetchScalarGridSpec(
        num_scalar_prefetch=8,
        grid=(n // tm,),
        in_specs=[pl.BlockSpec((8, tm), lambda i, *_: (0, i)),
                  pl.BlockSpec((tm, d), lambda i, *_: (i, 0))],
        out_specs=pl.BlockSpec(memory_space=pl.ANY),
        scratch_shapes=[pltpu.VMEM((2, SLAB_ROWS, d), BF16), pltpu.VMEM((DISPATCH_BLOCK, d), BF16),
                        pltpu.SemaphoreType.DMA((6,))],
    )
    return pl.pallas_call(
        _dispatch_kernel,
        grid_spec=grid_spec,
        out_shape=jax.ShapeDtypeStruct((buf_rows, d), BF16),
        compiler_params=pltpu.CompilerParams(dimension_semantics=("arbitrary",),
                                             vmem_limit_bytes=VMEM_LIMIT),
        name="dispatch",
    )(*tables, *tails, n_used, route, h2)


def _ffn_kernel(be_ref, nb_ref, x_ref, wg_ref, wu_ref, wd_ref, y_ref, wg_s, wu_s, wd_s):
    i = pl.program_id(0)
    prev = be_ref[jnp.maximum(i - 1, 0)]

    @pl.when((i == 0) | (be_ref[i] != prev))
    def _():
        wg_s[...] = wg_ref[...].astype(BF16)
        wu_s[...] = wu_ref[...].astype(BF16)
        wd_s[...] = wd_ref[...].astype(BF16)

    @pl.when(i < nb_ref[0])
    def _():
        xb = x_ref[...]
        g = _dot(xb, wg_s[...])
        u = _dot(xb, wu_s[...])
        a = g * jax.nn.sigmoid(g) * u
        y_ref[...] = _dot(a.astype(BF16), wd_s[...]).astype(y_ref.dtype)

    @pl.when(i >= nb_ref[0])
    def _():
        y_ref[...] = jnp.zeros_like(y_ref)


def _ffn_call(block_e, n_used, buf, w_gate, w_up, w_down, layer):
    rows, d = buf.shape
    bm = DISPATCH_BLOCK
    de = w_gate.shape[-1]
    grid_spec = pltpu.PrefetchScalarGridSpec(
        num_scalar_prefetch=2,
        grid=(rows // bm,),
        in_specs=[pl.BlockSpec((bm, d), lambda i, be, nb: (jnp.minimum(i, nb[0] - 1), 0)),
                  pl.BlockSpec((None, None, d, de), lambda i, be, nb: (layer, be[i], 0, 0)),
                  pl.BlockSpec((None, None, d, de), lambda i, be, nb: (layer, be[i], 0, 0)),
                  pl.BlockSpec((None, None, de, d), lambda i, be, nb: (layer, be[i], 0, 0))],
        out_specs=pl.BlockSpec((bm, d), lambda i, be, nb: (i, 0)),
        scratch_shapes=[pltpu.VMEM((d, de), BF16), pltpu.VMEM((d, de), BF16), pltpu.VMEM((de, d), BF16)],
    )
    return pl.pallas_call(
        _ffn_kernel,
        grid_spec=grid_spec,
        out_shape=jax.ShapeDtypeStruct((rows, d), BF16),
        compiler_params=pltpu.CompilerParams(dimension_semantics=("arbitrary",),
                                             vmem_limit_bytes=VMEM_LIMIT),
        name="expert_ffn",
    )(block_e, n_used, buf, w_gate, w_up, w_down)


def _combine_kernel(ng_ref, ls_ref, gs_ref, pairs_ref, singles_ref, x1_ref, gate_ref, mod_ref, g_ref, b_ref,
                    y_hbm_ref, o_ref, slab_ref, ffn_ref, sem):
    c = pl.program_id(0)
    last = pl.num_programs(0) - 1
    slot = c % 2
    tm = x1_ref.shape[0]
    rows = slab_ref.shape[1]

    def fetch(tile, into, first_expert, n_experts):
        def expert(e, carry):
            idx = tile * N_EXPERTS + e
            ls = ls_ref[idx]
            gs = gs_ref[idx]
            _start_segment(ng_ref[idx], lambda g, n, k: _granule_copy(
                y_hbm_ref, gs + g, slab_ref.at[into], ls + g, sem.at[2 * into + k], n))
            return carry

        lax.fori_loop(first_expert, first_expert + n_experts, expert, 0)

    def prefetch_next(part):
        @pl.when(c < last)
        def _():
            fetch(c + 1, 1 - slot, part * (N_EXPERTS // 4), N_EXPERTS // 4)

    @pl.when(c == 0)
    def _():
        slab_ref[...] = jnp.zeros_like(slab_ref)
        fetch(0, 0, 0, N_EXPERTS)

    _wait_copies(pairs_ref[c], lambda: _granule_copy(y_hbm_ref, 0, slab_ref.at[slot], 0, sem.at[2 * slot], 2))
    _wait_copies(singles_ref[c], lambda: _granule_copy(y_hbm_ref, 0, slab_ref.at[slot], 0, sem.at[2 * slot + 1]))

    prefetch_next(0)
    cols = gate_ref[...]

    def picked(k, lo, hi):
        lane = lax.broadcasted_iota(jnp.int32, (tm, hi - lo), 1) + lo
        pick = jnp.where(lane == cols[:, 2 + k:3 + k].astype(jnp.int32), 1.0, 0.0).astype(BF16)
        return cols[:, k:k + 1] * _dot(pick, slab_ref[slot, lo:hi, :])

    ffn_ref[...] = picked(0, 0, SLAB_MAIN_ROWS)
    prefetch_next(1)
    ffn_ref[...] += picked(1, 0, SLAB_MAIN_ROWS)

    @pl.when((2 * pairs_ref[c] + singles_ref[c]) * GRANULE > SLAB_MAIN_ROWS)
    def _():
        ffn_ref[...] += picked(0, SLAB_MAIN_ROWS, rows) + picked(1, SLAB_MAIN_ROWS, rows)

    prefetch_next(2)
    mod = mod_ref[...]
    o_ref[...] = (_standardize(DEEPNORM_ALPHA * x1_ref[...] + (1.0 + mod[5:6]) * ffn_ref[...]) * g_ref[...]
                  + b_ref[...])
    prefetch_next(3)


def _combine_call(tables, x1, gates, mod, g, b, y_buf, seq):
    n, d = x1.shape
    tm = TOKEN_TILE
    per_b = seq // tm
    tok = lambda i, *_: (i, 0)
    full = lambda i, *_: (0, 0)
    grid_spec = pltpu.PrefetchScalarGridSpec(
        num_scalar_prefetch=5,
        grid=(n // tm,),
        in_specs=[pl.BlockSpec((tm, d), tok),
                  pl.BlockSpec((tm, LANES), tok),
                  pl.BlockSpec((None, 6, d), lambda i, *_: (i // per_b, 0, 0)),
                  pl.BlockSpec((1, d), full),
                  pl.BlockSpec((1, d), full),
                  pl.BlockSpec(memory_space=pl.ANY)],
        out_specs=pl.BlockSpec((tm, d), tok),
        scratch_shapes=[pltpu.VMEM((2, SLAB_ROWS, d), BF16), pltpu.VMEM((tm, d), F32),
                        pltpu.SemaphoreType.DMA((4,))],
    )
    return pl.pallas_call(
        _combine_kernel,
        grid_spec=grid_spec,
        out_shape=jax.ShapeDtypeStruct((n, d), F32),
        compiler_params=pltpu.CompilerParams(dimension_semantics=("arbitrary",),
                                             vmem_limit_bytes=VMEM_LIMIT),
        name="combine",
    )(*tables, x1, gates, mod, g, b, y_buf)


def _layer_weights(w_in, w_uq, w_ukv, w_o):
    d = w_in.shape[0]
    w1 = w_in[:, :3 * SB_WIDTH].astype(BF16)
    lat = 3 * SB_WIDTH + MLA_Q_RANK + MLA_KV_RANK
    w2 = jnp.concatenate([w_in[:, 3 * SB_WIDTH:lat], jnp.zeros((d, ROPE_LANE0), F32), w_in[:, lat:],
                          jnp.zeros((d, LANES - ROPE_LANE0 - MLA_ROPE_DIM), F32)], axis=1).astype(BF16)
    uq = w_uq.reshape(MLA_Q_RANK, MLA_HEADS, MLA_NOPE_DIM + MLA_ROPE_DIM)
    uq = jnp.pad(uq, ((0, 0), (0, 0), (0, LANES - MLA_NOPE_DIM - MLA_ROPE_DIM)))
    wuq = uq.reshape(MLA_Q_RANK, MLA_HEADS * LANES).astype(BF16)
    ukv = w_ukv.reshape(MLA_KV_RANK, MLA_HEADS, MLA_NOPE_DIM + MLA_V_DIM)
    uk = jnp.pad(ukv[:, :, :MLA_NOPE_DIM], ((0, 0), (0, 0), (0, LANES - MLA_NOPE_DIM)))
    uv = ukv[:, :, MLA_NOPE_DIM:]
    even_head = (jnp.arange(MLA_HEADS) % 2 == 0)[None, :, None]
    uv = jnp.where(even_head, jnp.pad(uv, ((0, 0), (0, 0), (0, LANES - MLA_V_DIM))),
                   jnp.pad(uv, ((0, 0), (0, 0), (LANES - MLA_V_DIM, 0))))
    wukv = jnp.concatenate([uk.reshape(MLA_KV_RANK, MLA_HEADS * LANES),
                            uv.reshape(MLA_KV_RANK, MLA_HEADS * LANES)], axis=1).astype(BF16)
    return w1, w2, wuq, wukv, w_o[:SB_WIDTH].astype(BF16), w_o[SB_WIDTH:].astype(BF16)


def _router_weights(router_w, router_bias):
    d = router_w.shape[0]
    rw = router_w.reshape(d, N_GROUPS, EXPERTS_PER_GROUP).transpose(0, 2, 1).reshape(d, N_EXPERTS)
    hi = rw.astype(BF16)
    lo = (rw - hi.astype(F32)).astype(BF16)
    pad = jnp.zeros((d, LANES - N_EXPERTS), BF16)
    rwcat = jnp.concatenate([hi, pad, lo, pad], axis=1)
    rb = router_bias.reshape(N_GROUPS, EXPERTS_PER_GROUP).T.reshape(N_EXPERTS, 1)
    return rwcat, rb


def _dispatch_plan(counts, n_blocks):
    bm = DISPATCH_BLOCK
    gran = (counts + GRANULE - 1) // GRANULE
    slab_start = jnp.cumsum(gran, axis=1) - gran
    per_expert = jnp.sum(gran, axis=0) * GRANULE
    padded = (per_expert + bm - 1) // bm * bm
    pend = jnp.cumsum(padded)
    buf_start = (pend - padded) // GRANULE + jnp.cumsum(gran, axis=0) - gran
    tables = (gran.reshape(-1), slab_start.reshape(-1), buf_start.reshape(-1),
              jnp.sum(gran // 2, axis=1), jnp.sum(gran % 2, axis=1))
    tails = ((pend - padded + per_expert) // GRANULE, (padded - per_expert) // GRANULE)
    block_row = jnp.arange(n_blocks, dtype=jnp.int32) * bm
    block_e = jnp.minimum(jnp.sum(block_row[:, None] >= pend[None, :], axis=1), N_EXPERTS - 1)
    n_used = (pend[-1] // bm).reshape(1)
    as_i32 = lambda ts: tuple(t.astype(jnp.int32) for t in ts)
    return as_i32(tables), as_i32(tails), block_e.astype(jnp.int32), n_used.astype(jnp.int32)


def kernel(x, c, positions, ada_w, ada_b, w_in, q_norm, kv_norm, w_uq, w_ukv, w_o, ln1_g, ln1_b,
           router_w, router_bias, w_gate, w_up, w_down, ln2_g, ln2_b):
    batch, seq, d = x.shape
    n = batch * seq
    depth = ada_w.shape[0]
    assert seq % TOKEN_TILE == 0 and seq % ATT_TILE == 0

    mod_all = _ada_call(c, ada_w, ada_b).reshape(depth, batch, 6, d)
    cos_t, sin_t = _rope_call(positions)
    rwcat, rb = _router_weights(router_w, router_bias)
    t = ATT_TILE
    tri = (jnp.arange(t)[:, None] >= jnp.arange(t)[None, :]).astype(BF16)
    tm = TOKEN_TILE
    ut = (jnp.arange(tm)[:, None] < jnp.arange(tm)[None, :]).astype(BF16)
    max_rows = n * TOP_K + (n // tm) * N_EXPERTS * (GRANULE - 1) + N_EXPERTS * (DISPATCH_BLOCK - 1)
    n_blocks = -(-max_rows // DISPATCH_BLOCK)

    x2d = x.reshape(n, d)
    for l in range(depth):
        mod = mod_all[l]
        w1, w2, wuq, wukv, wo_sb, wo_mla = _layer_weights(w_in[l], w_uq[l], w_ukv[l], w_o[l])
        sbq, sbk, sbv, mq, mk, mv = _proj_call(x2d, mod, cos_t, sin_t, w1, w2, q_norm[l].reshape(1, -1),
                                               kv_norm[l].reshape(1, -1), wuq, wukv, seq)
        sb_out = _sb_call(sbq, sbk, sbv, tri, batch, seq)
        mla_out = _mla_call(mq, mk, mv, batch, seq)
        x1, h2, route, gates, counts = _post_call(sb_out, mla_out, x2d, mod, wo_sb, wo_mla,
                                                  ln1_g[l].reshape(1, d), ln1_b[l].reshape(1, d),
                                                  rwcat, rb, ut, seq)
        tables, tails, block_e, n_used = _dispatch_plan(counts[:, :, 0], n_blocks)
        buf = _dispatch_call(tables, tails, n_used, route, h2, n_blocks * DISPATCH_BLOCK)
        y_buf = _ffn_call(block_e, n_used, buf, w_gate, w_up, w_down, l)
        x2d = _combine_call(tables, x1, gates, mod, ln2_g[l].reshape(1, d), ln2_b[l].reshape(1, d), y_buf, seq)
    return x2d.reshape(batch, seq, d)
```
